```python
import math
import jax, jax.numpy as jnp
from jax import lax
import numpy as np

D_MODEL = 1024
BATCH = 8
SEQ = 2048
DEPTH = 2
DEC_BATCH = 32
DEC_SEQ = 1
PAST_LEN = 8192
PAGE_SIZE = 128

F32 = jnp.float32
EPS = 1e-6
N_META = 16
N_BRANCH = 4
BRANCH_W = D_MODEL // N_BRANCH
HA = 4
DVA = BRANCH_W // HA
DQA = DVA // 2
QBLK = 128
HB = 4
DKB = BRANCH_W // HB
DVB = BRANCH_W // HB
RET_CHUNK = 128
ROPE_BASE = 10000.0
WC = BRANCH_W
HC = 4
BWC = WC // HC
CONV_W = 4
LRU_C = 8.0
HD = 4
DVD = BRANCH_W // HD
DKD = DVD // 2
GLA_RANK = 16
GLA_TAU = 16.0
GLA_CHUNK = 16
N_GROUPS = 4
EXP_PER_GROUP = 4
N_EXPERTS = N_GROUPS * EXP_PER_GROUP
TOP_K_IN = 2
D_EXPERT = 256
SPLITS = (HA * 2 * DQA, HA * 2 * DQA, HA * DVA,
          HB * DKB, HB * DKB, HB * DVB, HB * DVB,
          WC, WC,
          HD * DKD, HD * DKD, HD * DVD, HD * DVD, GLA_RANK)
IN_W = sum(SPLITS)

kernel_name = 'hybrid_diffattn_retention_rglru_gla_hmoe_step'


def rms_norm(x, g):
    xf = x.astype(F32)
    y = xf * lax.rsqrt(jnp.mean(xf * xf, axis=-1, keepdims=True) + EPS)
    return (y * g.astype(F32)).astype(x.dtype)


def group_norm(x, g):
    xf = x.astype(F32)
    mu = jnp.mean(xf, axis=-1, keepdims=True)
    var = jnp.mean(jnp.square(xf - mu), axis=-1, keepdims=True)
    return ((xf - mu) * lax.rsqrt(var + EPS) * g.astype(F32)).astype(x.dtype)


def project(xn, w_in):
    cuts = [int(c) for c in np.cumsum(SPLITS)[:-1]]
    return jnp.split(xn @ w_in, cuts, axis=-1)


def left_pad(t, pad):
    return jnp.pad(t, [(0, 0), (pad, 0)] + [(0, 0)] * (t.ndim - 2))


def chunked(fn, s0, chunk, pad, *xs):
    xs = [left_pad(t, pad) for t in xs]
    b, lp = xs[0].shape[:2]
    nc = lp // chunk
    xc = tuple(t.reshape((b, nc, chunk) + t.shape[2:]).swapaxes(0, 1) for t in xs)
    s, o = lax.scan(lambda s, c: fn(s, *c), s0, xc)
    o = o.swapaxes(0, 1).reshape((b, lp) + o.shape[3:])
    return s, o[:, pad:]


def diff_attn_block(q, k, v, q_pos, k_pos, lam):
    s = jnp.einsum('bqhmd,bkhmd->bhmqk', q, k)
    s = jnp.where(k_pos[None, :] <= q_pos[:, None], s, -jnp.inf)
    p = jax.nn.softmax(s, axis=-1)
    w = p[:, :, 0] - lam * p[:, :, 1]
    return jnp.einsum('bhqk,bkhv->bqhv', w.astype(v.dtype), v)


def diff_attn_prompt(q, k, v, lam):
    b, l = q.shape[:2]
    nb = -(-l // QBLK)
    qp = jnp.pad(q, ((0, 0), (0, nb * QBLK - l), (0, 0), (0, 0), (0, 0)))
    qb = qp.reshape(b, nb, QBLK, HA, 2, DQA).swapaxes(0, 1)
    k_pos = jnp.arange(l)

    def one(args):
        qi, start = args
        return diff_attn_block(qi, k, v, start + jnp.arange(QBLK), k_pos, lam)

    o = lax.map(one, (qb, jnp.arange(nb) * QBLK))
    return o.swapaxes(0, 1).reshape(b, nb * QBLK, HA, DVA)[:, :l]


def rotary(x, pos):
    d = x.shape[-1]
    inv = ROPE_BASE ** (-jnp.linspace(0.0, 1.0, d // 2))
    ang = pos.astype(F32)[:, None] * inv[None]
    cos = jnp.cos(ang)[None, :, None, :]
    sin = jnp.sin(ang)[None, :, None, :]
    x1, x2 = x[..., :d // 2], x[..., d // 2:]
    return jnp.concatenate([x1 * cos - x2 * sin, x1 * sin + x2 * cos], axis=-1)


def ret_log_gamma():
    return jnp.log(1.0 - 2.0 ** (-5.0 - jnp.arange(HB, dtype=F32)))


def ret_chunk(s0, q, k, v):
    t_len = q.shape[1]
    lg = ret_log_gamma()
    t = jnp.arange(t_len, dtype=F32)
    rel = t[:, None] - t[None, :]
    dmat = jnp.where(rel >= 0, jnp.exp(jnp.maximum(rel, 0.0)[None] * lg[:, None, None]), 0.0)
    a = jnp.einsum('bthk,bjhk->bhtj', q, k) * dmat
    o = jnp.einsum('bhtj,bjhv->bthv', a, v)
    o = o + jnp.einsum('bthk,bhkv->bthv', q, s0) * jnp.exp((t[:, None] + 1.0) * lg[None])[None, :, :, None]
    kd = k * jnp.exp((t_len - 1.0 - t)[:, None] * lg[None])[None, :, :, None]
    s = jnp.exp(t_len * lg)[None, :, None, None] * s0 + jnp.einsum('bjhk,bjhv->bhkv', kd, v)
    return s, o


def causal_conv(x, buf, w, b):
    t_len = x.shape[1]
    xx = jnp.concatenate([buf.astype(x.dtype), x], axis=1)
    y = b + xx[:, 0:t_len] * w[0]
    for i in range(1, CONV_W):
        y = y + xx[:, i:i + t_len] * w[i]
    return y, xx[:, -(CONV_W - 1):]


def rglru(x, h0, wa, ba, wx, bx, lam):
    b, t_len = x.shape[:2]
    xb = x.reshape(b, t_len, HC, BWC)
    r = jax.nn.sigmoid(jnp.einsum('bthi,hij->bthj', xb, wa).reshape(b, t_len, WC) + ba)
    i = jax.nn.sigmoid(jnp.einsum('bthi,hij->bthj', xb, wx).reshape(b, t_len, WC) + bx)
    log_a = -LRU_C * r * jax.nn.softplus(-lam)
    a = jnp.exp(log_a)
    u = jnp.sqrt(-jnp.expm1(2.0 * log_a)) * (i * x)

    def comb(left, right):
        al, bl = left
        ar, br = right
        return al * ar, ar * bl + br

    a_cum, b_cum = lax.associative_scan(comb, (a, u), axis=1)
    return a_cum * h0[:, None] + b_cum


def gla_chunk(s0, q, k, v, la):
    t_len = q.shape[1]
    bc = jnp.cumsum(la, axis=1)
    causal = jnp.tril(jnp.ones((t_len, t_len), dtype=bool))
    diff = bc[:, :, None] - bc[:, None, :]
    dec = jnp.exp(jnp.where(causal[None, :, :, None, None], diff, -jnp.inf))
    a = jnp.einsum('bthk,bjhk,btjhk->bhtj', q, k, dec)
    o = jnp.einsum('bhtj,bjhv->bthv', a, v) + jnp.einsum('bthk,bhkv->bthv', q * jnp.exp(bc), s0)
    b_last = bc[:, -1:]
    s = jnp.exp(bc[:, -1])[..., None] * s0 + jnp.einsum('bjhk,bjhv->bhkv', k * jnp.exp(b_last - bc), v)
    return s, o


def hier_moe(x, rg_w, rg_b, re_w, re_b, w1, w3, w2):
    b, l, d = x.shape
    t = x.reshape(-1, d)
    pg = jax.nn.softmax((t @ rg_w + rg_b).astype(F32), axis=-1)
    pg_top, g_idx = lax.top_k(pg, 1)
    g_oh = jax.nn.one_hot(g_idx[:, 0], N_GROUPS, dtype=F32)
    le = (t @ re_w + re_b).astype(F32).reshape(-1, N_GROUPS, EXP_PER_GROUP)
    le_sel = jnp.einsum('ng,nge->ne', g_oh, le)
    e_val, e_idx = lax.top_k(le_sel, TOP_K_IN)
    wts = jax.nn.softmax(e_val, axis=-1) * pg_top
    in_gate = jnp.sum(jax.nn.one_hot(e_idx, EXP_PER_GROUP, dtype=F32) * wts[..., None], axis=1)
    gate = (g_oh[:, :, None] * in_gate[:, None, :]).reshape(-1, N_EXPERTS)
    h = jax.nn.silu(jnp.einsum('nd,edf->nef', t, w1)) * jnp.einsum('nd,edf->nef', t, w3)
    h = h * gate[..., None].astype(h.dtype)
    return jnp.einsum('nef,efd->nd', h, w2).reshape(b, l, d)


def layer(x, l, p, pos0, kv_past, ret_s0, conv_buf, lru_h0, gla_s0):
    b, l_len, _ = x.shape
    prompt = kv_past is None
    xn = rms_norm(x, p['norm1'])
    aq, ak, av, bq, bk, bv, bg, cx, cg, dq, dk, dv, dr, da = project(xn, p['w_in'])

    qa = rms_norm(aq.reshape(b, l_len, HA, 2, DQA), p['qn']).astype(F32) * (DQA ** -0.5)
    ka = rms_norm(ak.reshape(b, l_len, HA, 2, DQA), p['kn'])
    va = av.reshape(b, l_len, HA, DVA)
    lam_init = 0.8 - 0.6 * math.exp(-0.3 * l)
    lv = p['lam'].astype(F32)
    lam = jnp.exp(jnp.sum(lv[0] * lv[1])) - jnp.exp(jnp.sum(lv[2] * lv[3])) + lam_init
    if prompt:
        oa = diff_attn_prompt(qa, ka.astype(F32), va, lam)
    else:
        k_past, v_past = kv_past
        k_all = jnp.concatenate([k_past.astype(F32), ka.astype(F32)], axis=1)
        v_all = jnp.concatenate([v_past.astype(va.dtype), va], axis=1)
        oa = diff_attn_block(qa, k_all, v_all, pos0 + jnp.arange(l_len), jnp.arange(pos0 + l_len), lam)
    oa = (rms_norm(oa, p['an']) * (1.0 - lam_init)).reshape(b, l_len, BRANCH_W).astype(x.dtype)

    pos = pos0 + jnp.arange(l_len)
    qb = rotary(bq.reshape(b, l_len, HB, DKB).astype(F32), pos)
    kb = rotary(bk.reshape(b, l_len, HB, DKB).astype(F32), pos) * (DKB ** -0.5)
    vb = bv.reshape(b, l_len, HB, DVB).astype(F32)
    chunk, pad = (RET_CHUNK, (-N_META) % RET_CHUNK) if prompt else (l_len, 0)
    s_ret, ob = chunked(ret_chunk, ret_s0, chunk, pad, qb, kb, vb)
    ob = (group_norm(ob, p['rn']).reshape(b, l_len, BRANCH_W) * jax.nn.silu(bg.astype(F32))).astype(x.dtype)

    xc, conv_new = causal_conv(cx.astype(F32), conv_buf, p['conv_w'], p['conv_b'])
    h = rglru(xc, lru_h0, p['lru_wa'], p['lru_ba'], p['lru_wx'], p['lru_bx'], p['lru_lam'])
    oc = (h * jax.nn.gelu(cg.astype(F32))).astype(x.dtype)

    qd = dq.reshape(b, l_len, HD, DKD).astype(F32) * (DKD ** -0.5)
    kd = dk.reshape(b, l_len, HD, DKD).astype(F32)
    vd = dv.reshape(b, l_len, HD, DVD).astype(F32)
    la = (jax.nn.log_sigmoid(da.astype(F32) @ p['gla_w2'].astype(F32) + p['gla_b']) / GLA_TAU).reshape(b, l_len, HD, DKD)
    chunk, pad = (GLA_CHUNK, (-N_META) % GLA_CHUNK) if prompt else (l_len, 0)
    s_gla, od = chunked(gla_chunk, gla_s0, chunk, pad, qd, kd, vd, la)
    od = (rms_norm(od, p['gn']).reshape(b, l_len, BRANCH_W) * jax.nn.silu(dr.astype(F32))).astype(x.dtype)

    gates = jax.nn.sigmoid(xn @ p['w_merge'] + p['b_merge']).reshape(b, l_len, N_BRANCH, D_MODEL)
    br = jnp.stack([oa, ob, oc, od], axis=2)
    proj = jnp.einsum('blnw,nwd->blnd', br, p['w_branch'])
    hres = x + jnp.sum(gates * proj, axis=2) @ p['w_out']

    y = hres + hier_moe(rms_norm(hres, p['norm2']), p['rg_w'], p['rg_b'], p['re_w'], p['re_b'],
                        p['ex_w1'], p['ex_w3'], p['ex_w2'])
    new = (ka.reshape(b, l_len, HA, 2 * DQA), va, s_ret, h[:, -1], conv_new, s_gla)
    return y, new


def setup_inputs(seed: int = 0) -> dict:
    key = jax.random.key(seed)
    ks = jax.random.split(key, 40)

    def nrm(i, shape, scale):
        return jax.random.normal(ks[i], shape, F32) * scale

    def gain(i, shape):
        return 1.0 + 0.02 * jax.random.normal(ks[i], shape, F32)

    n_pages = PAST_LEN // PAGE_SIZE
    n_used = DEC_BATCH * n_pages
    n_phys = n_used + max(1, n_used // 4)
    page_table = jax.random.permutation(ks[0], n_phys)[:n_used].reshape(DEC_BATCH, n_pages).astype(jnp.int32)
    u = jax.random.uniform(ks[1], (DEPTH, WC), F32, 0.9, 0.999)
    s = u ** (1.0 / LRU_C)
    lru_lam = jnp.log(s) - jnp.log1p(-s)
    return {
        'x_prompt': nrm(2, (BATCH, SEQ, D_MODEL), 1.0),
        'x_sample': nrm(3, (DEC_BATCH, DEC_SEQ, D_MODEL), 1.0),
        'cache_k_a': nrm(4, (DEPTH, n_phys, PAGE_SIZE, HA, 2 * DQA), 1.0),
        'cache_v_a': nrm(5, (DEPTH, n_phys, PAGE_SIZE, HA, DVA), 1.0),
        'state_ret': nrm(6, (DEPTH, DEC_BATCH, HB, DKB, DVB), 0.5),
        'state_lru_h': nrm(7, (DEPTH, DEC_BATCH, WC), 0.5),
        'state_lru_conv': nrm(8, (DEPTH, DEC_BATCH, CONV_W - 1, WC), 1.0),
        'state_gla': nrm(9, (DEPTH, DEC_BATCH, HD, DKD, DVD), 0.5),
        'page_table': page_table,
        'meta': nrm(10, (N_META, D_MODEL), 1.0),
        'norm1': gain(11, (DEPTH, D_MODEL)),
        'w_in': nrm(12, (DEPTH, D_MODEL, IN_W), D_MODEL ** -0.5),
        'qn': gain(13, (DEPTH, DQA)),
        'kn': gain(14, (DEPTH, DQA)),
        'lam': nrm(15, (DEPTH, 4, DQA), 0.1),
        'an': gain(16, (DEPTH, DVA)),
        'rn': gain(17, (DEPTH, DVB)),
        'conv_w': nrm(18, (DEPTH, CONV_W, WC), CONV_W ** -0.5),
        'conv_b': nrm(19, (DEPTH, WC), 0.01),
        'lru_wa': nrm(20, (DEPTH, HC, BWC, BWC), BWC ** -0.5),
        'lru_ba': nrm(21, (DEPTH, WC), 0.01),
        'lru_wx': nrm(22, (DEPTH, HC, BWC, BWC), BWC ** -0.5),
        'lru_bx': nrm(23, (DEPTH, WC), 0.01),
        'lru_lam': lru_lam,
        'gla_w2': nrm(24, (DEPTH, GLA_RANK, HD * DKD), GLA_RANK ** -0.5),
        'gla_b': nrm(25, (DEPTH, HD * DKD), 0.01),
        'gn': gain(26, (DEPTH, DVD)),
        'w_branch': nrm(27, (DEPTH, N_BRANCH, BRANCH_W, D_MODEL), BRANCH_W ** -0.5),
        'w_merge': nrm(28, (DEPTH, D_MODEL, N_BRANCH * D_MODEL), D_MODEL ** -0.5),
        'b_merge': nrm(29, (DEPTH, N_BRANCH * D_MODEL), 0.01),
        'w_out': nrm(30, (DEPTH, D_MODEL, D_MODEL), D_MODEL ** -0.5),
        'norm2': gain(31, (DEPTH, D_MODEL)),
        'rg_w': nrm(32, (DEPTH, D_MODEL, N_GROUPS), D_MODEL ** -0.5),
        'rg_b': nrm(33, (DEPTH, N_GROUPS), 0.01),
        're_w': nrm(34, (DEPTH, D_MODEL, N_EXPERTS), D_MODEL ** -0.5),
        're_b': nrm(35, (DEPTH, N_EXPERTS), 0.01),
        'ex_w1': nrm(36, (DEPTH, N_EXPERTS, D_MODEL, D_EXPERT), D_MODEL ** -0.5),
        'ex_w3': nrm(37, (DEPTH, N_EXPERTS, D_MODEL, D_EXPERT), D_MODEL ** -0.5),
        'ex_w2': nrm(38, (DEPTH, N_EXPERTS, D_EXPERT, D_MODEL), D_EXPERT ** -0.5),
    }


def reference(x_prompt, x_sample, cache_k_a, cache_v_a, state_ret, state_lru_h, state_lru_conv, state_gla,
              page_table, meta, norm1, w_in, qn, kn, lam, an, rn, conv_w, conv_b, lru_wa, lru_ba, lru_wx,
              lru_bx, lru_lam, gla_w2, gla_b, gn, w_branch, w_merge, b_merge, w_out, norm2, rg_w, rg_b,
              re_w, re_b, ex_w1, ex_w3, ex_w2):
    bp, bd = x_prompt.shape[0], x_sample.shape[0]
    xp = jnp.concatenate([jnp.broadcast_to(meta.astype(x_prompt.dtype)[None], (bp, N_META, D_MODEL)), x_prompt], axis=1)
    xs = x_sample
    new_p, new_s = [], []
    for l in range(DEPTH):
        p = dict(norm1=norm1[l], w_in=w_in[l], qn=qn[l], kn=kn[l], lam=lam[l], an=an[l], rn=rn[l],
                 conv_w=conv_w[l], conv_b=conv_b[l], lru_wa=lru_wa[l], lru_ba=lru_ba[l], lru_wx=lru_wx[l],
                 lru_bx=lru_bx[l], lru_lam=lru_lam[l], gla_w2=gla_w2[l], gla_b=gla_b[l], gn=gn[l],
                 w_branch=w_branch[l], w_merge=w_merge[l], b_merge=b_merge[l], w_out=w_out[l], norm2=norm2[l],
                 rg_w=rg_w[l], rg_b=rg_b[l], re_w=re_w[l], re_b=re_b[l], ex_w1=ex_w1[l], ex_w3=ex_w3[l],
                 ex_w2=ex_w2[l])
        xp, st = layer(xp, l, p, 0, None,
                       jnp.zeros((bp, HB, DKB, DVB), F32), jnp.zeros((bp, CONV_W - 1, WC), F32),
                       jnp.zeros((bp, WC), F32), jnp.zeros((bp, HD, DKD, DVD), F32))
        new_p.append(st)
        kv_past = (cache_k_a[l][page_table].reshape(bd, -1, HA, 2, DQA),
                   cache_v_a[l][page_table].reshape(bd, -1, HA, DVA))
        xs, st = layer(xs, l, p, PAST_LEN, kv_past, state_ret[l].astype(F32), state_lru_conv[l].astype(F32),
                       state_lru_h[l].astype(F32), state_gla[l].astype(F32))
        new_s.append(st)

    def stk(states, i, dt):
        return jnp.stack([s[i] for s in states]).astype(dt)

    dp, ds = x_prompt.dtype, x_sample.dtype
    return (xp[:, N_META:], xs,
            stk(new_p, 0, dp), stk(new_p, 1, dp), stk(new_p, 2, dp), stk(new_p, 3, dp), stk(new_p, 4, dp), stk(new_p, 5, dp),
            stk(new_s, 0, ds), stk(new_s, 1, ds), stk(new_s, 2, ds), stk(new_s, 3, ds), stk(new_s, 4, ds), stk(new_s, 5, ds))
```

```python
import functools
import math

import numpy as np
import jax
import jax.numpy as jnp
from jax import lax
from jax.experimental import pallas as pl
from jax.experimental.pallas import tpu as pltpu

F32 = jnp.float32
BF16 = jnp.bfloat16
EPS = 1e-6
D_MODEL = 1024
N_META = 16
PAGE = 128
BLK = 128
PADL = BLK - N_META
HEADS = 4
BW = 256
DQA = 32
ROPE_BASE = 10000.0
CONV_W = 4
LRU_C = 8.0
GLA_RANK = 16
GLA_TAU = 16.0
GLA_CHUNK = 16
N_GROUPS = 4
EXP_PER_GROUP = 4
N_EXPERTS = 16
D_EXPERT = 256
W_A, W_B, W_C, W_D = 768, 1024, 512, 896
QK_PACK = 8 * 128
V_PACK = HEADS * 128
VMEM_LIMIT = 56 * 1024 * 1024
NEG = -1e30
PAGES_PER_STEP = 16
ATTN_KEY_STEP = 512


def _cparams(sem):
    return pltpu.CompilerParams(dimension_semantics=sem, vmem_limit_bytes=VMEM_LIMIT)


def _layer_spec(arr, layer):
    nd = arr.ndim
    return pl.BlockSpec((1,) + arr.shape[1:], lambda *_: (layer,) + (0,) * (nd - 1),
                        pipeline_mode=pl.Buffered(1))


def _whole_spec(arr):
    nd = arr.ndim
    return pl.BlockSpec(arr.shape, lambda *_: (0,) * nd, pipeline_mode=pl.Buffered(1))


def _bf16_prefix(x):
    bits = lax.bitcast_convert_type(x, jnp.uint32) & jnp.uint32(0xFFFF0000)
    return lax.bitcast_convert_type(bits, F32)


def _split(x):
    hi = _bf16_prefix(x)
    return hi.astype(BF16), (x - hi).astype(BF16)


def _split3(x):
    hi = _bf16_prefix(x)
    r = x - hi
    mid = _bf16_prefix(r)
    return hi.astype(BF16), mid.astype(BF16), (r - mid).astype(BF16)


def _bdot(a, b, dims=None):
    if dims is None:
        return jnp.dot(a, b, preferred_element_type=F32)
    return lax.dot_general(a, b, (dims, ((), ())), preferred_element_type=F32)


def _operand(a, prec):
    if prec == 2:
        return a
    return _split(a) if prec == 1 else a.astype(BF16)


def _wdot(a, w_ref, idx, prec):
    get = lambda part: w_ref[(0, part) + idx]
    if prec == 2:
        return jnp.dot(a, get(0), precision=lax.Precision.HIGHEST, preferred_element_type=F32)
    if prec == 1:
        a_hi, a_lo = a if isinstance(a, tuple) else _split(a)
        return _bdot(a_hi, get(0)) + _bdot(a_lo, get(0)) + _bdot(a_hi, get(1))
    return _bdot(a.astype(BF16), get(0))


def _adot(a, b, prec, dims=None):
    if prec == 1:
        a_hi, a_lo = _split(a)
        b_hi, b_lo = _split(b)
        return _bdot(a_hi, b_hi, dims) + _bdot(a_lo, b_hi, dims) + _bdot(a_hi, b_lo, dims)
    return _bdot(a.astype(BF16), b.astype(BF16), dims)


NT = ((1,), (1,))
TN = ((0,), (0,))


def _seg_mean(x, m, prec=0):
    if prec == 2:
        return jnp.dot(x, m, precision=lax.Precision.HIGHEST, preferred_element_type=F32)
    hi, lo = _split(x)
    mb = m.astype(BF16)
    return _bdot(hi, mb) + _bdot(lo, mb)


def _rms(x, g):
    return x * lax.rsqrt(jnp.mean(x * x, axis=-1, keepdims=True) + EPS) * g


def _sigmoid(x):
    return 1.0 / (1.0 + jnp.exp(-x))


def _silu(x):
    return x * _sigmoid(x)


def _log_sigmoid(x):
    return jnp.minimum(x, 0.0) - jnp.log(1.0 + jnp.exp(-jnp.abs(x)))


def _softplus(x):
    return jnp.maximum(x, 0.0) + jnp.log(1.0 + jnp.exp(-jnp.abs(x)))


def _gelu_tanh(x):
    return 0.5 * x * (1.0 + jnp.tanh(math.sqrt(2.0 / math.pi) * (x + 0.044715 * (x * x * x))))


def _lane_mask(width, lo, hi):
    lane = lax.broadcasted_iota(jnp.int32, (1, width), 1)
    return (lane >= lo) & (lane < hi)


def _lam_scalar(lv, lam_init):
    s01 = jnp.sum(lv[0:1] * lv[1:2], axis=-1, keepdims=True)
    s23 = jnp.sum(lv[2:3] * lv[3:4], axis=-1, keepdims=True)
    return jnp.exp(s01) - jnp.exp(s23) + lam_init


def _lam_init(layer):
    return 0.8 - 0.6 * math.exp(-0.3 * layer)


def _in_proj_kernel(x_ref, g_ref, wa_ref, wb_ref, wc_ref, wd_ref, qn_ref, kn_ref, m32_ref, w2_ref, gb_ref,
                    pq_ref, pk_ref, pv_ref, qa_ref, ka_ref, va_ref, sb_ref, sc_ref, sd_ref, *packs, prec):
    xn = _rms(x_ref[...], g_ref[0])
    xs = _operand(xn, prec)
    a = _wdot(xs, wa_ref, (), prec)
    q, k, v = a[:, 0:BW], a[:, BW:2 * BW], a[:, 2 * BW:3 * BW]
    m32 = m32_ref[...]
    qa = q * lax.rsqrt(_seg_mean(q * q, m32, prec) + EPS) * qn_ref[0] * (DQA ** -0.5)
    ka = k * lax.rsqrt(_seg_mean(k * k, m32, prec) + EPS) * kn_ref[0]
    qa_ref[...] = qa
    ka_ref[...] = ka
    va_ref[...] = v
    sb_ref[...] = _wdot(xs, wb_ref, (), prec)
    sc_ref[...] = _wdot(xs, wc_ref, (), prec)
    d = _wdot(xs, wd_ref, (), prec)
    sd_ref[:, 0:768] = d[:, 0:768]
    z = _wdot(d[:, 768:896], w2_ref, (), prec) + gb_ref[0]
    sd_ref[:, 768:896] = _log_sigmoid(z) * (1.0 / GLA_TAU)
    if packs:
        for src, p_ref, dst in zip((qa, ka, v), (pq_ref, pk_ref, pv_ref), packs):
            hi, lo = _split(src)
            dst[...] = (_bdot(hi, p_ref[0]) + _bdot(lo, p_ref[1])).astype(BF16)


def _pack_matrices():
    pq = np.zeros((2, BW, QK_PACK), np.float32)
    pk = np.zeros((2, BW, QK_PACK), np.float32)
    pv = np.zeros((2, BW, V_PACK), np.float32)
    for g in range(8):
        for d in range(DQA):
            r, c = DQA * g + d, 128 * g + d
            pq[0, r, c] = pq[0, r, c + 64] = 1.0
            pq[1, r, c + 32] = 1.0
            pk[0, r, c] = pk[0, r, c + 32] = 1.0
            pk[1, r, c + 64] = 1.0
    for h in range(HEADS):
        for d in range(64):
            pv[0, 64 * h + d, 128 * h + d] = 1.0
            pv[1, 64 * h + d, 128 * h + 64 + d] = 1.0
    return [jnp.asarray(a, BF16) for a in (pq, pk, pv)]


def _in_proj(x, p, tm, prec, packed):
    n = x.shape[0]
    layer = p["idx"]
    outs = [jax.ShapeDtypeStruct((n, BW), F32)] * 3 + [jax.ShapeDtypeStruct((n, W_B), F32),
            jax.ShapeDtypeStruct((n, W_C), F32), jax.ShapeDtypeStruct((n, W_D), F32)]
    if packed:
        outs += [jax.ShapeDtypeStruct((n, QK_PACK), BF16)] * 2 + [jax.ShapeDtypeStruct((n, V_PACK), BF16)]
    row = lambda w: pl.BlockSpec((tm, w), lambda i: (i, 0))
    params = [p["norm1"], p["wa"], p["wb"], p["wc"], p["wd"], p["qn_t"], p["kn_t"]]
    tail = [p["gla_w2p"], p["gla_b"]]
    consts = _pack_matrices()
    return pl.pallas_call(
        functools.partial(_in_proj_kernel, prec=prec),
        grid=(n // tm,),
        in_specs=([row(D_MODEL)] + [_layer_spec(a, layer) for a in params] + [_whole_spec(p["m32"])]
                  + [_layer_spec(a, layer) for a in tail] + [_whole_spec(a) for a in consts]),
        out_specs=[row(o.shape[1]) for o in outs],
        out_shape=outs,
        compiler_params=_cparams(("parallel",)),
        name="in_proj_p%d" % prec,
    )(x, *params, p["m32"], *tail, *consts)


def _attn_body(q_ref, k_ref, v_ref, lam, an_ref, o_ref, i, nk, *, lam_init, tq):
    pos_q = i * tq + lax.broadcasted_iota(jnp.int32, (tq, 1), 0)
    pos_k = lax.broadcasted_iota(jnp.int32, (1, nk), 1)
    mask = (pos_k <= pos_q) & ((pos_k >= PADL) | (pos_q < PADL))
    lane = lax.broadcasted_iota(jnp.int32, (1, 128), 1)
    outs = []
    for h in range(HEADS):
        ws = []
        for m in range(2):
            g = 2 * h + m
            s = _bdot(q_ref[:, 128 * g:128 * (g + 1)], k_ref[0:nk, 128 * g:128 * (g + 1)], NT)
            s = jnp.where(mask, s, NEG)
            p = jnp.exp(s - jnp.max(s, axis=-1, keepdims=True))
            ws.append(p / jnp.sum(p, axis=-1, keepdims=True))
        w_hi, w_lo = _split(ws[0] - lam * ws[1])
        o2 = _bdot(jnp.concatenate([w_hi, w_lo], axis=0), v_ref[0:nk, 128 * h:128 * (h + 1)])
        o = o2[0:tq] + o2[tq:2 * tq]
        o = o + pltpu.roll(o, 64, 1)
        outs.append(o * lax.rsqrt(jnp.mean(o * o, axis=-1, keepdims=True) + EPS))
    for t in range(2):
        pair = jnp.where(lane < 64, outs[2 * t], outs[2 * t + 1])
        o_ref[:, 128 * t:128 * (t + 1)] = (pair * an_ref[0][:, 128 * t:128 * (t + 1)]
                                           * (1.0 - lam_init)).astype(o_ref.dtype)


def _attn_kernel(q_ref, k_ref, v_ref, lam_ref, an_ref, o_ref, *, lam_init, tq, key_ranges):
    i = pl.program_id(1)
    lam = _lam_scalar(lam_ref[0], lam_init)
    need = (i + 1) * tq
    lo = 0
    for nk in key_ranges:
        @pl.when((need > lo) & (need <= nk))
        def _(nk=nk):
            _attn_body(q_ref, k_ref, v_ref, lam, an_ref, o_ref, i, nk, lam_init=lam_init, tq=tq)
        lo = nk


def _attn_prompt(qp, kp, vp, p, b, lp):
    nq = lp // BLK
    layer = p["idx"]
    key_ranges = tuple(sorted({min(lp, ATTN_KEY_STEP * j) for j in range(1, -(-lp // ATTN_KEY_STEP) + 1)}))
    return pl.pallas_call(
        functools.partial(_attn_kernel, lam_init=_lam_init(p["layer"]), tq=BLK, key_ranges=key_ranges),
        grid=(b, nq),
        in_specs=[pl.BlockSpec((BLK, QK_PACK), lambda bi, i: (bi * nq + i, 0)),
                  pl.BlockSpec((lp, QK_PACK), lambda bi, i: (bi, 0)),
                  pl.BlockSpec((lp, V_PACK), lambda bi, i: (bi, 0)),
                  _layer_spec(p["lam"], layer), _layer_spec(p["an_t"], layer)],
        out_specs=pl.BlockSpec((BLK, BW), lambda bi, i: (bi * nq + i, 0)),
        out_shape=jax.ShapeDtypeStruct((b * lp, BW), F32),
        compiler_params=_cparams(("parallel", "parallel")),
        name="attn_prompt",
    )(qp, kp, vp, p["lam"], p["an_t"])


def _swap_halves(x):
    w = x.shape[-1]
    lane = lax.broadcasted_iota(jnp.int32, (1, w), 1)
    return jnp.where((lane % 64) < 32, pltpu.roll(x, w - 32, 1), pltpu.roll(x, 32, 1))


def _ret_kernel(sb_ref, cos_ref, sin_ref, dmat_ref, dq_ref, dk_ref, dstate_ref, rn_ref, m64_ref,
                o_ref, s_out_ref, s_ref, *, prec):
    c = pl.program_id(1)

    @pl.when(c == 0)
    def _():
        s_ref[...] = jnp.zeros_like(s_ref)

    cos, sin = cos_ref[...], sin_ref[...]
    q, k, v = sb_ref[:, 0:BW], sb_ref[:, BW:2 * BW], sb_ref[:, 2 * BW:3 * BW]
    qr = q * cos + _swap_halves(q) * sin
    kr = (k * cos + _swap_halves(k) * sin) * 0.125
    o = _adot(qr, s_ref[...], prec) * dq_ref[...]
    for h in range(HEADS):
        hm = _lane_mask(BW, 64 * h, 64 * (h + 1))
        a = _adot(jnp.where(hm, qr, 0.0), kr, prec, NT) * dmat_ref[h]
        o = o + jnp.where(hm, _adot(a, v, prec), 0.0)
    upd = _adot(kr * dk_ref[...], v, prec, TN)
    row_head = lax.broadcasted_iota(jnp.int32, (BW, BW), 0) // 64
    col_head = lax.broadcasted_iota(jnp.int32, (BW, BW), 1) // 64
    s_new = s_ref[...] * dstate_ref[...] + jnp.where(row_head == col_head, upd, 0.0)
    s_ref[...] = s_new
    s_out_ref[0] = s_new
    m64 = m64_ref[...]
    mu = _seg_mean(o, m64)
    var = _seg_mean((o - mu) * (o - mu), m64)
    g = sb_ref[:, 3 * BW:4 * BW]
    o_ref[...] = (o - mu) * lax.rsqrt(var + EPS) * rn_ref[0] * _silu(g)


def _ret_gammas():
    return 1.0 - 2.0 ** (-5.0 - np.arange(HEADS, dtype=np.float64))


def _ret_tables():
    lg = np.log(_ret_gammas())
    t = np.arange(BLK, dtype=np.float64)
    rel = t[:, None] - t[None, :]
    dmat = np.where(rel >= 0, np.exp(np.maximum(rel, 0.0)[None] * lg[:, None, None]), 0.0)
    lane_lg = np.repeat(lg, 64)[None, :]
    dq = np.exp((t[:, None] + 1.0) * lane_lg)
    dk = np.exp((BLK - 1.0 - t)[:, None] * lane_lg)
    dstate = np.exp(BLK * lane_lg)
    return [jnp.asarray(a, F32) for a in (dmat, dq, dk, dstate)]


def _rope_cos_sin(pos):
    inv = ROPE_BASE ** (-jnp.linspace(0.0, 1.0, 32))
    ang = pos.astype(F32)[:, None] * inv[None]
    return jnp.cos(ang), jnp.sin(ang)


def _ret_prompt(sb, p, b, lp, prec):
    nc = lp // BLK
    cos, sin = _rope_cos_sin(jnp.arange(lp) - PADL)
    cos_t = jnp.tile(jnp.concatenate([cos, cos], -1), (1, HEADS))
    sin_t = jnp.tile(jnp.concatenate([-sin, sin], -1), (1, HEADS))
    tables = _ret_tables()
    return pl.pallas_call(
        functools.partial(_ret_kernel, prec=prec),
        grid=(b, nc),
        in_specs=[pl.BlockSpec((BLK, W_B), lambda bi, c: (bi * nc + c, 0)),
                  pl.BlockSpec((BLK, BW), lambda bi, c: (c, 0)),
                  pl.BlockSpec((BLK, BW), lambda bi, c: (c, 0))]
                 + [_whole_spec(a) for a in tables] + [_layer_spec(p["rn_t"], p["idx"]), _whole_spec(p["m64"])],
        out_specs=[pl.BlockSpec((BLK, BW), lambda bi, c: (bi * nc + c, 0)),
                   pl.BlockSpec((1, BW, BW), lambda bi, c: (bi, 0, 0))],
        out_shape=[jax.ShapeDtypeStruct((b * lp, BW), F32), jax.ShapeDtypeStruct((b, BW, BW), F32)],
        scratch_shapes=[pltpu.VMEM((BW, BW), F32)],
        compiler_params=_cparams(("parallel", "arbitrary")),
        name="ret_prompt_p%d" % prec,
    )(sb, cos_t, sin_t, *tables, p["rn_t"], p["m64"])


def _lru_gates(y, wa_ref, wx_ref, ba, bx, sp, prec):
    ys = _operand(y, prec)
    r = _sigmoid(_wdot(ys, wa_ref, (), prec) + ba)
    ig = _sigmoid(_wdot(ys, wx_ref, (), prec) + bx)
    log_a = -LRU_C * r * sp
    a = jnp.exp(log_a)
    u = jnp.sqrt(1.0 - jnp.exp(2.0 * log_a)) * (ig * y)
    return a, u


def _lru_kernel(sc_ref, cw_ref, cb_ref, wa_ref, wx_ref, ba_ref, bx_ref, lam_ref,
                o_ref, hl_ref, cbuf, a_s, u_s, h_s, hcar, *, nb, prec):
    t = pl.program_id(0)

    @pl.when(t == 0)
    def _():
        cbuf[...] = jnp.zeros_like(cbuf)
        hcar[...] = jnp.zeros_like(hcar)

    sp = _softplus(-lam_ref[0])
    row = t * BLK + lax.broadcasted_iota(jnp.int32, (BLK, 1), 0)
    for b in range(nb):
        x = sc_ref[b, :, 0:BW]
        cbuf[b, 8:8 + BLK, :] = x
        y = cb_ref[0] + cbuf[b, 5:5 + BLK, :] * cw_ref[0, 0:1, :]
        for i in range(1, CONV_W):
            y = y + cbuf[b, 5 + i:5 + i + BLK, :] * cw_ref[0, i:i + 1, :]
        cbuf[b, 0:8, :] = x[BLK - 8:BLK, :]
        a, u = _lru_gates(y, wa_ref, wx_ref, ba_ref[0], bx_ref[0], sp, prec)
        a_s[b] = a
        u_s[b] = jnp.where(row >= PADL, u, 0.0)

    def step(i, hs):
        out = []
        for b in range(nb):
            h = a_s[b, pl.ds(i, 1), :] * hs[b] + u_s[b, pl.ds(i, 1), :]
            h_s[b, pl.ds(i, 1), :] = h
            out.append(h)
        return tuple(out)

    hs = lax.fori_loop(0, BLK, step, tuple(hcar[b:b + 1, :] for b in range(nb)))
    for b in range(nb):
        hcar[b:b + 1, :] = hs[b]
        o_ref[b] = h_s[b] * _gelu_tanh(sc_ref[b, :, BW:2 * BW])
    hl_ref[...] = hcar[...]


def _lru_prompt(sc, p, b, lp, prec):
    nt = lp // BLK
    sc3 = sc.reshape(b, lp, W_C)
    params = [p["conv_w"], p["conv_b"], p["lru_wa_bd"], p["lru_wx_bd"], p["lru_ba"], p["lru_bx"], p["lru_lam"]]
    o, hl = pl.pallas_call(
        functools.partial(_lru_kernel, nb=b, prec=prec),
        grid=(nt,),
        in_specs=[pl.BlockSpec((b, BLK, W_C), lambda t: (0, t, 0))] + [_layer_spec(a, p["idx"]) for a in params],
        out_specs=[pl.BlockSpec((b, BLK, BW), lambda t: (0, t, 0)), pl.BlockSpec((b, BW), lambda t: (0, 0))],
        out_shape=[jax.ShapeDtypeStruct((b, lp, BW), F32), jax.ShapeDtypeStruct((b, BW), F32)],
        scratch_shapes=[pltpu.VMEM((b, BLK + 8, BW), F32), pltpu.VMEM((b, BLK, BW), F32),
                        pltpu.VMEM((b, BLK, BW), F32), pltpu.VMEM((b, BLK, BW), F32), pltpu.VMEM((b, BW), F32)],
        compiler_params=_cparams(("arbitrary",)),
        name="lru_prompt_p%d" % prec,
    )(sc3, *params)
    return o.reshape(b * lp, BW), hl


def _gla_kernel(sd_ref, tri_ref, ind_ref, gn_ref, m64_ref, o_ref, st_out_ref, st_ref, o_s, *, prec):
    c = pl.program_id(1)

    @pl.when(c == 0)
    def _():
        st_ref[...] = jnp.zeros_like(st_ref)

    q = sd_ref[:, 0:128] * (DQA ** -0.5)
    k = sd_ref[:, 128:256]
    v = sd_ref[:, 256:512]
    la = sd_ref[:, 768:896]
    tri = tri_ref[...]
    bc = sum(_bdot(tri, part) for part in _split3(la))
    ind = ind_ref[...]
    t_idx = lax.broadcasted_iota(jnp.int32, (GLA_CHUNK, 1), 0)
    row_head = lax.broadcasted_iota(jnp.int32, (BW, 128), 0) // 64
    col_head = lax.broadcasted_iota(jnp.int32, (BW, 128), 1) // 32
    bd = row_head == col_head
    for s in range(BLK // GLA_CHUNK):
        r0 = s * GLA_CHUNK
        bcs = bc[r0:r0 + GLA_CHUNK]
        if s > 0:
            bcs = bcs - bc[r0 - 1:r0]
        qs, ks, vs = q[r0:r0 + GLA_CHUNK], k[r0:r0 + GLA_CHUNK], v[r0:r0 + GLA_CHUNK]
        zs = []
        for j in range(GLA_CHUNK):
            dec = jnp.where(t_idx >= j, jnp.exp(jnp.minimum(bcs - bcs[j:j + 1], 0.0)), 0.0)
            zs.append(qs * dec * ks[j:j + 1])
        z = jnp.concatenate(zs, axis=0)
        if prec == 1:
            z_hi, z_lo = _split(z)
            aa = _bdot(z_hi, ind) + _bdot(z_lo, ind)
        else:
            aa = _bdot(z.astype(BF16), ind)
        st = st_ref[...]
        o = _adot(qs * jnp.exp(bcs), st, prec, NT)
        for j in range(GLA_CHUNK):
            o = o + aa[j * GLA_CHUNK:(j + 1) * GLA_CHUNK] * vs[j:j + 1]
        o_s[r0:r0 + GLA_CHUNK, :] = o
        last = bcs[GLA_CHUNK - 1:GLA_CHUNK]
        upd = _adot(vs, ks * jnp.exp(last - bcs), prec, TN)
        st_ref[...] = st * jnp.exp(last) + jnp.where(bd, upd, 0.0)
    st_out_ref[0] = st_ref[...]
    o = o_s[...]
    ms = _seg_mean(o * o, m64_ref[...])
    o_ref[...] = o * lax.rsqrt(ms + EPS) * gn_ref[0] * _silu(sd_ref[:, 512:768])


def _gla_prompt(sd, p, b, lp, prec):
    nc = lp // BLK
    t = np.arange(BLK)
    tri = jnp.asarray(t[:, None] >= t[None, :], BF16)
    ind = jnp.asarray((np.arange(128)[:, None] // 32) == (np.arange(BW)[None, :] // 64), BF16)
    return pl.pallas_call(
        functools.partial(_gla_kernel, prec=prec),
        grid=(b, nc),
        in_specs=[pl.BlockSpec((BLK, W_D), lambda bi, c: (bi * nc + c, 0)),
                  _whole_spec(tri), _whole_spec(ind), _layer_spec(p["gn_t"], p["idx"]), _whole_spec(p["m64"])],
        out_specs=[pl.BlockSpec((BLK, BW), lambda bi, c: (bi * nc + c, 0)),
                   pl.BlockSpec((1, BW, 128), lambda bi, c: (bi, 0, 0))],
        out_shape=[jax.ShapeDtypeStruct((b * lp, BW), F32), jax.ShapeDtypeStruct((b, BW, 128), F32)],
        scratch_shapes=[pltpu.VMEM((BW, 128), F32), pltpu.VMEM((BLK, BW), F32)],
        compiler_params=_cparams(("parallel", "arbitrary")),
        name="gla_prompt_p%d" % prec,
    )(sd, tri, ind, p["gn_t"], p["m64"])


def _merge_kernel(x_ref, g_ref, oa_ref, ob_ref, oc_ref, od_ref, wm_ref, bm_ref, wbr_ref, wo_ref, y_ref, *, prec):
    x = x_ref[...]
    xn = _rms(x, g_ref[0])
    xs = _operand(xn, prec)
    acc = None
    for n, br in enumerate((oa_ref, ob_ref, oc_ref, od_ref)):
        cols = slice(n * D_MODEL, (n + 1) * D_MODEL)
        gate = _sigmoid(_wdot(xs, wm_ref, (slice(None), cols), prec) + bm_ref[0, :, cols])
        term = gate * _wdot(br[...], wbr_ref, (n,), prec)
        acc = term if acc is None else acc + term
    y_ref[...] = x + _wdot(acc, wo_ref, (), prec)


def _merge(x, oa, ob, oc, od, p, tm, prec):
    n = x.shape[0]
    layer = p["idx"]
    row = lambda w: pl.BlockSpec((tm, w), lambda i: (i, 0))
    params = [p["w_merge"], p["b_merge"], p["w_branch"], p["w_out"]]
    return pl.pallas_call(
        functools.partial(_merge_kernel, prec=prec),
        grid=(n // tm,),
        in_specs=[row(D_MODEL), _layer_spec(p["norm1"], layer), row(BW), row(BW), row(BW), row(BW)]
                 + [_layer_spec(a, layer) for a in params],
        out_specs=row(D_MODEL),
        out_shape=jax.ShapeDtypeStruct((n, D_MODEL), F32),
        compiler_params=_cparams(("parallel",)),
        name="merge_p%d" % prec,
    )(x, p["norm1"], oa, ob, oc, od, *params)


def _first_argmax(vals, valid, lane):
    v = jnp.where(valid, vals, NEG)
    top = jnp.max(v, axis=-1, keepdims=True)
    idx = jnp.min(jnp.where(valid & (v == top), lane, 1 << 20), axis=-1, keepdims=True)
    return top, idx


def _router_gate(lg, le):
    lane = lax.broadcasted_iota(jnp.int32, lg.shape, 1)
    gvalid = lane < N_GROUPS
    lgm = jnp.where(gvalid, lg, NEG)
    pe = jnp.exp(lgm - jnp.max(lgm, axis=-1, keepdims=True))
    pg = pe / jnp.sum(pe, axis=-1, keepdims=True)
    pg_top, g_idx = _first_argmax(pg, gvalid, lane)
    sel = (lane // EXP_PER_GROUP == g_idx) & (lane < N_EXPERTS)
    v1, i1 = _first_argmax(le, sel, lane)
    v2, i2 = _first_argmax(le, sel & (lane != i1), lane)
    e2 = jnp.exp(v2 - v1)
    den = 1.0 + e2
    return jnp.where(lane == i1, pg_top / den, jnp.where(lane == i2, pg_top * e2 / den, 0.0))


def _moe_kernel(h_ref, g_ref, rw_ref, rb_ref, w1_ref, w3_ref, w2_ref, y_ref, t_s, gate_s, *, prec):
    e = pl.program_id(1)

    @pl.when(e == 0)
    def _():
        x = h_ref[...]
        t = _rms(x, g_ref[0])
        t_s[...] = t.astype(t_s.dtype)
        logits = _wdot(t, rw_ref, (), max(prec, 1)) + rb_ref[0]
        gate_s[...] = _router_gate(logits[:, 0:128], logits[:, 128:256])
        y_ref[...] = x

    t = t_s[...]
    lane = lax.broadcasted_iota(jnp.int32, gate_s.shape, 1)
    g = jnp.sum(jnp.where(lane == e, gate_s[...], 0.0), axis=-1, keepdims=True)
    hh = _silu(_wdot(t, w1_ref, (0,), prec)) * _wdot(t, w3_ref, (0,), prec) * g
    y_ref[...] += _wdot(hh, w2_ref, (0,), prec)


def _moe(h, p, tm, prec):
    n = h.shape[0]
    layer = p["idx"]
    wspec = lambda a: pl.BlockSpec((1, a.shape[1], 1) + a.shape[3:], lambda i, e: (layer, 0, e, 0, 0))
    return pl.pallas_call(
        functools.partial(_moe_kernel, prec=prec),
        grid=(n // tm, N_EXPERTS),
        in_specs=[pl.BlockSpec((tm, D_MODEL), lambda i, e: (i, 0)), _layer_spec(p["norm2"], layer),
                  _layer_spec(p["rw"], layer), _layer_spec(p["rb"], layer),
                  wspec(p["ex_w1"]), wspec(p["ex_w3"]), wspec(p["ex_w2"])],
        out_specs=pl.BlockSpec((tm, D_MODEL), lambda i, e: (i, 0)),
        out_shape=jax.ShapeDtypeStruct((n, D_MODEL), F32),
        scratch_shapes=[pltpu.VMEM((tm, D_MODEL), F32 if prec == 2 else BF16), pltpu.VMEM((tm, 128), F32)],
        compiler_params=_cparams(("parallel", "arbitrary")),
        name="moe_p%d" % prec,
    )(h, p["norm2"], p["rw"], p["rb"], p["ex_w1"], p["ex_w3"], p["ex_w2"])


def _dec_attn_kernel(pt_ref, q_ref, kn_ref, vn_ref, lam_ref, *rest, pp, lam_init):
    del pt_ref
    kp, vp = rest[:pp], rest[pp:2 * pp]
    o_ref, s_s, m_s, l_s, acc_s = rest[2 * pp:]
    step = pl.program_id(1)

    @pl.when(step == 0)
    def _():
        m_s[...] = jnp.full_like(m_s, NEG)
        l_s[...] = jnp.zeros_like(l_s)
        acc_s[...] = jnp.zeros_like(acc_s)

    for j in range(pp):
        for h in range(HEADS):
            t = kp[j][0, 0, h] * q_ref[0, h]
            for m in range(2):
                s_s[2 * h + m:2 * h + m + 1, j * PAGE:(j + 1) * PAGE] = jnp.sum(
                    t[DQA * m:DQA * (m + 1)], axis=0, keepdims=True)
    s = s_s[...]
    m_old = m_s[...]
    m_new = jnp.maximum(m_old, jnp.max(s, axis=-1, keepdims=True))
    alpha = jnp.exp(m_old - m_new)
    pr = jnp.exp(s - m_new)
    l_s[...] = alpha * l_s[...] + jnp.sum(pr, axis=-1, keepdims=True)
    m_s[...] = m_new
    s_s[...] = pr
    for h in range(HEADS):
        a0 = acc_s[2 * h] * alpha[2 * h:2 * h + 1]
        a1 = acc_s[2 * h + 1] * alpha[2 * h + 1:2 * h + 2]
        for j in range(pp):
            vt = vp[j][0, 0, h]
            a0 = a0 + vt * s_s[2 * h:2 * h + 1, j * PAGE:(j + 1) * PAGE]
            a1 = a1 + vt * s_s[2 * h + 1:2 * h + 2, j * PAGE:(j + 1) * PAGE]
        acc_s[2 * h] = a0
        acc_s[2 * h + 1] = a1

    @pl.when(step == pl.num_programs(1) - 1)
    def _():
        lam = _lam_scalar(lam_ref[0], lam_init)
        for h in range(HEADS):
            t = q_ref[0, h] * kn_ref[0, h]
            maps = []
            for m in range(2):
                g = 2 * h + m
                sc = jnp.sum(t[DQA * m:DQA * (m + 1)], axis=0, keepdims=True)
                m_o = m_s[g:g + 1]
                m_n = jnp.maximum(m_o, sc)
                al = jnp.exp(m_o - m_n)
                pn = jnp.exp(sc - m_n)
                num = al * jnp.sum(acc_s[g], axis=-1, keepdims=True) + pn * vn_ref[0, h]
                maps.append(num / (al * l_s[g:g + 1] + pn))
            o_ref[0, h] = maps[0] - lam * maps[1]


def _dec_attn(qa, ka, va, cache_k, cache_v, page_table, p):
    db = qa.shape[0]
    layer = p["layer"]
    n_pages = page_table.shape[1]
    pp = math.gcd(PAGES_PER_STEP, n_pages)
    ck = cache_k.transpose(0, 1, 3, 4, 2)
    cv = cache_v.transpose(0, 1, 3, 4, 2)
    page = lambda j: pl.BlockSpec((1, 1, HEADS, 64, PAGE), lambda b, s, pt: (layer, pt[b, s * pp + j], 0, 0, 0))
    col = pl.BlockSpec((1, HEADS, 64, 1), lambda b, s, pt: (b, 0, 0, 0))
    lam = p["lam"]
    grid_spec = pltpu.PrefetchScalarGridSpec(
        num_scalar_prefetch=1,
        grid=(db, n_pages // pp),
        in_specs=[col, col, col, pl.BlockSpec((1,) + lam.shape[1:], lambda b, s, pt: (p["idx"], 0, 0))]
                 + [page(j) for j in range(pp)] + [page(j) for j in range(pp)],
        out_specs=col,
        scratch_shapes=[pltpu.VMEM((8, pp * PAGE), F32), pltpu.VMEM((8, 1), F32), pltpu.VMEM((8, 1), F32),
                        pltpu.VMEM((8, 64, PAGE), F32)],
    )
    cols = lambda x: x.reshape(db, HEADS, 64, 1)
    out = pl.pallas_call(
        functools.partial(_dec_attn_kernel, pp=pp, lam_init=_lam_init(layer)),
        grid_spec=grid_spec,
        out_shape=jax.ShapeDtypeStruct((db, HEADS, 64, 1), F32),
        compiler_params=_cparams(("parallel", "arbitrary")),
        name="attn_sample",
    )(page_table, cols(qa), cols(ka), cols(va), lam, *([ck] * pp), *([cv] * pp))
    return out.reshape(db * HEADS, 64).T


def _step_kernel(oa_ref, an_ref,
                 bq1_ref, bq2_ref, bk1_ref, bk2_ref, bv_ref, bg_ref, s0_ref, cos_ref, sin_ref, gam_ref, rn_ref,
                 dq_ref, dk_ref, dv_ref, dr_ref, la_ref, g0_ref, gn_ref,
                 cx_ref, cg_ref, buf_ref, h0_ref, cw_ref, cb_ref, wa_ref, wx_ref, ba_ref, bx_ref, lam_ref,
                 oa_o, ob_o, od_o, oc_o, s_o, g_o, h_o, *, lam_init):
    x = oa_ref[...]
    oa_o[...] = x * lax.rsqrt(jnp.mean(x * x, axis=0, keepdims=True) + EPS) * an_ref[0] * (1.0 - lam_init)

    cos, sin, gam = cos_ref[...], sin_ref[...], gam_ref[...]
    rot = lambda x1, x2: jnp.concatenate([x1 * cos - x2 * sin, x1 * sin + x2 * cos], axis=0)
    qb = rot(bq1_ref[...], bq2_ref[...])
    kb = rot(bk1_ref[...], bk2_ref[...]) * 0.125
    v = bv_ref[...]
    o = jnp.sum(qb * kb, axis=0, keepdims=True) * v
    st = jnp.zeros_like(v)
    for k in range(64):
        s0k = s0_ref[k]
        st = st + s0k * qb[k:k + 1]
        s_o[k] = gam * s0k + kb[k:k + 1] * v
    o = o + gam * st
    mu = jnp.mean(o, axis=0, keepdims=True)
    var = jnp.mean((o - mu) * (o - mu), axis=0, keepdims=True)
    ob_o[...] = (o - mu) * lax.rsqrt(var + EPS) * rn_ref[0] * _silu(bg_ref[...])

    qd = dq_ref[...] * (DQA ** -0.5)
    kd = dk_ref[...]
    vd = dv_ref[...]
    dec = jnp.exp(la_ref[...])
    o = jnp.sum(qd * kd, axis=0, keepdims=True) * vd
    qe = qd * dec
    for k in range(32):
        g0k = g0_ref[k]
        o = o + g0k * qe[k:k + 1]
        g_o[k] = dec[k:k + 1] * g0k + kd[k:k + 1] * vd
    od_o[...] = o * lax.rsqrt(jnp.mean(o * o, axis=0, keepdims=True) + EPS) * gn_ref[0] * _silu(dr_ref[...])

    cx = cx_ref[...]
    y = cb_ref[0] + cx * cw_ref[0, CONV_W - 1:CONV_W, :]
    for i in range(CONV_W - 1):
        y = y + buf_ref[i] * cw_ref[0, i:i + 1, :]
    a, u = _lru_gates(y, wa_ref, wx_ref, ba_ref[0], bx_ref[0], _softplus(-lam_ref[0]), 2)
    h = a * h0_ref[...] + u
    h_o[...] = h
    oc_o[...] = h * _gelu_tanh(cg_ref[...])


def _to_fm(x, db, width):
    return x.reshape(db, HEADS, width).transpose(2, 0, 1).reshape(width, db * HEADS)


def _from_fm(x, db, width):
    return x.reshape(width, db, HEADS).transpose(1, 2, 0).reshape(db, HEADS * width)


def _halves_fm(x, db):
    x4 = x.reshape(db, HEADS, 2, 32).transpose(2, 3, 0, 1).reshape(2, 32, db * HEADS)
    return x4[0], x4[1]


def _sample_step(oa_fm, sb, sc, sd, ret0, gla0, conv_buf, h0, p, pos):
    db = sb.shape[0]
    n = db * HEADS
    layer = p["idx"]
    cos, sin = _rope_cos_sin(jnp.asarray([pos]))
    gam = jnp.tile(jnp.asarray(_ret_gammas(), F32), db)[None, :]
    bq1, bq2 = _halves_fm(sb[:, 0:BW], db)
    bk1, bk2 = _halves_fm(sb[:, BW:2 * BW], db)
    args = [
        oa_fm, p["an_c"],
        bq1, bq2, bk1, bk2, _to_fm(sb[:, 2 * BW:3 * BW], db, 64), _to_fm(sb[:, 3 * BW:4 * BW], db, 64),
        ret0.transpose(2, 3, 0, 1).reshape(64, 64, n), cos.reshape(32, 1), sin.reshape(32, 1), gam, p["rn_c"],
        _to_fm(sd[:, 0:128], db, 32), _to_fm(sd[:, 128:256], db, 32), _to_fm(sd[:, 256:512], db, 64),
        _to_fm(sd[:, 512:768], db, 64), _to_fm(sd[:, 768:896], db, 32),
        gla0.transpose(2, 3, 0, 1).reshape(32, 64, n), p["gn_c"],
        sc[:, 0:BW], sc[:, BW:2 * BW], conv_buf.transpose(1, 0, 2), h0,
        p["conv_w"], p["conv_b"], p["lru_wa_bd"], p["lru_wx_bd"], p["lru_ba"], p["lru_bx"], p["lru_lam"],
    ]
    layered = {1, 12, 19, 24, 25, 26, 27, 28, 29, 30}
    in_specs = [_layer_spec(a, layer) if i in layered else _whole_spec(a) for i, a in enumerate(args)]
    outs = [jax.ShapeDtypeStruct((64, n), F32)] * 3 + [jax.ShapeDtypeStruct((db, BW), F32),
            jax.ShapeDtypeStruct((64, 64, n), F32), jax.ShapeDtypeStruct((32, 64, n), F32),
            jax.ShapeDtypeStruct((db, BW), F32)]
    oa, ob, od, oc, s_new, g_new, h_new = pl.pallas_call(
        functools.partial(_step_kernel, lam_init=_lam_init(p["layer"])),
        grid=(1,),
        in_specs=in_specs,
        out_specs=[pl.BlockSpec(o.shape, lambda i, nd=len(o.shape): (0,) * nd) for o in outs],
        out_shape=outs,
        compiler_params=_cparams(("arbitrary",)),
        name="step_sample",
    )(*args)
    ret_new = s_new.reshape(64, 64, db, HEADS).transpose(2, 3, 0, 1)
    gla_new = g_new.reshape(32, 64, db, HEADS).transpose(2, 3, 0, 1)
    return _from_fm(oa, db, 64), _from_fm(ob, db, 64), oc, _from_fm(od, db, 64), ret_new, h_new, gla_new


def _block_diag(w):
    d, h, n, _ = w.shape
    eye = jnp.eye(h, dtype=w.dtype)
    return jnp.einsum("dhij,hg->dhigj", w, eye).reshape(d, h * n, h * n)


def _seg_matrix(seg):
    idx = np.arange(BW)
    return jnp.asarray((idx[:, None] // seg == idx[None, :] // seg) / float(seg), F32)


def _parts(w, prec):
    if prec == 2:
        return w[:, None]
    if prec == 0:
        return w.astype(BF16)[:, None]
    return jnp.stack(_split(w), axis=1)


def _prepare(raw, layers, prec, moe_prec, layer):
    sl = lambda a: a[layers]
    (norm1, w_in, qn, kn, lam, an, rn, conv_w, conv_b, lru_wa, lru_ba, lru_wx, lru_bx, lru_lam, gla_w2, gla_b,
     gn, w_branch, w_merge, b_merge, w_out, norm2, rg_w, rg_b, re_w, re_b, ex_w1, ex_w3, ex_w2) = raw
    nl = sl(norm1).shape[0]
    r3 = lambda a: sl(a).reshape(nl, 1, -1)
    tile3 = lambda a, k: jnp.tile(sl(a), (1, k)).reshape(nl, 1, -1)
    pad_cols = lambda a, w: jnp.pad(a, ((0, 0), (0, 0), (0, w - a.shape[-1])))
    w_in = sl(w_in)
    c = np.cumsum([0, 768, 1024, 512, 784])
    rw = jnp.concatenate([pad_cols(sl(rg_w), 128), pad_cols(sl(re_w), 128)], axis=-1)
    return dict(
        idx=layer if nl > 1 else 0, layer=layer,
        norm1=r3(norm1), qn_t=tile3(qn, 8), kn_t=tile3(kn, 8), lam=sl(lam), an_t=tile3(an, 4), rn_t=tile3(rn, 4),
        gn_t=tile3(gn, 4), an_c=sl(an).reshape(nl, 64, 1), rn_c=sl(rn).reshape(nl, 64, 1),
        gn_c=sl(gn).reshape(nl, 64, 1), conv_w=sl(conv_w), conv_b=r3(conv_b), lru_ba=r3(lru_ba),
        lru_bx=r3(lru_bx), lru_lam=r3(lru_lam), gla_b=r3(gla_b), b_merge=r3(b_merge), norm2=r3(norm2),
        rb=jnp.concatenate([pad_cols(r3(rg_b), 128), pad_cols(r3(re_b), 128)], axis=-1),
        m32=_seg_matrix(32), m64=_seg_matrix(64),
        wa=_parts(w_in[:, :, c[0]:c[1]], prec), wb=_parts(w_in[:, :, c[1]:c[2]], prec),
        wc=_parts(w_in[:, :, c[2]:c[3]], prec), wd=_parts(pad_cols(w_in[:, :, c[3]:c[4]], W_D), prec),
        gla_w2p=_parts(jnp.pad(sl(gla_w2), ((0, 0), (0, 128 - GLA_RANK), (0, 0))), prec),
        lru_wa_bd=_parts(_block_diag(sl(lru_wa)), prec), lru_wx_bd=_parts(_block_diag(sl(lru_wx)), prec),
        w_merge=_parts(sl(w_merge), prec), w_branch=_parts(sl(w_branch), prec), w_out=_parts(sl(w_out), prec),
        rw=_parts(rw, max(moe_prec, 1)),
        ex_w1=_parts(sl(ex_w1), moe_prec), ex_w3=_parts(sl(ex_w3), moe_prec), ex_w2=_parts(sl(ex_w2), moe_prec),
    )


def _diag_blocks(s, rows, cols):
    b = s.shape[0]
    s5 = s.reshape(b, HEADS, rows, HEADS, cols)
    return jnp.stack([s5[:, h, :, h, :] for h in range(HEADS)], axis=1)


def _prompt_tile(n, prec):
    for tm in ((256, 128) if prec else (512, 256, 128)):
        if n % tm == 0:
            return tm
    raise ValueError(n)


def kernel(x_prompt, x_sample, cache_k_a, cache_v_a, state_ret, state_lru_h, state_lru_conv, state_gla,
           page_table, meta, norm1, w_in, qn, kn, lam, an, rn, conv_w, conv_b, lru_wa, lru_ba, lru_wx,
           lru_bx, lru_lam, gla_w2, gla_b, gn, w_branch, w_merge, b_merge, w_out, norm2, rg_w, rg_b,
           re_w, re_b, ex_w1, ex_w3, ex_w2):
    bp, seq, _ = x_prompt.shape
    db = x_sample.shape[0]
    depth = w_in.shape[0]
    assert seq % BLK == 0 and x_sample.shape[1] == 1
    lp = BLK + seq
    past = page_table.shape[1] * PAGE
    raw = (norm1, w_in, qn, kn, lam, an, rn, conv_w, conv_b, lru_wa, lru_ba, lru_wx, lru_bx, lru_lam, gla_w2,
           gla_b, gn, w_branch, w_merge, b_merge, w_out, norm2, rg_w, rg_b, re_w, re_b, ex_w1, ex_w3, ex_w2)
    head = jnp.concatenate([jnp.zeros((PADL, D_MODEL), F32), meta.astype(F32)], axis=0)
    xp = jnp.concatenate([jnp.broadcast_to(head[None], (bp, BLK, D_MODEL)), x_prompt], axis=1)
    xp = xp.reshape(bp * lp, D_MODEL)
    xs = x_sample.reshape(db, D_MODEL)
    new_p, new_s = [], []
    for layer in range(depth):
        prec = 1 if layer == 0 else 0
        tm = _prompt_tile(bp * lp, prec)
        p = _prepare(raw, slice(layer, layer + 1), prec, 0, layer)
        _, ka, va, sb, sc, sd, qp, kp, vp = _in_proj(xp, p, tm, prec, True)
        oa = _attn_prompt(qp, kp, vp, p, bp, lp)
        ob, s_ret = _ret_prompt(sb, p, bp, lp, prec)
        oc, h_last = _lru_prompt(sc, p, bp, lp, prec)
        od, s_gla = _gla_prompt(sd, p, bp, lp, prec)
        hres = _merge(xp, oa, ob, oc, od, p, tm, prec)
        xp = _moe(hres, p, _prompt_tile(bp * lp, 0), 0)
        new_p.append((ka.reshape(bp, lp, HEADS, 64)[:, PADL:], va.reshape(bp, lp, HEADS, 64)[:, PADL:],
                      _diag_blocks(s_ret, 64, 64), h_last,
                      sc.reshape(bp, lp, W_C)[:, lp - (CONV_W - 1):, 0:BW],
                      _diag_blocks(s_gla, 64, 32).transpose(0, 1, 3, 2)))
        ps = _prepare(raw, slice(None), 2, 2, layer)
        qa, ka, va, sb, sc, sd = _in_proj(xs, ps, db, 2, False)
        oa_fm = _dec_attn(qa, ka, va, cache_k_a, cache_v_a, page_table, ps)
        oa, ob, oc, od, ret_new, h_new, gla_new = _sample_step(
            oa_fm, sb, sc, sd, state_ret[layer], state_gla[layer], state_lru_conv[layer], state_lru_h[layer],
            ps, past)
        hres = _merge(xs, oa, ob, oc, od, ps, db, 2)
        xs = _moe(hres, ps, db, 2)
        conv_new = jnp.concatenate([state_lru_conv[layer][:, 1:], sc[:, None, 0:BW]], axis=1)
        new_s.append((ka.reshape(db, 1, HEADS, 64), va.reshape(db, 1, HEADS, 64), ret_new, h_new, conv_new,
                      gla_new))
    stk = lambda states, i: jnp.stack([s[i] for s in states])
    y_prompt = xp.reshape(bp, lp, D_MODEL)[:, BLK:]
    return ((y_prompt, xs.reshape(db, 1, D_MODEL))
            + tuple(stk(new_p, i) for i in range(6)) + tuple(stk(new_s, i) for i in range(6)))
```

```python
import functools
import math

import numpy as np
import jax
import jax.numpy as jnp
from jax import lax
from jax.experimental import pallas as pl
from jax.experimental.pallas import tpu as pltpu

F32 = jnp.float32
BF16 = jnp.bfloat16
EPS = 1e-6
D_MODEL = 1024
N_META = 16
PAGE = 128
BLK = 128
PADL = BLK - N_META
HEADS = 4
BW = 256
DQA = 32
ROPE_BASE = 10000.0
CONV_W = 4
LRU_C = 8.0
GLA_RANK = 16
GLA_TAU = 16.0
GLA_CHUNK = 16
N_GROUPS = 4
EXP_PER_GROUP = 4
N_EXPERTS = 16
D_EXPERT = 256
W_A, W_B, W_C, W_D = 768, 1024, 512, 896
QK_PACK = 8 * 128
V_PACK = HEADS * 128
VMEM_LIMIT = 56 * 1024 * 1024
NEG = -1e30
PAGES_PER_STEP = 16
ATTN_KEY_STEP = 512
PAD_LANE = 96
LOG2E = math.log2(math.e)


def _cparams(sem):
    return pltpu.CompilerParams(dimension_semantics=sem, vmem_limit_bytes=VMEM_LIMIT)


def _layer_spec(arr, layer):
    nd = arr.ndim
    return pl.BlockSpec((1,) + arr.shape[1:], lambda *_: (layer,) + (0,) * (nd - 1),
                        pipeline_mode=pl.Buffered(1))


def _whole_spec(arr):
    nd = arr.ndim
    return pl.BlockSpec(arr.shape, lambda *_: (0,) * nd, pipeline_mode=pl.Buffered(1))


def _bf16_prefix(x):
    bits = lax.bitcast_convert_type(x, jnp.uint32) & jnp.uint32(0xFFFF0000)
    return lax.bitcast_convert_type(bits, F32)


def _split(x):
    hi = _bf16_prefix(x)
    return hi.astype(BF16), (x - hi).astype(BF16)


def _split3(x):
    hi = _bf16_prefix(x)
    r = x - hi
    mid = _bf16_prefix(r)
    return hi.astype(BF16), mid.astype(BF16), (r - mid).astype(BF16)


def _bdot(a, b, dims=None):
    if dims is None:
        return jnp.dot(a, b, preferred_element_type=F32)
    return lax.dot_general(a, b, (dims, ((), ())), preferred_element_type=F32)


def _operand(a, prec):
    if prec == 2:
        return a
    return _split(a) if prec == 1 else a.astype(BF16)


def _wdot(a, w_ref, idx, prec):
    get = lambda part: w_ref[(0, part) + idx]
    if prec == 2:
        return jnp.dot(a, get(0), precision=lax.Precision.HIGHEST, preferred_element_type=F32)
    if prec == 1:
        a_hi, a_lo = a if isinstance(a, tuple) else _split(a)
        return _bdot(a_hi, get(0)) + _bdot(a_lo, get(0)) + _bdot(a_hi, get(1))
    return _bdot(a.astype(BF16), get(0))


def _adot(a, b, prec, dims=None):
    if prec == 1:
        a_hi, a_lo = _split(a)
        b_hi, b_lo = _split(b)
        return _bdot(a_hi, b_hi, dims) + _bdot(a_lo, b_hi, dims) + _bdot(a_hi, b_lo, dims)
    return _bdot(a.astype(BF16), b.astype(BF16), dims)


NT = ((1,), (1,))
TN = ((0,), (0,))


def _seg_mean(x, m, prec=0):
    if prec == 2:
        return jnp.dot(x, m, precision=lax.Precision.HIGHEST, preferred_element_type=F32)
    hi, lo = _split(x)
    mb = m.astype(BF16)
    return _bdot(hi, mb) + _bdot(lo, mb)


def _rms(x, g):
    return x * lax.rsqrt(jnp.mean(x * x, axis=-1, keepdims=True) + EPS) * g


def _sigmoid(x):
    return 1.0 / (1.0 + jnp.exp(-x))


def _silu(x):
    return x * _sigmoid(x)


def _log_sigmoid(x):
    return jnp.minimum(x, 0.0) - jnp.log(1.0 + jnp.exp(-jnp.abs(x)))


def _softplus(x):
    return jnp.maximum(x, 0.0) + jnp.log(1.0 + jnp.exp(-jnp.abs(x)))


def _gelu_tanh(x):
    return 0.5 * x * (1.0 + jnp.tanh(math.sqrt(2.0 / math.pi) * (x + 0.044715 * (x * x * x))))


def _lane_mask(width, lo, hi):
    lane = lax.broadcasted_iota(jnp.int32, (1, width), 1)
    return (lane >= lo) & (lane < hi)


def _lam_scalar(lv, lam_init):
    s01 = jnp.sum(lv[0:1] * lv[1:2], axis=-1, keepdims=True)
    s23 = jnp.sum(lv[2:3] * lv[3:4], axis=-1, keepdims=True)
    return jnp.exp(s01) - jnp.exp(s23) + lam_init


def _lam_init(layer):
    return 0.8 - 0.6 * math.exp(-0.3 * layer)


def _in_proj_kernel(x_ref, g_ref, wa_ref, wb_ref, wc_ref, wd_ref, qn_ref, kn_ref, m32_ref, w2_ref, gb_ref,
                    pq_ref, pk_ref, pv_ref, qa_ref, ka_ref, va_ref, sb_ref, sc_ref, sd_ref, *packs, prec, seq_len):
    xn = _rms(x_ref[...], g_ref[0])
    xs = _operand(xn, prec)
    a = _wdot(xs, wa_ref, (), prec)
    q, k, v = a[:, 0:BW], a[:, BW:2 * BW], a[:, 2 * BW:3 * BW]
    m32 = m32_ref[...]
    qa = q * lax.rsqrt(_seg_mean(q * q, m32, prec) + EPS) * qn_ref[0] * (DQA ** -0.5)
    ka = k * lax.rsqrt(_seg_mean(k * k, m32, prec) + EPS) * kn_ref[0]
    qa_ref[...] = qa
    ka_ref[...] = ka
    va_ref[...] = v
    sb_ref[...] = _wdot(xs, wb_ref, (), prec)
    sc_ref[...] = _wdot(xs, wc_ref, (), prec)
    d = _wdot(xs, wd_ref, (), prec)
    sd_ref[:, 0:768] = d[:, 0:768]
    z = _wdot(d[:, 768:896], w2_ref, (), prec) + gb_ref[0]
    sd_ref[:, 768:896] = _log_sigmoid(z) * (1.0 / GLA_TAU)
    if packs:
        def place(src, p_ref):
            hi, lo = _split(src)
            return _bdot(hi, p_ref[0]) + _bdot(lo, p_ref[1])

        tm = x_ref.shape[0]
        row = (pl.program_id(0) * tm + lax.broadcasted_iota(jnp.int32, (tm, 1), 0)).astype(F32)
        in_seq = row - jnp.floor((row + 0.5) * (1.0 / seq_len)) * seq_len
        is_pad = in_seq < PADL
        flag = (lax.broadcasted_iota(jnp.int32, (1, QK_PACK), 1) % 128 == PAD_LANE).astype(F32)
        qp_ref, kp_ref, vp_ref = packs
        qp_ref[...] = (place(qa * LOG2E, pq_ref) + jnp.where(is_pad, 0.0, 1.0) * flag).astype(BF16)
        kp_ref[...] = (place(ka, pk_ref) + jnp.where(is_pad, NEG, 0.0) * flag).astype(BF16)
        vp_ref[...] = place(v, pv_ref).astype(BF16)


def _pack_matrices():
    pq = np.zeros((2, BW, QK_PACK), np.float32)
    pk = np.zeros((2, BW, QK_PACK), np.float32)
    pv = np.zeros((2, BW, V_PACK), np.float32)
    for g in range(8):
        for d in range(DQA):
            r, c = DQA * g + d, 128 * g + d
            pq[0, r, c] = pq[0, r, c + 64] = 1.0
            pq[1, r, c + 32] = 1.0
            pk[0, r, c] = pk[0, r, c + 32] = 1.0
            pk[1, r, c + 64] = 1.0
    for h in range(HEADS):
        for d in range(64):
            pv[0, 64 * h + d, 128 * h + d] = 1.0
            pv[1, 64 * h + d, 128 * h + 64 + d] = 1.0
    return [jnp.asarray(a, BF16) for a in (pq, pk, pv)]


def _in_proj(x, p, tm, prec, seq_len=None):
    n = x.shape[0]
    packed = seq_len is not None
    layer = p["idx"]
    outs = [jax.ShapeDtypeStruct((n, BW), F32)] * 3 + [jax.ShapeDtypeStruct((n, W_B), F32),
            jax.ShapeDtypeStruct((n, W_C), F32), jax.ShapeDtypeStruct((n, W_D), F32)]
    if packed:
        outs += [jax.ShapeDtypeStruct((n, QK_PACK), BF16)] * 2 + [jax.ShapeDtypeStruct((n, V_PACK), BF16)]
    row = lambda w: pl.BlockSpec((tm, w), lambda i: (i, 0))
    params = [p["norm1"], p["wa"], p["wb"], p["wc"], p["wd"], p["qn_t"], p["kn_t"]]
    tail = [p["gla_w2p"], p["gla_b"]]
    consts = _pack_matrices()
    return pl.pallas_call(
        functools.partial(_in_proj_kernel, prec=prec, seq_len=seq_len),
        grid=(n // tm,),
        in_specs=([row(D_MODEL)] + [_layer_spec(a, layer) for a in params] + [_whole_spec(p["m32"])]
                  + [_layer_spec(a, layer) for a in tail] + [_whole_spec(a) for a in consts]),
        out_specs=[row(o.shape[1]) for o in outs],
        out_shape=outs,
        compiler_params=_cparams(("parallel",)),
        name="in_proj_p%d" % prec,
    )(x, *params, p["m32"], *tail, *consts)


def _attn_body(q_ref, k_ref, v_ref, lam, an_ref, o_ref, i, nk, *, lam_init, tq, prec):
    n_free = max(nk - ATTN_KEY_STEP, 0)
    pos_q = i * tq + lax.broadcasted_iota(jnp.int32, (tq, 1), 0)
    causal = n_free + lax.broadcasted_iota(jnp.int32, (1, nk - n_free), 1) <= pos_q
    lane = lax.broadcasted_iota(jnp.int32, (1, 128), 1)
    outs = []
    for h in range(HEADS):
        ps, scale = [], []
        for m in range(2):
            g = 2 * h + m
            s = _bdot(q_ref[:, 128 * g:128 * (g + 1)], k_ref[0:nk, 128 * g:128 * (g + 1)], NT)
            tail = jnp.where(causal, s[:, n_free:], NEG)
            mx = jnp.max(tail, axis=-1, keepdims=True)
            if n_free:
                mx = jnp.maximum(mx, jnp.max(s[:, :n_free], axis=-1, keepdims=True))
                p = jnp.concatenate([jnp.exp2(s[:, :n_free] - mx), jnp.exp2(tail - mx)], axis=1)
            else:
                p = jnp.exp2(tail - mx)
            ps.append(p)
            scale.append(1.0 / jnp.sum(p, axis=-1, keepdims=True))
        w = ps[0] * scale[0] - ps[1] * (lam * scale[1])
        vt = v_ref[0:nk, 128 * h:128 * (h + 1)]
        if prec == 1:
            w_hi, w_lo = _split(w)
            o2 = _bdot(jnp.concatenate([w_hi, w_lo], axis=0), vt)
            o = o2[0:tq] + o2[tq:2 * tq]
        else:
            o = _bdot(w.astype(BF16), vt)
        o = o + pltpu.roll(o, 64, 1)
        outs.append(o * lax.rsqrt(jnp.mean(o * o, axis=-1, keepdims=True) + EPS))
    for t in range(2):
        pair = jnp.where(lane < 64, outs[2 * t], outs[2 * t + 1])
        o_ref[:, 128 * t:128 * (t + 1)] = (pair * an_ref[0][:, 128 * t:128 * (t + 1)]
                                           * (1.0 - lam_init)).astype(o_ref.dtype)


def _attn_kernel(q_ref, k_ref, v_ref, lam_ref, an_ref, o_ref, *, lam_init, tq, key_ranges, prec):
    i = pl.program_id(1)
    lam = _lam_scalar(lam_ref[0], lam_init)
    need = (i + 1) * tq
    lo = 0
    for nk in key_ranges:
        @pl.when((need > lo) & (need <= nk))
        def _(nk=nk):
            _attn_body(q_ref, k_ref, v_ref, lam, an_ref, o_ref, i, nk, lam_init=lam_init, tq=tq, prec=prec)
        lo = nk


def _attn_prompt(qp, kp, vp, p, b, lp, prec):
    nq = lp // BLK
    layer = p["idx"]
    key_ranges = tuple(sorted({min(lp, ATTN_KEY_STEP * j) for j in range(1, -(-lp // ATTN_KEY_STEP) + 1)}))
    return pl.pallas_call(
        functools.partial(_attn_kernel, lam_init=_lam_init(p["layer"]), tq=BLK, key_ranges=key_ranges,
                          prec=prec),
        grid=(b, nq),
        in_specs=[pl.BlockSpec((BLK, QK_PACK), lambda bi, i: (bi * nq + i, 0)),
                  pl.BlockSpec((lp, QK_PACK), lambda bi, i: (bi, 0)),
                  pl.BlockSpec((lp, V_PACK), lambda bi, i: (bi, 0)),
                  _layer_spec(p["lam"], layer), _layer_spec(p["an_t"], layer)],
        out_specs=pl.BlockSpec((BLK, BW), lambda bi, i: (bi * nq + i, 0)),
        out_shape=jax.ShapeDtypeStruct((b * lp, BW), F32),
        compiler_params=_cparams(("parallel", "parallel")),
        name="attn_prompt_p%d" % prec,
    )(qp, kp, vp, p["lam"], p["an_t"])


def _swap_halves(x):
    w = x.shape[-1]
    lane = lax.broadcasted_iota(jnp.int32, (1, w), 1)
    return jnp.where((lane % 64) < 32, pltpu.roll(x, w - 32, 1), pltpu.roll(x, 32, 1))


def _ret_kernel(sb_ref, cos_ref, sin_ref, dmat_ref, dq_ref, dk_ref, dstate_ref, rn_ref, m64_ref,
                o_ref, s_out_ref, s_ref, *, prec):
    c = pl.program_id(1)

    @pl.when(c == 0)
    def _():
        s_ref[...] = jnp.zeros_like(s_ref)

    cos, sin = cos_ref[...], sin_ref[...]
    q, k, v = sb_ref[:, 0:BW], sb_ref[:, BW:2 * BW], sb_ref[:, 2 * BW:3 * BW]
    qr = q * cos + _swap_halves(q) * sin
    kr = (k * cos + _swap_halves(k) * sin) * 0.125
    o = _adot(qr, s_ref[...], prec) * dq_ref[...]
    for h in range(HEADS):
        hm = _lane_mask(BW, 64 * h, 64 * (h + 1))
        a = _adot(jnp.where(hm, qr, 0.0), kr, prec, NT) * dmat_ref[h]
        o = o + jnp.where(hm, _adot(a, v, prec), 0.0)
    upd = _adot(kr * dk_ref[...], v, prec, TN)
    row_head = lax.broadcasted_iota(jnp.int32, (BW, BW), 0) // 64
    col_head = lax.broadcasted_iota(jnp.int32, (BW, BW), 1) // 64
    s_new = s_ref[...] * dstate_ref[...] + jnp.where(row_head == col_head, upd, 0.0)
    s_ref[...] = s_new
    s_out_ref[0] = s_new
    m64 = m64_ref[...]
    mu = _seg_mean(o, m64)
    var = _seg_mean((o - mu) * (o - mu), m64)
    g = sb_ref[:, 3 * BW:4 * BW]
    o_ref[...] = (o - mu) * lax.rsqrt(var + EPS) * rn_ref[0] * _silu(g)


def _ret_gammas():
    return 1.0 - 2.0 ** (-5.0 - np.arange(HEADS, dtype=np.float64))


def _ret_tables():
    lg = np.log(_ret_gammas())
    t = np.arange(BLK, dtype=np.float64)
    rel = t[:, None] - t[None, :]
    dmat = np.where(rel >= 0, np.exp(np.maximum(rel, 0.0)[None] * lg[:, None, None]), 0.0)
    lane_lg = np.repeat(lg, 64)[None, :]
    dq = np.exp((t[:, None] + 1.0) * lane_lg)
    dk = np.exp((BLK - 1.0 - t)[:, None] * lane_lg)
    dstate = np.exp(BLK * lane_lg)
    return [jnp.asarray(a, F32) for a in (dmat, dq, dk, dstate)]


def _rope_cos_sin(pos):
    inv = ROPE_BASE ** (-jnp.linspace(0.0, 1.0, 32))
    ang = pos.astype(F32)[:, None] * inv[None]
    return jnp.cos(ang), jnp.sin(ang)


def _ret_prompt(sb, p, b, lp, prec):
    nc = lp // BLK
    cos, sin = _rope_cos_sin(jnp.arange(lp) - PADL)
    cos_t = jnp.tile(jnp.concatenate([cos, cos], -1), (1, HEADS))
    sin_t = jnp.tile(jnp.concatenate([-sin, sin], -1), (1, HEADS))
    tables = _ret_tables()
    return pl.pallas_call(
        functools.partial(_ret_kernel, prec=prec),
        grid=(b, nc),
        in_specs=[pl.BlockSpec((BLK, W_B), lambda bi, c: (bi * nc + c, 0)),
                  pl.BlockSpec((BLK, BW), lambda bi, c: (c, 0)),
                  pl.BlockSpec((BLK, BW), lambda bi, c: (c, 0))]
                 + [_whole_spec(a) for a in tables] + [_layer_spec(p["rn_t"], p["idx"]), _whole_spec(p["m64"])],
        out_specs=[pl.BlockSpec((BLK, BW), lambda bi, c: (bi * nc + c, 0)),
                   pl.BlockSpec((1, BW, BW), lambda bi, c: (bi, 0, 0))],
        out_shape=[jax.ShapeDtypeStruct((b * lp, BW), F32), jax.ShapeDtypeStruct((b, BW, BW), F32)],
        scratch_shapes=[pltpu.VMEM((BW, BW), F32)],
        compiler_params=_cparams(("parallel", "arbitrary")),
        name="ret_prompt_p%d" % prec,
    )(sb, cos_t, sin_t, *tables, p["rn_t"], p["m64"])


def _lru_gates(y, wa_ref, wx_ref, ba, bx, sp, prec):
    ys = _operand(y, prec)
    r = _sigmoid(_wdot(ys, wa_ref, (), prec) + ba)
    ig = _sigmoid(_wdot(ys, wx_ref, (), prec) + bx)
    log_a = -LRU_C * r * sp
    a = jnp.exp(log_a)
    u = jnp.sqrt(1.0 - jnp.exp(2.0 * log_a)) * (ig * y)
    return a, u


def _lru_kernel(sc_ref, cw_ref, cb_ref, wa_ref, wx_ref, ba_ref, bx_ref, lam_ref,
                o_ref, hl_ref, cbuf, a_s, u_s, h_s, hcar, *, nb, prec):
    t = pl.program_id(0)

    @pl.when(t == 0)
    def _():
        cbuf[...] = jnp.zeros_like(cbuf)
        hcar[...] = jnp.zeros_like(hcar)

    sp = _softplus(-lam_ref[0])
    row = t * BLK + lax.broadcasted_iota(jnp.int32, (BLK, 1), 0)
    for b in range(nb):
        x = sc_ref[b, :, 0:BW]
        cbuf[b, 8:8 + BLK, :] = x
        y = cb_ref[0] + cbuf[b, 5:5 + BLK, :] * cw_ref[0, 0:1, :]
        for i in range(1, CONV_W):
            y = y + cbuf[b, 5 + i:5 + i + BLK, :] * cw_ref[0, i:i + 1, :]
        cbuf[b, 0:8, :] = x[BLK - 8:BLK, :]
        a, u = _lru_gates(y, wa_ref, wx_ref, ba_ref[0], bx_ref[0], sp, prec)
        a_s[b] = a
        u_s[b] = jnp.where(row >= PADL, u, 0.0)

    def step(i, hs):
        out = []
        for b in range(nb):
            h = a_s[b, pl.ds(i, 1), :] * hs[b] + u_s[b, pl.ds(i, 1), :]
            h_s[b, pl.ds(i, 1), :] = h
            out.append(h)
        return tuple(out)

    hs = lax.fori_loop(0, BLK, step, tuple(hcar[b:b + 1, :] for b in range(nb)))
    for b in range(nb):
        hcar[b:b + 1, :] = hs[b]
        o_ref[b] = h_s[b] * _gelu_tanh(sc_ref[b, :, BW:2 * BW])
    hl_ref[...] = hcar[...]


def _lru_prompt(sc, p, b, lp, prec):
    nt = lp // BLK
    sc3 = sc.reshape(b, lp, W_C)
    params = [p["conv_w"], p["conv_b"], p["lru_wa_bd"], p["lru_wx_bd"], p["lru_ba"], p["lru_bx"], p["lru_lam"]]
    o, hl = pl.pallas_call(
        functools.partial(_lru_kernel, nb=b, prec=prec),
        grid=(nt,),
        in_specs=[pl.BlockSpec((b, BLK, W_C), lambda t: (0, t, 0))] + [_layer_spec(a, p["idx"]) for a in params],
        out_specs=[pl.BlockSpec((b, BLK, BW), lambda t: (0, t, 0)), pl.BlockSpec((b, BW), lambda t: (0, 0))],
        out_shape=[jax.ShapeDtypeStruct((b, lp, BW), F32), jax.ShapeDtypeStruct((b, BW), F32)],
        scratch_shapes=[pltpu.VMEM((b, BLK + 8, BW), F32), pltpu.VMEM((b, BLK, BW), F32),
                        pltpu.VMEM((b, BLK, BW), F32), pltpu.VMEM((b, BLK, BW), F32), pltpu.VMEM((b, BW), F32)],
        compiler_params=_cparams(("arbitrary",)),
        name="lru_prompt_p%d" % prec,
    )(sc3, *params)
    return o.reshape(b * lp, BW), hl


def _gla_kernel(sd_ref, tri_ref, ind_ref, gn_ref, m64_ref, o_ref, st_out_ref, st_ref, o_s, *, prec):
    c = pl.program_id(1)

    @pl.when(c == 0)
    def _():
        st_ref[...] = jnp.zeros_like(st_ref)

    q = sd_ref[:, 0:128] * (DQA ** -0.5)
    k = sd_ref[:, 128:256]
    v = sd_ref[:, 256:512]
    la = sd_ref[:, 768:896]
    tri = tri_ref[...]
    bc = sum(_bdot(tri, part) for part in _split3(la))
    ind = ind_ref[...]
    t_idx = lax.broadcasted_iota(jnp.int32, (GLA_CHUNK, 1), 0)
    row_head = lax.broadcasted_iota(jnp.int32, (BW, 128), 0) // 64
    col_head = lax.broadcasted_iota(jnp.int32, (BW, 128), 1) // 32
    bd = row_head == col_head
    for s in range(BLK // GLA_CHUNK):
        r0 = s * GLA_CHUNK
        bcs = bc[r0:r0 + GLA_CHUNK]
        if s > 0:
            bcs = bcs - bc[r0 - 1:r0]
        qs, ks, vs = q[r0:r0 + GLA_CHUNK], k[r0:r0 + GLA_CHUNK], v[r0:r0 + GLA_CHUNK]
        zs = []
        for j in range(GLA_CHUNK):
            dec = jnp.where(t_idx >= j, jnp.exp(jnp.minimum(bcs - bcs[j:j + 1], 0.0)), 0.0)
            zs.append(qs * dec * ks[j:j + 1])
        z = jnp.concatenate(zs, axis=0)
        if prec == 1:
            z_hi, z_lo = _split(z)
            aa = _bdot(z_hi, ind) + _bdot(z_lo, ind)
        else:
            aa = _bdot(z.astype(BF16), ind)
        st = st_ref[...]
        o = _adot(qs * jnp.exp(bcs), st, prec, NT)
        for j in range(GLA_CHUNK):
            o = o + aa[j * GLA_CHUNK:(j + 1) * GLA_CHUNK] * vs[j:j + 1]
        o_s[r0:r0 + GLA_CHUNK, :] = o
        last = bcs[GLA_CHUNK - 1:GLA_CHUNK]
        upd = _adot(vs, ks * jnp.exp(last - bcs), prec, TN)
        st_ref[...] = st * jnp.exp(last) + jnp.where(bd, upd, 0.0)
    st_out_ref[0] = st_ref[...]
    o = o_s[...]
    ms = _seg_mean(o * o, m64_ref[...])
    o_ref[...] = o * lax.rsqrt(ms + EPS) * gn_ref[0] * _silu(sd_ref[:, 512:768])


def _gla_prompt(sd, p, b, lp, prec):
    nc = lp // BLK
    t = np.arange(BLK)
    tri = jnp.asarray(t[:, None] >= t[None, :], BF16)
    ind = jnp.asarray((np.arange(128)[:, None] // 32) == (np.arange(BW)[None, :] // 64), BF16)
    return pl.pallas_call(
        functools.partial(_gla_kernel, prec=prec),
        grid=(b, nc),
        in_specs=[pl.BlockSpec((BLK, W_D), lambda bi, c: (bi * nc + c, 0)),
                  _whole_spec(tri), _whole_spec(ind), _layer_spec(p["gn_t"], p["idx"]), _whole_spec(p["m64"])],
        out_specs=[pl.BlockSpec((BLK, BW), lambda bi, c: (bi * nc + c, 0)),
                   pl.BlockSpec((1, BW, 128), lambda bi, c: (bi, 0, 0))],
        out_shape=[jax.ShapeDtypeStruct((b * lp, BW), F32), jax.ShapeDtypeStruct((b, BW, 128), F32)],
        scratch_shapes=[pltpu.VMEM((BW, 128), F32), pltpu.VMEM((BLK, BW), F32)],
        compiler_params=_cparams(("parallel", "arbitrary")),
        name="gla_prompt_p%d" % prec,
    )(sd, tri, ind, p["gn_t"], p["m64"])


def _merge_kernel(x_ref, g_ref, oa_ref, ob_ref, oc_ref, od_ref, wm_ref, bm_ref, wbr_ref, wo_ref, y_ref, *, prec):
    x = x_ref[...]
    xn = _rms(x, g_ref[0])
    xs = _operand(xn, prec)
    acc = None
    for n, br in enumerate((oa_ref, ob_ref, oc_ref, od_ref)):
        cols = slice(n * D_MODEL, (n + 1) * D_MODEL)
        gate = _sigmoid(_wdot(xs, wm_ref, (slice(None), cols), prec) + bm_ref[0, :, cols])
        term = gate * _wdot(br[...], wbr_ref, (n,), prec)
        acc = term if acc is None else acc + term
    y_ref[...] = x + _wdot(acc, wo_ref, (), prec)


def _merge(x, oa, ob, oc, od, p, tm, prec):
    n = x.shape[0]
    layer = p["idx"]
    row = lambda w: pl.BlockSpec((tm, w), lambda i: (i, 0))
    params = [p["w_merge"], p["b_merge"], p["w_branch"], p["w_out"]]
    return pl.pallas_call(
        functools.partial(_merge_kernel, prec=prec),
        grid=(n // tm,),
        in_specs=[row(D_MODEL), _layer_spec(p["norm1"], layer), row(BW), row(BW), row(BW), row(BW)]
                 + [_layer_spec(a, layer) for a in params],
        out_specs=row(D_MODEL),
        out_shape=jax.ShapeDtypeStruct((n, D_MODEL), F32),
        compiler_params=_cparams(("parallel",)),
        name="merge_p%d" % prec,
    )(x, p["norm1"], oa, ob, oc, od, *params)


def _first_argmax(vals, valid, lane):
    v = jnp.where(valid, vals, NEG)
    top = jnp.max(v, axis=-1, keepdims=True)
    idx = jnp.min(jnp.where(valid & (v == top), lane, 1 << 20), axis=-1, keepdims=True)
    return top, idx


def _router_gate(lg, le):
    lane = lax.broadcasted_iota(jnp.int32, lg.shape, 1)
    gvalid = lane < N_GROUPS
    lgm = jnp.where(gvalid, lg, NEG)
    pe = jnp.exp(lgm - jnp.max(lgm, axis=-1, keepdims=True))
    pg = pe / jnp.sum(pe, axis=-1, keepdims=True)
    pg_top, g_idx = _first_argmax(pg, gvalid, lane)
    sel = (lane // EXP_PER_GROUP == g_idx) & (lane < N_EXPERTS)
    v1, i1 = _first_argmax(le, sel, lane)
    v2, i2 = _first_argmax(le, sel & (lane != i1), lane)
    e2 = jnp.exp(v2 - v1)
    den = 1.0 + e2
    return jnp.where(lane == i1, pg_top / den, jnp.where(lane == i2, pg_top * e2 / den, 0.0))


def _moe_kernel(h_ref, g_ref, rw_ref, rb_ref, sel_ref, w1_ref, w3_ref, w2_ref, y_ref, t_s, gate_s, *, prec):
    grp = pl.program_id(1)

    @pl.when(grp == 0)
    def _():
        x = h_ref[...]
        t = _rms(x, g_ref[0])
        t_s[...] = t.astype(t_s.dtype)
        logits = _wdot(t, rw_ref, (), max(prec, 1)) + rb_ref[0]
        gate_s[...] = _router_gate(logits[:, 0:128], logits[:, 128:256])
        y_ref[...] = x

    t = t_s[...]
    gcols = _seg_mean(gate_s[...], sel_ref[grp], prec)
    acc = None
    for j in range(EXP_PER_GROUP):
        gj = gcols[:, 128 * j:128 * (j + 1)]
        hj = _silu(_wdot(t, w1_ref, (j,), prec)) * _wdot(t, w3_ref, (j,), prec)
        term = _wdot(hj * jnp.concatenate([gj, gj], axis=1), w2_ref, (j,), prec)
        acc = term if acc is None else acc + term
    y_ref[...] += acc


def _moe(h, p, tm, prec):
    n = h.shape[0]
    layer = p["idx"]
    wspec = lambda a: pl.BlockSpec((1, a.shape[1], EXP_PER_GROUP) + a.shape[3:],
                                   lambda i, g: (layer, 0, g, 0, 0))
    sel = np.zeros((N_GROUPS, 128, EXP_PER_GROUP * 128), np.float32)
    for g in range(N_GROUPS):
        for j in range(EXP_PER_GROUP):
            sel[g, EXP_PER_GROUP * g + j, 128 * j:128 * (j + 1)] = 1.0
    sel = jnp.asarray(sel)
    return pl.pallas_call(
        functools.partial(_moe_kernel, prec=prec),
        grid=(n // tm, N_GROUPS),
        in_specs=[pl.BlockSpec((tm, D_MODEL), lambda i, g: (i, 0)), _layer_spec(p["norm2"], layer),
                  _layer_spec(p["rw"], layer), _layer_spec(p["rb"], layer), _whole_spec(sel),
                  wspec(p["ex_w1"]), wspec(p["ex_w3"]), wspec(p["ex_w2"])],
        out_specs=pl.BlockSpec((tm, D_MODEL), lambda i, g: (i, 0)),
        out_shape=jax.ShapeDtypeStruct((n, D_MODEL), F32),
        scratch_shapes=[pltpu.VMEM((tm, D_MODEL), F32 if prec == 2 else BF16), pltpu.VMEM((tm, 128), F32)],
        compiler_params=_cparams(("parallel", "arbitrary")),
        name="moe_p%d" % prec,
    )(h, p["norm2"], p["rw"], p["rb"], sel, p["ex_w1"], p["ex_w3"], p["ex_w2"])


def _dec_attn_kernel(pt_ref, q_ref, kn_ref, vn_ref, lam_ref, *rest, pp, lam_init):
    del pt_ref
    kp, vp = rest[:pp], rest[pp:2 * pp]
    o_ref, s_s, m_s, l_s, acc_s = rest[2 * pp:]
    step = pl.program_id(1)

    @pl.when(step == 0)
    def _():
        m_s[...] = jnp.full_like(m_s, NEG)
        l_s[...] = jnp.zeros_like(l_s)
        acc_s[...] = jnp.zeros_like(acc_s)

    for j in range(pp):
        for h in range(HEADS):
            t = kp[j][0, 0, h] * q_ref[0, h]
            for m in range(2):
                s_s[2 * h + m:2 * h + m + 1, j * PAGE:(j + 1) * PAGE] = jnp.sum(
                    t[DQA * m:DQA * (m + 1)], axis=0, keepdims=True)
    s = s_s[...]
    m_old = m_s[...]
    m_new = jnp.maximum(m_old, jnp.max(s, axis=-1, keepdims=True))
    alpha = jnp.exp(m_old - m_new)
    pr = jnp.exp(s - m_new)
    l_s[...] = alpha * l_s[...] + jnp.sum(pr, axis=-1, keepdims=True)
    m_s[...] = m_new
    s_s[...] = pr
    for h in range(HEADS):
        a0 = acc_s[2 * h] * alpha[2 * h:2 * h + 1]
        a1 = acc_s[2 * h + 1] * alpha[2 * h + 1:2 * h + 2]
        for j in range(pp):
            vt = vp[j][0, 0, h]
            a0 = a0 + vt * s_s[2 * h:2 * h + 1, j * PAGE:(j + 1) * PAGE]
            a1 = a1 + vt * s_s[2 * h + 1:2 * h + 2, j * PAGE:(j + 1) * PAGE]
        acc_s[2 * h] = a0
        acc_s[2 * h + 1] = a1

    @pl.when(step == pl.num_programs(1) - 1)
    def _():
        lam = _lam_scalar(lam_ref[0], lam_init)
        for h in range(HEADS):
            t = q_ref[0, h] * kn_ref[0, h]
            maps = []
            for m in range(2):
                g = 2 * h + m
                sc = jnp.sum(t[DQA * m:DQA * (m + 1)], axis=0, keepdims=True)
                m_o = m_s[g:g + 1]
                m_n = jnp.maximum(m_o, sc)
                al = jnp.exp(m_o - m_n)
                pn = jnp.exp(sc - m_n)
                num = al * jnp.sum(acc_s[g], axis=-1, keepdims=True) + pn * vn_ref[0, h]
                maps.append(num / (al * l_s[g:g + 1] + pn))
            o_ref[0, h] = maps[0] - lam * maps[1]


def _dec_attn(qa, ka, va, cache_k, cache_v, page_table, p):
    db = qa.shape[0]
    layer = p["layer"]
    n_pages = page_table.shape[1]
    pp = math.gcd(PAGES_PER_STEP, n_pages)
    ck = cache_k.transpose(0, 1, 3, 4, 2)
    cv = cache_v.transpose(0, 1, 3, 4, 2)
    page = lambda j: pl.BlockSpec((1, 1, HEADS, 64, PAGE), lambda b, s, pt: (layer, pt[b, s * pp + j], 0, 0, 0))
    col = pl.BlockSpec((1, HEADS, 64, 1), lambda b, s, pt: (b, 0, 0, 0))
    lam = p["lam"]
    grid_spec = pltpu.PrefetchScalarGridSpec(
        num_scalar_prefetch=1,
        grid=(db, n_pages // pp),
        in_specs=[col, col, col, pl.BlockSpec((1,) + lam.shape[1:], lambda b, s, pt: (p["idx"], 0, 0))]
                 + [page(j) for j in range(pp)] + [page(j) for j in range(pp)],
        out_specs=col,
        scratch_shapes=[pltpu.VMEM((8, pp * PAGE), F32), pltpu.VMEM((8, 1), F32), pltpu.VMEM((8, 1), F32),
                        pltpu.VMEM((8, 64, PAGE), F32)],
    )
    cols = lambda x: x.reshape(db, HEADS, 64, 1)
    out = pl.pallas_call(
        functools.partial(_dec_attn_kernel, pp=pp, lam_init=_lam_init(layer)),
        grid_spec=grid_spec,
        out_shape=jax.ShapeDtypeStruct((db, HEADS, 64, 1), F32),
        compiler_params=_cparams(("parallel", "arbitrary")),
        name="attn_sample",
    )(page_table, cols(qa), cols(ka), cols(va), lam, *([ck] * pp), *([cv] * pp))
    return out.reshape(db * HEADS, 64).T


def _step_kernel(oa_ref, an_ref,
                 bq1_ref, bq2_ref, bk1_ref, bk2_ref, bv_ref, bg_ref, s0_ref, cos_ref, sin_ref, gam_ref, rn_ref,
                 dq_ref, dk_ref, dv_ref, dr_ref, la_ref, g0_ref, gn_ref,
                 cx_ref, cg_ref, buf_ref, h0_ref, cw_ref, cb_ref, wa_ref, wx_ref, ba_ref, bx_ref, lam_ref,
                 oa_o, ob_o, od_o, oc_o, s_o, g_o, h_o, *, lam_init):
    x = oa_ref[...]
    oa_o[...] = x * lax.rsqrt(jnp.mean(x * x, axis=0, keepdims=True) + EPS) * an_ref[0] * (1.0 - lam_init)

    cos, sin, gam = cos_ref[...], sin_ref[...], gam_ref[...]
    rot = lambda x1, x2: jnp.concatenate([x1 * cos - x2 * sin, x1 * sin + x2 * cos], axis=0)
    qb = rot(bq1_ref[...], bq2_ref[...])
    kb = rot(bk1_ref[...], bk2_ref[...]) * 0.125
    v = bv_ref[...]
    o = jnp.sum(qb * kb, axis=0, keepdims=True) * v
    st = jnp.zeros_like(v)
    for k in range(64):
        s0k = s0_ref[k]
        st = st + s0k * qb[k:k + 1]
        s_o[k] = gam * s0k + kb[k:k + 1] * v
    o = o + gam * st
    mu = jnp.mean(o, axis=0, keepdims=True)
    var = jnp.mean((o - mu) * (o - mu), axis=0, keepdims=True)
    ob_o[...] = (o - mu) * lax.rsqrt(var + EPS) * rn_ref[0] * _silu(bg_ref[...])

    qd = dq_ref[...] * (DQA ** -0.5)
    kd = dk_ref[...]
    vd = dv_ref[...]
    dec = jnp.exp(la_ref[...])
    o = jnp.sum(qd * kd, axis=0, keepdims=True) * vd
    qe = qd * dec
    for k in range(32):
        g0k = g0_ref[k]
        o = o + g0k * qe[k:k + 1]
        g_o[k] = dec[k:k + 1] * g0k + kd[k:k + 1] * vd
    od_o[...] = o * lax.rsqrt(jnp.mean(o * o, axis=0, keepdims=True) + EPS) * gn_ref[0] * _silu(dr_ref[...])

    cx = cx_ref[...]
    y = cb_ref[0] + cx * cw_ref[0, CONV_W - 1:CONV_W, :]
    for i in range(CONV_W - 1):
        y = y + buf_ref[i] * cw_ref[0, i:i + 1, :]
    a, u = _lru_gates(y, wa_ref, wx_ref, ba_ref[0], bx_ref[0], _softplus(-lam_ref[0]), 2)
    h = a * h0_ref[...] + u
    h_o[...] = h
    oc_o[...] = h * _gelu_tanh(cg_ref[...])


def _to_fm(x, db, width):
    return x.reshape(db, HEADS, width).transpose(2, 0, 1).reshape(width, db * HEADS)


def _from_fm(x, db, width):
    return x.reshape(width, db, HEADS).transpose(1, 2, 0).reshape(db, HEADS * width)


def _halves_fm(x, db):
    x4 = x.reshape(db, HEADS, 2, 32).transpose(2, 3, 0, 1).reshape(2, 32, db * HEADS)
    return x4[0], x4[1]


def _sample_step(oa_fm, sb, sc, sd, ret0, gla0, conv_buf, h0, p, pos):
    db = sb.shape[0]
    n = db * HEADS
    layer = p["idx"]
    cos, sin = _rope_cos_sin(jnp.asarray([pos]))
    gam = jnp.tile(jnp.asarray(_ret_gammas(), F32), db)[None, :]
    bq1, bq2 = _halves_fm(sb[:, 0:BW], db)
    bk1, bk2 = _halves_fm(sb[:, BW:2 * BW], db)
    args = [
        oa_fm, p["an_c"],
        bq1, bq2, bk1, bk2, _to_fm(sb[:, 2 * BW:3 * BW], db, 64), _to_fm(sb[:, 3 * BW:4 * BW], db, 64),
        ret0.transpose(2, 3, 0, 1).reshape(64, 64, n), cos.reshape(32, 1), sin.reshape(32, 1), gam, p["rn_c"],
        _to_fm(sd[:, 0:128], db, 32), _to_fm(sd[:, 128:256], db, 32), _to_fm(sd[:, 256:512], db, 64),
        _to_fm(sd[:, 512:768], db, 64), _to_fm(sd[:, 768:896], db, 32),
        gla0.transpose(2, 3, 0, 1).reshape(32, 64, n), p["gn_c"],
        sc[:, 0:BW], sc[:, BW:2 * BW], conv_buf.transpose(1, 0, 2), h0,
        p["conv_w"], p["conv_b"], p["lru_wa_bd"], p["lru_wx_bd"], p["lru_ba"], p["lru_bx"], p["lru_lam"],
    ]
    layered = {1, 12, 19, 24, 25, 26, 27, 28, 29, 30}
    in_specs = [_layer_spec(a, layer) if i in layered else _whole_spec(a) for i, a in enumerate(args)]
    outs = [jax.ShapeDtypeStruct((64, n), F32)] * 3 + [jax.ShapeDtypeStruct((db, BW), F32),
            jax.ShapeDtypeStruct((64, 64, n), F32), jax.ShapeDtypeStruct((32, 64, n), F32),
            jax.ShapeDtypeStruct((db, BW), F32)]
    oa, ob, od, oc, s_new, g_new, h_new = pl.pallas_call(
        functools.partial(_step_kernel, lam_init=_lam_init(p["layer"])),
        grid=(1,),
        in_specs=in_specs,
        out_specs=[pl.BlockSpec(o.shape, lambda i, nd=len(o.shape): (0,) * nd) for o in outs],
        out_shape=outs,
        compiler_params=_cparams(("arbitrary",)),
        name="step_sample",
    )(*args)
    ret_new = s_new.reshape(64, 64, db, HEADS).transpose(2, 3, 0, 1)
    gla_new = g_new.reshape(32, 64, db, HEADS).transpose(2, 3, 0, 1)
    return _from_fm(oa, db, 64), _from_fm(ob, db, 64), oc, _from_fm(od, db, 64), ret_new, h_new, gla_new


def _block_diag(w):
    d, h, n, _ = w.shape
    eye = jnp.eye(h, dtype=w.dtype)
    return jnp.einsum("dhij,hg->dhigj", w, eye).reshape(d, h * n, h * n)


def _seg_matrix(seg):
    idx = np.arange(BW)
    return jnp.asarray((idx[:, None] // seg == idx[None, :] // seg) / float(seg), F32)


def _parts(w, prec):
    if prec == 2:
        return w[:, None]
    if prec == 0:
        return w.astype(BF16)[:, None]
    return jnp.stack(_split(w), axis=1)


def _prepare(raw, layers, prec, moe_prec, layer):
    sl = lambda a: a[layers]
    (norm1, w_in, qn, kn, lam, an, rn, conv_w, conv_b, lru_wa, lru_ba, lru_wx, lru_bx, lru_lam, gla_w2, gla_b,
     gn, w_branch, w_merge, b_merge, w_out, norm2, rg_w, rg_b, re_w, re_b, ex_w1, ex_w3, ex_w2) = raw
    nl = sl(norm1).shape[0]
    r3 = lambda a: sl(a).reshape(nl, 1, -1)
    tile3 = lambda a, k: jnp.tile(sl(a), (1, k)).reshape(nl, 1, -1)
    pad_cols = lambda a, w: jnp.pad(a, ((0, 0), (0, 0), (0, w - a.shape[-1])))
    w_in = sl(w_in)
    c = np.cumsum([0, 768, 1024, 512, 784])
    rw = jnp.concatenate([pad_cols(sl(rg_w), 128), pad_cols(sl(re_w), 128)], axis=-1)
    return dict(
        idx=layer if nl > 1 else 0, layer=layer,
        norm1=r3(norm1), qn_t=tile3(qn, 8), kn_t=tile3(kn, 8), lam=sl(lam), an_t=tile3(an, 4), rn_t=tile3(rn, 4),
        gn_t=tile3(gn, 4), an_c=sl(an).reshape(nl, 64, 1), rn_c=sl(rn).reshape(nl, 64, 1),
        gn_c=sl(gn).reshape(nl, 64, 1), conv_w=sl(conv_w), conv_b=r3(conv_b), lru_ba=r3(lru_ba),
        lru_bx=r3(lru_bx), lru_lam=r3(lru_lam), gla_b=r3(gla_b), b_merge=r3(b_merge), norm2=r3(norm2),
        rb=jnp.concatenate([pad_cols(r3(rg_b), 128), pad_cols(r3(re_b), 128)], axis=-1),
        m32=_seg_matrix(32), m64=_seg_matrix(64),
        wa=_parts(w_in[:, :, c[0]:c[1]], prec), wb=_parts(w_in[:, :, c[1]:c[2]], prec),
        wc=_parts(w_in[:, :, c[2]:c[3]], prec), wd=_parts(pad_cols(w_in[:, :, c[3]:c[4]], W_D), prec),
        gla_w2p=_parts(jnp.pad(sl(gla_w2), ((0, 0), (0, 128 - GLA_RANK), (0, 0))), prec),
        lru_wa_bd=_parts(_block_diag(sl(lru_wa)), prec), lru_wx_bd=_parts(_block_diag(sl(lru_wx)), prec),
        w_merge=_parts(sl(w_merge), prec), w_branch=_parts(sl(w_branch), prec), w_out=_parts(sl(w_out), prec),
        rw=_parts(rw, max(moe_prec, 1)),
        ex_w1=_parts(sl(ex_w1), moe_prec), ex_w3=_parts(sl(ex_w3), moe_prec), ex_w2=_parts(sl(ex_w2), moe_prec),
    )


def _diag_blocks(s, rows, cols):
    b = s.shape[0]
    s5 = s.reshape(b, HEADS, rows, HEADS, cols)
    return jnp.stack([s5[:, h, :, h, :] for h in range(HEADS)], axis=1)


def _prompt_tile(n, prec):
    for tm in ((256, 128) if prec else (512, 256, 128)):
        if n % tm == 0:
            return tm
    raise ValueError(n)


def kernel(x_prompt, x_sample, cache_k_a, cache_v_a, state_ret, state_lru_h, state_lru_conv, state_gla,
           page_table, meta, norm1, w_in, qn, kn, lam, an, rn, conv_w, conv_b, lru_wa, lru_ba, lru_wx,
           lru_bx, lru_lam, gla_w2, gla_b, gn, w_branch, w_merge, b_merge, w_out, norm2, rg_w, rg_b,
           re_w, re_b, ex_w1, ex_w3, ex_w2):
    bp, seq, _ = x_prompt.shape
    db = x_sample.shape[0]
    depth = w_in.shape[0]
    assert seq % BLK == 0 and x_sample.shape[1] == 1
    lp = BLK + seq
    past = page_table.shape[1] * PAGE
    raw = (norm1, w_in, qn, kn, lam, an, rn, conv_w, conv_b, lru_wa, lru_ba, lru_wx, lru_bx, lru_lam, gla_w2,
           gla_b, gn, w_branch, w_merge, b_merge, w_out, norm2, rg_w, rg_b, re_w, re_b, ex_w1, ex_w3, ex_w2)
    head = jnp.concatenate([jnp.zeros((PADL, D_MODEL), F32), meta.astype(F32)], axis=0)
    xp = jnp.concatenate([jnp.broadcast_to(head[None], (bp, BLK, D_MODEL)), x_prompt], axis=1)
    xp = xp.reshape(bp * lp, D_MODEL)
    xs = x_sample.reshape(db, D_MODEL)
    new_p, new_s = [], []
    for layer in range(depth):
        prec = 1 if layer == 0 else 0
        tm = _prompt_tile(bp * lp, prec)
        p = _prepare(raw, slice(layer, layer + 1), prec, 0, layer)
        _, ka, va, sb, sc, sd, qp, kp, vp = _in_proj(xp, p, tm, prec, lp)
        oa = _attn_prompt(qp, kp, vp, p, bp, lp, prec)
        ob, s_ret = _ret_prompt(sb, p, bp, lp, prec)
        oc, h_last = _lru_prompt(sc, p, bp, lp, prec)
        od, s_gla = _gla_prompt(sd, p, bp, lp, prec)
        hres = _merge(xp, oa, ob, oc, od, p, tm, prec)
        xp = _moe(hres, p, _prompt_tile(bp * lp, 0), 0)
        new_p.append((ka.reshape(bp, lp, HEADS, 64)[:, PADL:], va.reshape(bp, lp, HEADS, 64)[:, PADL:],
                      _diag_blocks(s_ret, 64, 64), h_last,
                      sc.reshape(bp, lp, W_C)[:, lp - (CONV_W - 1):, 0:BW],
                      _diag_blocks(s_gla, 64, 32).transpose(0, 1, 3, 2)))
        ps = _prepare(raw, slice(None), 2, 2, layer)
        qa, ka, va, sb, sc, sd = _in_proj(xs, ps, db, 2)
        oa_fm = _dec_attn(qa, ka, va, cache_k_a, cache_v_a, page_table, ps)
        oa, ob, oc, od, ret_new, h_new, gla_new = _sample_step(
            oa_fm, sb, sc, sd, state_ret[layer], state_gla[layer], state_lru_conv[layer], state_lru_h[layer],
            ps, past)
        hres = _merge(xs, oa, ob, oc, od, ps, db, 2)
        xs = _moe(hres, ps, db, 2)
        conv_new = jnp.concatenate([state_lru_conv[layer][:, 1:], sc[:, None, 0:BW]], axis=1)
        new_s.append((ka.reshape(db, 1, HEADS, 64), va.reshape(db, 1, HEADS, 64), ret_new, h_new, conv_new,
                      gla_new))
    stk = lambda states, i: jnp.stack([s[i] for s in states])
    y_prompt = xp.reshape(bp, lp, D_MODEL)[:, BLK:]
    return ((y_prompt, xs.reshape(db, 1, D_MODEL))
            + tuple(stk(new_p, i) for i in range(6)) + tuple(stk(new_s, i) for i in range(6)))
```

```python
import functools
import math

import numpy as np
import jax
import jax.numpy as jnp
from jax import lax
from jax.experimental import pallas as pl
from jax.experimental.pallas import tpu as pltpu

F32 = jnp.float32
BF16 = jnp.bfloat16
EPS = 1e-6
D_MODEL = 1024
N_META = 16
PAGE = 128
BLK = 128
PADL = BLK - N_META
HEADS = 4
BW = 256
DQA = 32
ROPE_BASE = 10000.0
CONV_W = 4
LRU_C = 8.0
GLA_RANK = 16
GLA_TAU = 16.0
GLA_CHUNK = 16
N_GROUPS = 4
EXP_PER_GROUP = 4
N_EXPERTS = 16
D_EXPERT = 256
W_A, W_B, W_C, W_D = 768, 1024, 512, 896
QK_PACK = 8 * 128
V_PACK = HEADS * 128
VMEM_LIMIT = 56 * 1024 * 1024
NEG = -1e30
PAGES_PER_STEP = 16
ATTN_KEY_STEP = 512
PAD_LANE = 96
LOG2E = math.log2(math.e)


def _cparams(sem):
    return pltpu.CompilerParams(dimension_semantics=sem, vmem_limit_bytes=VMEM_LIMIT)


def _layer_spec(arr, layer):
    nd = arr.ndim
    return pl.BlockSpec((1,) + arr.shape[1:], lambda *_: (layer,) + (0,) * (nd - 1),
                        pipeline_mode=pl.Buffered(1))


def _whole_spec(arr):
    nd = arr.ndim
    return pl.BlockSpec(arr.shape, lambda *_: (0,) * nd, pipeline_mode=pl.Buffered(1))


def _bf16_prefix(x):
    bits = lax.bitcast_convert_type(x, jnp.uint32) & jnp.uint32(0xFFFF0000)
    return lax.bitcast_convert_type(bits, F32)


def _split(x):
    hi = _bf16_prefix(x)
    return hi.astype(BF16), (x - hi).astype(BF16)


def _split3(x):
    hi = _bf16_prefix(x)
    r = x - hi
    mid = _bf16_prefix(r)
    return hi.astype(BF16), mid.astype(BF16), (r - mid).astype(BF16)


def _bdot(a, b, dims=None):
    if dims is None:
        return jnp.dot(a, b, preferred_element_type=F32)
    return lax.dot_general(a, b, (dims, ((), ())), preferred_element_type=F32)


def _operand(a, prec):
    if prec == 2:
        return a
    return _split(a) if prec == 1 else a.astype(BF16)


def _wdot(a, w_ref, idx, prec):
    get = lambda part: w_ref[(0, part) + idx]
    if prec == 2:
        return jnp.dot(a, get(0), precision=lax.Precision.HIGHEST, preferred_element_type=F32)
    if prec == 1:
        a_hi, a_lo = a if isinstance(a, tuple) else _split(a)
        return _bdot(a_hi, get(0)) + _bdot(a_lo, get(0)) + _bdot(a_hi, get(1))
    return _bdot(a.astype(BF16), get(0))


def _adot(a, b, prec, dims=None):
    if prec == 1:
        a_hi, a_lo = _split(a)
        b_hi, b_lo = _split(b)
        return _bdot(a_hi, b_hi, dims) + _bdot(a_lo, b_hi, dims) + _bdot(a_hi, b_lo, dims)
    return _bdot(a.astype(BF16), b.astype(BF16), dims)


NT = ((1,), (1,))
TN = ((0,), (0,))


def _seg_mean(x, m, prec=0):
    if prec == 2:
        return jnp.dot(x, m, precision=lax.Precision.HIGHEST, preferred_element_type=F32)
    hi, lo = _split(x)
    mb = m.astype(BF16)
    return _bdot(hi, mb) + _bdot(lo, mb)


def _rms(x, g):
    return x * lax.rsqrt(jnp.mean(x * x, axis=-1, keepdims=True) + EPS) * g


def _sigmoid(x):
    return 1.0 / (1.0 + jnp.exp(-x))


def _silu(x):
    return x * _sigmoid(x)


def _log_sigmoid(x):
    return jnp.minimum(x, 0.0) - jnp.log(1.0 + jnp.exp(-jnp.abs(x)))


def _softplus(x):
    return jnp.maximum(x, 0.0) + jnp.log(1.0 + jnp.exp(-jnp.abs(x)))


def _gelu_tanh(x):
    return 0.5 * x * (1.0 + jnp.tanh(math.sqrt(2.0 / math.pi) * (x + 0.044715 * (x * x * x))))


def _lane_mask(width, lo, hi):
    lane = lax.broadcasted_iota(jnp.int32, (1, width), 1)
    return (lane >= lo) & (lane < hi)


def _lam_scalar(lv, lam_init):
    s01 = jnp.sum(lv[0:1] * lv[1:2], axis=-1, keepdims=True)
    s23 = jnp.sum(lv[2:3] * lv[3:4], axis=-1, keepdims=True)
    return jnp.exp(s01) - jnp.exp(s23) + lam_init


def _lam_init(layer):
    return 0.8 - 0.6 * math.exp(-0.3 * layer)


def _in_proj_kernel(x_ref, g_ref, wa_ref, wb_ref, wc_ref, wd_ref, qn_ref, kn_ref, m32_ref, w2_ref, gb_ref,
                    pq_ref, pk_ref, pv_ref, qa_ref, ka_ref, va_ref, sb_ref, sc_ref, sd_ref, *packs, prec, seq_len):
    xn = _rms(x_ref[...], g_ref[0])
    xs = _operand(xn, prec)
    a = _wdot(xs, wa_ref, (), prec)
    sb_ref[...] = _wdot(xs, wb_ref, (), prec)
    sc_ref[...] = _wdot(xs, wc_ref, (), prec)
    d = _wdot(xs, wd_ref, (), prec)
    q, k, v = a[:, 0:BW], a[:, BW:2 * BW], a[:, 2 * BW:3 * BW]
    m32 = m32_ref[...]
    qa = q * lax.rsqrt(_seg_mean(q * q, m32, prec) + EPS) * qn_ref[0] * (DQA ** -0.5)
    ka = k * lax.rsqrt(_seg_mean(k * k, m32, prec) + EPS) * kn_ref[0]
    qa_ref[...] = qa
    ka_ref[...] = ka
    va_ref[...] = v
    sd_ref[:, 0:768] = d[:, 0:768]
    z = _wdot(d[:, 768:896], w2_ref, (), prec) + gb_ref[0]
    sd_ref[:, 768:896] = _log_sigmoid(z) * (1.0 / GLA_TAU)
    if packs:
        def place(src, p_ref):
            hi, lo = _split(src)
            return _bdot(hi, p_ref[0]) + _bdot(lo, p_ref[1])

        tm = x_ref.shape[0]
        row = (pl.program_id(0) * tm + lax.broadcasted_iota(jnp.int32, (tm, 1), 0)).astype(F32)
        in_seq = row - jnp.floor((row + 0.5) * (1.0 / seq_len)) * seq_len
        is_pad = in_seq < PADL
        flag = (lax.broadcasted_iota(jnp.int32, (1, QK_PACK), 1) % 128 == PAD_LANE).astype(F32)
        qp_ref, kp_ref, vp_ref = packs
        qp_ref[...] = (place(qa * LOG2E, pq_ref) + jnp.where(is_pad, 0.0, 1.0) * flag).astype(BF16)
        kp_ref[...] = (place(ka, pk_ref) + jnp.where(is_pad, NEG, 0.0) * flag).astype(BF16)
        vp_ref[...] = place(v, pv_ref).astype(BF16)


def _pack_matrices():
    pq = np.zeros((2, BW, QK_PACK), np.float32)
    pk = np.zeros((2, BW, QK_PACK), np.float32)
    pv = np.zeros((2, BW, V_PACK), np.float32)
    for g in range(8):
        for d in range(DQA):
            r, c = DQA * g + d, 128 * g + d
            pq[0, r, c] = pq[0, r, c + 64] = 1.0
            pq[1, r, c + 32] = 1.0
            pk[0, r, c] = pk[0, r, c + 32] = 1.0
            pk[1, r, c + 64] = 1.0
    for h in range(HEADS):
        for d in range(64):
            pv[0, 64 * h + d, 128 * h + d] = 1.0
            pv[1, 64 * h + d, 128 * h + 64 + d] = 1.0
    return [jnp.asarray(a, BF16) for a in (pq, pk, pv)]


def _in_proj(x, p, tm, prec, seq_len=None):
    n = x.shape[0]
    packed = seq_len is not None
    layer = p["idx"]
    outs = [jax.ShapeDtypeStruct((n, BW), F32)] * 3 + [jax.ShapeDtypeStruct((n, W_B), F32),
            jax.ShapeDtypeStruct((n, W_C), F32), jax.ShapeDtypeStruct((n, W_D), F32)]
    if packed:
        outs += [jax.ShapeDtypeStruct((n, QK_PACK), BF16)] * 2 + [jax.ShapeDtypeStruct((n, V_PACK), BF16)]
    row = lambda w: pl.BlockSpec((tm, w), lambda i: (i, 0))
    params = [p["norm1"], p["wa"], p["wb"], p["wc"], p["wd"], p["qn_t"], p["kn_t"]]
    tail = [p["gla_w2p"], p["gla_b"]]
    consts = _pack_matrices()
    return pl.pallas_call(
        functools.partial(_in_proj_kernel, prec=prec, seq_len=seq_len),
        grid=(n // tm,),
        in_specs=([row(D_MODEL)] + [_layer_spec(a, layer) for a in params] + [_whole_spec(p["m32"])]
                  + [_layer_spec(a, layer) for a in tail] + [_whole_spec(a) for a in consts]),
        out_specs=[row(o.shape[1]) for o in outs],
        out_shape=outs,
        compiler_params=_cparams(("parallel",)),
        name="in_proj_p%d" % prec,
    )(x, *params, p["m32"], *tail, *consts)


def _attn_body(q_ref, k_ref, v_ref, lam, an_ref, o_ref, s_buf, w_buf, i, nk, *, lam_init, tq, prec):
    n_free = max(nk - ATTN_KEY_STEP, 0)
    pos_q = i * tq + lax.broadcasted_iota(jnp.int32, (tq, 1), 0)
    causal = n_free + lax.broadcasted_iota(jnp.int32, (1, nk - n_free), 1) <= pos_q
    lane = lax.broadcasted_iota(jnp.int32, (1, 128), 1)
    for g in range(2 * HEADS):
        s_buf[g, :, 0:nk] = _bdot(q_ref[:, 128 * g:128 * (g + 1)], k_ref[0:nk, 128 * g:128 * (g + 1)], NT)
    outs = []
    for h in range(HEADS):
        scale = []
        for m in range(2):
            g = 2 * h + m
            tail = jnp.where(causal, s_buf[g, :, n_free:nk], NEG)
            mx = jnp.max(tail, axis=-1, keepdims=True)
            if n_free:
                mx = jnp.maximum(mx, jnp.max(s_buf[g, :, 0:n_free], axis=-1, keepdims=True))
                head_p = jnp.exp2(s_buf[g, :, 0:n_free] - mx)
                s_buf[g, :, 0:n_free] = head_p
            tail_p = jnp.exp2(tail - mx)
            s_buf[g, :, n_free:nk] = tail_p
            total = jnp.sum(tail_p, axis=-1, keepdims=True)
            if n_free:
                total = total + jnp.sum(head_p, axis=-1, keepdims=True)
            scale.append(1.0 / total)
        w = s_buf[2 * h, :, 0:nk] * scale[0] - s_buf[2 * h + 1, :, 0:nk] * (lam * scale[1])
        vt = v_ref[0:nk, 128 * h:128 * (h + 1)]
        if prec == 1:
            w_hi, w_lo = _split(w)
            w_buf[h, 0:tq, 0:nk] = w_hi
            w_buf[h, tq:2 * tq, 0:nk] = w_lo
            o2 = _bdot(w_buf[h, :, 0:nk], vt)
            o = o2[0:tq] + o2[tq:2 * tq]
        else:
            w_buf[h, 0:tq, 0:nk] = w.astype(BF16)
            o = _bdot(w_buf[h, 0:tq, 0:nk], vt)
        o = o + pltpu.roll(o, 64, 1)
        outs.append(o * lax.rsqrt(jnp.mean(o * o, axis=-1, keepdims=True) + EPS))
    for t in range(2):
        pair = jnp.where(lane < 64, outs[2 * t], outs[2 * t + 1])
        o_ref[:, 128 * t:128 * (t + 1)] = (pair * an_ref[0][:, 128 * t:128 * (t + 1)]
                                           * (1.0 - lam_init)).astype(o_ref.dtype)


def _attn_kernel(q_ref, k_ref, v_ref, lam_ref, an_ref, o_ref, s_buf, w_buf, *, lam_init, tq, key_ranges, prec):
    i = pl.program_id(1)
    lam = _lam_scalar(lam_ref[0], lam_init)
    need = (i + 1) * tq
    lo = 0
    for nk in key_ranges:
        @pl.when((need > lo) & (need <= nk))
        def _(nk=nk):
            _attn_body(q_ref, k_ref, v_ref, lam, an_ref, o_ref, s_buf, w_buf, i, nk, lam_init=lam_init, tq=tq,
                       prec=prec)
        lo = nk


def _attn_prompt(qp, kp, vp, p, b, lp, prec):
    nq = lp // BLK
    layer = p["idx"]
    key_ranges = tuple(sorted({min(lp, ATTN_KEY_STEP * j) for j in range(1, -(-lp // ATTN_KEY_STEP) + 1)}))
    return pl.pallas_call(
        functools.partial(_attn_kernel, lam_init=_lam_init(p["layer"]), tq=BLK, key_ranges=key_ranges,
                          prec=prec),
        grid=(b, nq),
        in_specs=[pl.BlockSpec((BLK, QK_PACK), lambda bi, i: (bi * nq + i, 0)),
                  pl.BlockSpec((lp, QK_PACK), lambda bi, i: (bi, 0)),
                  pl.BlockSpec((lp, V_PACK), lambda bi, i: (bi, 0)),
                  _layer_spec(p["lam"], layer), _layer_spec(p["an_t"], layer)],
        out_specs=pl.BlockSpec((BLK, BW), lambda bi, i: (bi * nq + i, 0)),
        out_shape=jax.ShapeDtypeStruct((b * lp, BW), F32),
        scratch_shapes=[pltpu.VMEM((2 * HEADS, BLK, lp), F32), pltpu.VMEM((HEADS, 2 * BLK, lp), BF16)],
        compiler_params=_cparams(("parallel", "parallel")),
        name="attn_prompt_p%d" % prec,
    )(qp, kp, vp, p["lam"], p["an_t"])


def _swap_halves(x):
    w = x.shape[-1]
    lane = lax.broadcasted_iota(jnp.int32, (1, w), 1)
    return jnp.where((lane % 64) < 32, pltpu.roll(x, w - 32, 1), pltpu.roll(x, 32, 1))


def _ret_kernel(sb_ref, cos_ref, sin_ref, dmat_ref, dq_ref, dk_ref, dstate_ref, rn_ref, m64_ref,
                o_ref, s_out_ref, s_ref, *, prec):
    c = pl.program_id(1)

    @pl.when(c == 0)
    def _():
        s_ref[...] = jnp.zeros_like(s_ref)

    cos, sin = cos_ref[...], sin_ref[...]
    q, k, v = sb_ref[:, 0:BW], sb_ref[:, BW:2 * BW], sb_ref[:, 2 * BW:3 * BW]
    qr = q * cos + _swap_halves(q) * sin
    kr = (k * cos + _swap_halves(k) * sin) * 0.125
    o = _adot(qr, s_ref[...], prec) * dq_ref[...]
    upd = _adot(kr * dk_ref[...], v, prec, TN)
    masks = [_lane_mask(BW, 64 * h, 64 * (h + 1)) for h in range(HEADS)]
    scores = [_adot(jnp.where(hm, qr, 0.0), kr, prec, NT) for hm in masks]
    for h, hm in enumerate(masks):
        o = o + jnp.where(hm, _adot(scores[h] * dmat_ref[h], v, prec), 0.0)
    row_head = lax.broadcasted_iota(jnp.int32, (BW, BW), 0) // 64
    col_head = lax.broadcasted_iota(jnp.int32, (BW, BW), 1) // 64
    s_new = s_ref[...] * dstate_ref[...] + jnp.where(row_head == col_head, upd, 0.0)
    s_ref[...] = s_new
    s_out_ref[0] = s_new
    m64 = m64_ref[...]
    mu = _seg_mean(o, m64)
    var = _seg_mean((o - mu) * (o - mu), m64)
    g = sb_ref[:, 3 * BW:4 * BW]
    o_ref[...] = (o - mu) * lax.rsqrt(var + EPS) * rn_ref[0] * _silu(g)


def _ret_gammas():
    return 1.0 - 2.0 ** (-5.0 - np.arange(HEADS, dtype=np.float64))


def _ret_tables():
    lg = np.log(_ret_gammas())
    t = np.arange(BLK, dtype=np.float64)
    rel = t[:, None] - t[None, :]
    dmat = np.where(rel >= 0, np.exp(np.maximum(rel, 0.0)[None] * lg[:, None, None]), 0.0)
    lane_lg = np.repeat(lg, 64)[None, :]
    dq = np.exp((t[:, None] + 1.0) * lane_lg)
    dk = np.exp((BLK - 1.0 - t)[:, None] * lane_lg)
    dstate = np.exp(BLK * lane_lg)
    return [jnp.asarray(a, F32) for a in (dmat, dq, dk, dstate)]


def _rope_cos_sin(pos):
    inv = ROPE_BASE ** (-jnp.linspace(0.0, 1.0, 32))
    ang = pos.astype(F32)[:, None] * inv[None]
    return jnp.cos(ang), jnp.sin(ang)


def _ret_prompt(sb, p, b, lp, prec):
    nc = lp // BLK
    cos, sin = _rope_cos_sin(jnp.arange(lp) - PADL)
    cos_t = jnp.tile(jnp.concatenate([cos, cos], -1), (1, HEADS))
    sin_t = jnp.tile(jnp.concatenate([-sin, sin], -1), (1, HEADS))
    tables = _ret_tables()
    return pl.pallas_call(
        functools.partial(_ret_kernel, prec=prec),
        grid=(b, nc),
        in_specs=[pl.BlockSpec((BLK, W_B), lambda bi, c: (bi * nc + c, 0)),
                  pl.BlockSpec((BLK, BW), lambda bi, c: (c, 0)),
                  pl.BlockSpec((BLK, BW), lambda bi, c: (c, 0))]
                 + [_whole_spec(a) for a in tables] + [_layer_spec(p["rn_t"], p["idx"]), _whole_spec(p["m64"])],
        out_specs=[pl.BlockSpec((BLK, BW), lambda bi, c: (bi * nc + c, 0)),
                   pl.BlockSpec((1, BW, BW), lambda bi, c: (bi, 0, 0))],
        out_shape=[jax.ShapeDtypeStruct((b * lp, BW), F32), jax.ShapeDtypeStruct((b, BW, BW), F32)],
        scratch_shapes=[pltpu.VMEM((BW, BW), F32)],
        compiler_params=_cparams(("parallel", "arbitrary")),
        name="ret_prompt_p%d" % prec,
    )(sb, cos_t, sin_t, *tables, p["rn_t"], p["m64"])


def _lru_gates(y, wa_ref, wx_ref, ba, bx, sp, prec):
    ys = _operand(y, prec)
    r = _sigmoid(_wdot(ys, wa_ref, (), prec) + ba)
    ig = _sigmoid(_wdot(ys, wx_ref, (), prec) + bx)
    log_a = -LRU_C * r * sp
    a = jnp.exp(log_a)
    u = jnp.sqrt(1.0 - jnp.exp(2.0 * log_a)) * (ig * y)
    return a, u


def _lru_kernel(sc_ref, cw_ref, cb_ref, wa_ref, wx_ref, ba_ref, bx_ref, lam_ref,
                o_ref, hl_ref, cbuf, a_s, u_s, h_s, hcar, *, nb, prec):
    t = pl.program_id(0)

    @pl.when(t == 0)
    def _():
        cbuf[...] = jnp.zeros_like(cbuf)
        hcar[...] = jnp.zeros_like(hcar)

    sp = _softplus(-lam_ref[0])
    row = t * BLK + lax.broadcasted_iota(jnp.int32, (BLK, 1), 0)
    for b in range(nb):
        x = sc_ref[b, :, 0:BW]
        cbuf[b, 8:8 + BLK, :] = x
        y = cb_ref[0] + cbuf[b, 5:5 + BLK, :] * cw_ref[0, 0:1, :]
        for i in range(1, CONV_W):
            y = y + cbuf[b, 5 + i:5 + i + BLK, :] * cw_ref[0, i:i + 1, :]
        cbuf[b, 0:8, :] = x[BLK - 8:BLK, :]
        a, u = _lru_gates(y, wa_ref, wx_ref, ba_ref[0], bx_ref[0], sp, prec)
        a_s[b] = a
        u_s[b] = jnp.where(row >= PADL, u, 0.0)

    def step(i, hs):
        out = []
        for b in range(nb):
            h = a_s[b, pl.ds(i, 1), :] * hs[b] + u_s[b, pl.ds(i, 1), :]
            h_s[b, pl.ds(i, 1), :] = h
            out.append(h)
        return tuple(out)

    hs = lax.fori_loop(0, BLK, step, tuple(hcar[b:b + 1, :] for b in range(nb)))
    for b in range(nb):
        hcar[b:b + 1, :] = hs[b]
        o_ref[b] = h_s[b] * _gelu_tanh(sc_ref[b, :, BW:2 * BW])
    hl_ref[...] = hcar[...]


def _lru_prompt(sc, p, b, lp, prec):
    nt = lp // BLK
    sc3 = sc.reshape(b, lp, W_C)
    params = [p["conv_w"], p["conv_b"], p["lru_wa_bd"], p["lru_wx_bd"], p["lru_ba"], p["lru_bx"], p["lru_lam"]]
    o, hl = pl.pallas_call(
        functools.partial(_lru_kernel, nb=b, prec=prec),
        grid=(nt,),
        in_specs=[pl.BlockSpec((b, BLK, W_C), lambda t: (0, t, 0))] + [_layer_spec(a, p["idx"]) for a in params],
        out_specs=[pl.BlockSpec((b, BLK, BW), lambda t: (0, t, 0)), pl.BlockSpec((b, BW), lambda t: (0, 0))],
        out_shape=[jax.ShapeDtypeStruct((b, lp, BW), F32), jax.ShapeDtypeStruct((b, BW), F32)],
        scratch_shapes=[pltpu.VMEM((b, BLK + 8, BW), F32), pltpu.VMEM((b, BLK, BW), F32),
                        pltpu.VMEM((b, BLK, BW), F32), pltpu.VMEM((b, BLK, BW), F32), pltpu.VMEM((b, BW), F32)],
        compiler_params=_cparams(("arbitrary",)),
        name="lru_prompt_p%d" % prec,
    )(sc3, *params)
    return o.reshape(b * lp, BW), hl


def _gla_kernel(sd_ref, tri_ref, ind_ref, gn_ref, m64_ref, o_ref, st_out_ref, st_ref, o_s, upd_s, *, prec):
    c = pl.program_id(1)

    @pl.when(c == 0)
    def _():
        st_ref[...] = jnp.zeros_like(st_ref)

    q = sd_ref[:, 0:128] * (DQA ** -0.5)
    k = sd_ref[:, 128:256]
    v = sd_ref[:, 256:512]
    la = sd_ref[:, 768:896]
    tri = tri_ref[...]
    bc = sum(_bdot(tri, part) for part in _split3(la))
    ind = ind_ref[...]
    t_idx = lax.broadcasted_iota(jnp.int32, (GLA_CHUNK, 1), 0)
    row_head = lax.broadcasted_iota(jnp.int32, (BW, 128), 0) // 64
    col_head = lax.broadcasted_iota(jnp.int32, (BW, 128), 1) // 32
    bd = row_head == col_head
    n_sub = BLK // GLA_CHUNK

    def local_cumsum(s):
        r0 = s * GLA_CHUNK
        bcs = bc[r0:r0 + GLA_CHUNK]
        return bcs - bc[r0 - 1:r0] if s > 0 else bcs

    for s in range(n_sub):
        r0 = s * GLA_CHUNK
        bcs = local_cumsum(s)
        qs, ks, vs = q[r0:r0 + GLA_CHUNK], k[r0:r0 + GLA_CHUNK], v[r0:r0 + GLA_CHUNK]
        zs = []
        for j in range(GLA_CHUNK):
            dec = jnp.where(t_idx >= j, jnp.exp(jnp.minimum(bcs - bcs[j:j + 1], 0.0)), 0.0)
            zs.append(qs * dec * ks[j:j + 1])
        z = jnp.concatenate(zs, axis=0)
        if prec == 1:
            z_hi, z_lo = _split(z)
            aa = _bdot(z_hi, ind) + _bdot(z_lo, ind)
        else:
            aa = _bdot(z.astype(BF16), ind)
        o = aa[0:GLA_CHUNK] * vs[0:1]
        for j in range(1, GLA_CHUNK):
            o = o + aa[j * GLA_CHUNK:(j + 1) * GLA_CHUNK] * vs[j:j + 1]
        o_s[r0:r0 + GLA_CHUNK, :] = o
        last = bcs[GLA_CHUNK - 1:GLA_CHUNK]
        upd_s[s] = jnp.where(bd, _adot(vs, ks * jnp.exp(last - bcs), prec, TN), 0.0)
    st = st_ref[...]
    for s in range(n_sub):
        r0 = s * GLA_CHUNK
        bcs = local_cumsum(s)
        o_s[r0:r0 + GLA_CHUNK, :] += _adot(q[r0:r0 + GLA_CHUNK] * jnp.exp(bcs), st, prec, NT)
        st = st * jnp.exp(bcs[GLA_CHUNK - 1:GLA_CHUNK]) + upd_s[s]
    st_ref[...] = st
    st_out_ref[0] = st
    o = o_s[...]
    ms = _seg_mean(o * o, m64_ref[...])
    o_ref[...] = o * lax.rsqrt(ms + EPS) * gn_ref[0] * _silu(sd_ref[:, 512:768])


def _gla_prompt(sd, p, b, lp, prec):
    nc = lp // BLK
    t = np.arange(BLK)
    tri = jnp.asarray(t[:, None] >= t[None, :], BF16)
    ind = jnp.asarray((np.arange(128)[:, None] // 32) == (np.arange(BW)[None, :] // 64), BF16)
    return pl.pallas_call(
        functools.partial(_gla_kernel, prec=prec),
        grid=(b, nc),
        in_specs=[pl.BlockSpec((BLK, W_D), lambda bi, c: (bi * nc + c, 0)),
                  _whole_spec(tri), _whole_spec(ind), _layer_spec(p["gn_t"], p["idx"]), _whole_spec(p["m64"])],
        out_specs=[pl.BlockSpec((BLK, BW), lambda bi, c: (bi * nc + c, 0)),
                   pl.BlockSpec((1, BW, 128), lambda bi, c: (bi, 0, 0))],
        out_shape=[jax.ShapeDtypeStruct((b * lp, BW), F32), jax.ShapeDtypeStruct((b, BW, 128), F32)],
        scratch_shapes=[pltpu.VMEM((BW, 128), F32), pltpu.VMEM((BLK, BW), F32),
                        pltpu.VMEM((BLK // GLA_CHUNK, BW, 128), F32)],
        compiler_params=_cparams(("parallel", "arbitrary")),
        name="gla_prompt_p%d" % prec,
    )(sd, tri, ind, p["gn_t"], p["m64"])


def _merge_kernel(x_ref, g_ref, oa_ref, ob_ref, oc_ref, od_ref, wm_ref, bm_ref, wbr_ref, wo_ref, y_ref, *, prec):
    x = x_ref[...]
    xn = _rms(x, g_ref[0])
    xs = _operand(xn, prec)
    acc = None
    for n, br in enumerate((oa_ref, ob_ref, oc_ref, od_ref)):
        cols = slice(n * D_MODEL, (n + 1) * D_MODEL)
        gate = _sigmoid(_wdot(xs, wm_ref, (slice(None), cols), prec) + bm_ref[0, :, cols])
        term = gate * _wdot(br[...], wbr_ref, (n,), prec)
        acc = term if acc is None else acc + term
    y_ref[...] = x + _wdot(acc, wo_ref, (), prec)


def _merge(x, oa, ob, oc, od, p, tm, prec):
    n = x.shape[0]
    layer = p["idx"]
    row = lambda w: pl.BlockSpec((tm, w), lambda i: (i, 0))
    params = [p["w_merge"], p["b_merge"], p["w_branch"], p["w_out"]]
    return pl.pallas_call(
        functools.partial(_merge_kernel, prec=prec),
        grid=(n // tm,),
        in_specs=[row(D_MODEL), _layer_spec(p["norm1"], layer), row(BW), row(BW), row(BW), row(BW)]
                 + [_layer_spec(a, layer) for a in params],
        out_specs=row(D_MODEL),
        out_shape=jax.ShapeDtypeStruct((n, D_MODEL), F32),
        compiler_params=_cparams(("parallel",)),
        name="merge_p%d" % prec,
    )(x, p["norm1"], oa, ob, oc, od, *params)


def _first_argmax(vals, valid, lane):
    v = jnp.where(valid, vals, NEG)
    top = jnp.max(v, axis=-1, keepdims=True)
    idx = jnp.min(jnp.where(valid & (v == top), lane, 1 << 20), axis=-1, keepdims=True)
    return top, idx


def _router_gate(lg, le):
    lane = lax.broadcasted_iota(jnp.int32, lg.shape, 1)
    gvalid = lane < N_GROUPS
    lgm = jnp.where(gvalid, lg, NEG)
    pe = jnp.exp(lgm - jnp.max(lgm, axis=-1, keepdims=True))
    pg = pe / jnp.sum(pe, axis=-1, keepdims=True)
    pg_top, g_idx = _first_argmax(pg, gvalid, lane)
    sel = (lane // EXP_PER_GROUP == g_idx) & (lane < N_EXPERTS)
    v1, i1 = _first_argmax(le, sel, lane)
    v2, i2 = _first_argmax(le, sel & (lane != i1), lane)
    e2 = jnp.exp(v2 - v1)
    den = 1.0 + e2
    return jnp.where(lane == i1, pg_top / den, jnp.where(lane == i2, pg_top * e2 / den, 0.0))


def _moe_kernel(h_ref, g_ref, rw_ref, rb_ref, sel_ref, w1_ref, w3_ref, w2_ref, y_ref, t_s, gate_s, *, prec):
    grp = pl.program_id(1)

    @pl.when(grp == 0)
    def _():
        x = h_ref[...]
        t = _rms(x, g_ref[0])
        t_s[...] = t.astype(t_s.dtype)
        logits = _wdot(t, rw_ref, (), max(prec, 1)) + rb_ref[0]
        gate_s[...] = _router_gate(logits[:, 0:128], logits[:, 128:256])
        y_ref[...] = x

    t = t_s[...]
    gcols = _seg_mean(gate_s[...], sel_ref[grp], prec)
    hs = []
    for j in range(EXP_PER_GROUP):
        gj = gcols[:, 128 * j:128 * (j + 1)]
        hj = _silu(_wdot(t, w1_ref, (j,), prec)) * _wdot(t, w3_ref, (j,), prec)
        hs.append(hj * jnp.concatenate([gj, gj], axis=1))
    acc = _wdot(hs[0], w2_ref, (0,), prec)
    for j in range(1, EXP_PER_GROUP):
        acc = acc + _wdot(hs[j], w2_ref, (j,), prec)
    y_ref[...] += acc


def _moe(h, p, tm, prec):
    n = h.shape[0]
    layer = p["idx"]
    wspec = lambda a: pl.BlockSpec((1, a.shape[1], EXP_PER_GROUP) + a.shape[3:],
                                   lambda i, g: (layer, 0, g, 0, 0))
    sel = np.zeros((N_GROUPS, 128, EXP_PER_GROUP * 128), np.float32)
    for g in range(N_GROUPS):
        for j in range(EXP_PER_GROUP):
            sel[g, EXP_PER_GROUP * g + j, 128 * j:128 * (j + 1)] = 1.0
    sel = jnp.asarray(sel)
    return pl.pallas_call(
        functools.partial(_moe_kernel, prec=prec),
        grid=(n // tm, N_GROUPS),
        in_specs=[pl.BlockSpec((tm, D_MODEL), lambda i, g: (i, 0)), _layer_spec(p["norm2"], layer),
                  _layer_spec(p["rw"], layer), _layer_spec(p["rb"], layer), _whole_spec(sel),
                  wspec(p["ex_w1"]), wspec(p["ex_w3"]), wspec(p["ex_w2"])],
        out_specs=pl.BlockSpec((tm, D_MODEL), lambda i, g: (i, 0)),
        out_shape=jax.ShapeDtypeStruct((n, D_MODEL), F32),
        scratch_shapes=[pltpu.VMEM((tm, D_MODEL), F32 if prec == 2 else BF16), pltpu.VMEM((tm, 128), F32)],
        compiler_params=_cparams(("parallel", "arbitrary")),
        name="moe_p%d" % prec,
    )(h, p["norm2"], p["rw"], p["rb"], sel, p["ex_w1"], p["ex_w3"], p["ex_w2"])


def _sublane_total(x):
    for shift in (4, 2, 1):
        x = x + pltpu.roll(x, shift, 0)
    return x


def _lane_all(x, op):
    return jnp.broadcast_to(op(x, axis=-1, keepdims=True), x.shape)


def _dec_attn_kernel(pt_ref, q_ref, kn_ref, vn_ref, lam_ref, *rest, pp, lam_init):
    del pt_ref
    kp, vp = rest[:pp], rest[pp:2 * pp]
    o_ref, s_s, m_s, l_s, acc_s = rest[2 * pp:]
    step = pl.program_id(1)

    @pl.when(step == 0)
    def _():
        m_s[...] = jnp.full_like(m_s, NEG)
        l_s[...] = jnp.zeros_like(l_s)
        acc_s[...] = jnp.zeros_like(acc_s)

    def map_scores(t, m):
        part = t[4 * m] + t[4 * m + 1] + t[4 * m + 2] + t[4 * m + 3]
        return _sublane_total(part)

    tops = [None] * (2 * HEADS)
    for j in range(pp):
        for h in range(HEADS):
            t = kp[j][0, 0, h] * q_ref[0, h]
            for m in range(2):
                g = 2 * h + m
                sc = map_scores(t, m)
                s_s[g, j] = sc
                tops[g] = sc if tops[g] is None else jnp.maximum(tops[g], sc)
    for h in range(HEADS):
        m_new, accs, sums = [], [], []
        for m in range(2):
            g = 2 * h + m
            m_old = m_s[g]
            m_new.append(jnp.maximum(m_old, _lane_all(tops[g], jnp.max)))
            alpha = jnp.exp(m_old - m_new[m])
            accs.append(acc_s[g] * alpha)
            sums.append(None)
            l_s[g] = l_s[g] * alpha
            m_s[g] = m_new[m]
        for j in range(pp):
            vt = vp[j][0, 0, h]
            for m in range(2):
                pr = jnp.exp(s_s[2 * h + m, j] - m_new[m])
                accs[m] = accs[m] + vt * pr
                sums[m] = pr if sums[m] is None else sums[m] + pr
        for m in range(2):
            g = 2 * h + m
            acc_s[g] = accs[m]
            l_s[g] = l_s[g] + _lane_all(sums[m], jnp.sum)

    @pl.when(step == pl.num_programs(1) - 1)
    def _():
        lam = _lam_scalar(lam_ref[0], lam_init)
        for h in range(HEADS):
            t = q_ref[0, h] * kn_ref[0, h]
            maps = []
            for m in range(2):
                g = 2 * h + m
                sc = map_scores(t, m)
                m_o = m_s[g][:, 0:1]
                m_n = jnp.maximum(m_o, sc)
                al = jnp.exp(m_o - m_n)
                pn = jnp.exp(sc - m_n)
                num = al * jnp.sum(acc_s[g], axis=-1, keepdims=True) + pn * vn_ref[0, h]
                maps.append(num / (al * l_s[g][:, 0:1] + pn))
            o_ref[0, h] = maps[0] - lam * maps[1]


def _dec_attn(qa, ka, va, cache_k, cache_v, page_table, p):
    db = qa.shape[0]
    layer = p["layer"]
    n_pages = page_table.shape[1]
    n_phys = cache_k.shape[1]
    pp = math.gcd(PAGES_PER_STEP, n_pages)
    tiles = lambda c: c.transpose(0, 1, 3, 4, 2).reshape(c.shape[0], n_phys, HEADS, 8, 8, PAGE)
    ck, cv = tiles(cache_k), tiles(cache_v)
    page = lambda j: pl.BlockSpec((1, 1, HEADS, 8, 8, PAGE),
                                  lambda b, s, pt: (layer, pt[b, s * pp + j], 0, 0, 0, 0))
    col = pl.BlockSpec((1, HEADS, 8, 8, 1), lambda b, s, pt: (b, 0, 0, 0, 0))
    lam = p["lam"]
    grid_spec = pltpu.PrefetchScalarGridSpec(
        num_scalar_prefetch=1,
        grid=(db, n_pages // pp),
        in_specs=[col, col, col, pl.BlockSpec((1,) + lam.shape[1:], lambda b, s, pt: (p["idx"], 0, 0))]
                 + [page(j) for j in range(pp)] + [page(j) for j in range(pp)],
        out_specs=col,
        scratch_shapes=[pltpu.VMEM((2 * HEADS, pp, 8, PAGE), F32), pltpu.VMEM((2 * HEADS, 8, PAGE), F32),
                        pltpu.VMEM((2 * HEADS, 8, PAGE), F32), pltpu.VMEM((2 * HEADS, 8, 8, PAGE), F32)],
    )
    cols = lambda x: x.reshape(db, HEADS, 8, 8, 1)
    out = pl.pallas_call(
        functools.partial(_dec_attn_kernel, pp=pp, lam_init=_lam_init(layer)),
        grid_spec=grid_spec,
        out_shape=jax.ShapeDtypeStruct((db, HEADS, 8, 8, 1), F32),
        compiler_params=_cparams(("parallel", "arbitrary")),
        name="attn_sample",
    )(page_table, cols(qa), cols(ka), cols(va), lam, *([ck] * pp), *([cv] * pp))
    return out.reshape(db * HEADS, 64).T


def _step_kernel(oa_ref, an_ref,
                 bq1_ref, bq2_ref, bk1_ref, bk2_ref, bv_ref, bg_ref, s0_ref, cos_ref, sin_ref, gam_ref, rn_ref,
                 dq_ref, dk_ref, dv_ref, dr_ref, la_ref, g0_ref, gn_ref,
                 cx_ref, cg_ref, buf_ref, h0_ref, cw_ref, cb_ref, wa_ref, wx_ref, ba_ref, bx_ref, lam_ref,
                 oa_o, ob_o, od_o, oc_o, s_o, g_o, h_o, *, lam_init):
    x = oa_ref[...]
    oa_o[...] = x * lax.rsqrt(jnp.mean(x * x, axis=0, keepdims=True) + EPS) * an_ref[0] * (1.0 - lam_init)

    cos, sin, gam = cos_ref[...], sin_ref[...], gam_ref[...]
    rot = lambda x1, x2: jnp.concatenate([x1 * cos - x2 * sin, x1 * sin + x2 * cos], axis=0)
    qb = rot(bq1_ref[...], bq2_ref[...])
    kb = rot(bk1_ref[...], bk2_ref[...]) * 0.125
    v = bv_ref[...]
    o = jnp.sum(qb * kb, axis=0, keepdims=True) * v
    st = jnp.zeros_like(v)
    for k in range(64):
        s0k = s0_ref[k]
        st = st + s0k * qb[k:k + 1]
        s_o[k] = gam * s0k + kb[k:k + 1] * v
    o = o + gam * st
    mu = jnp.mean(o, axis=0, keepdims=True)
    var = jnp.mean((o - mu) * (o - mu), axis=0, keepdims=True)
    ob_o[...] = (o - mu) * lax.rsqrt(var + EPS) * rn_ref[0] * _silu(bg_ref[...])

    qd = dq_ref[...] * (DQA ** -0.5)
    kd = dk_ref[...]
    vd = dv_ref[...]
    dec = jnp.exp(la_ref[...])
    o = jnp.sum(qd * kd, axis=0, keepdims=True) * vd
    qe = qd * dec
    for k in range(32):
        g0k = g0_ref[k]
        o = o + g0k * qe[k:k + 1]
        g_o[k] = dec[k:k + 1] * g0k + kd[k:k + 1] * vd
    od_o[...] = o * lax.rsqrt(jnp.mean(o * o, axis=0, keepdims=True) + EPS) * gn_ref[0] * _silu(dr_ref[...])

    cx = cx_ref[...]
    y = cb_ref[0] + cx * cw_ref[0, CONV_W - 1:CONV_W, :]
    for i in range(CONV_W - 1):
        y = y + buf_ref[i] * cw_ref[0, i:i + 1, :]
    a, u = _lru_gates(y, wa_ref, wx_ref, ba_ref[0], bx_ref[0], _softplus(-lam_ref[0]), 2)
    h = a * h0_ref[...] + u
    h_o[...] = h
    oc_o[...] = h * _gelu_tanh(cg_ref[...])


def _to_fm(x, db, width):
    return x.reshape(db, HEADS, width).transpose(2, 0, 1).reshape(width, db * HEADS)


def _from_fm(x, db, width):
    return x.reshape(width, db, HEADS).transpose(1, 2, 0).reshape(db, HEADS * width)


def _halves_fm(x, db):
    x4 = x.reshape(db, HEADS, 2, 32).transpose(2, 3, 0, 1).reshape(2, 32, db * HEADS)
    return x4[0], x4[1]


def _sample_step(oa_fm, sb, sc, sd, ret0, gla0, conv_buf, h0, p, pos):
    db = sb.shape[0]
    n = db * HEADS
    layer = p["idx"]
    cos, sin = _rope_cos_sin(jnp.asarray([pos]))
    gam = jnp.tile(jnp.asarray(_ret_gammas(), F32), db)[None, :]
    bq1, bq2 = _halves_fm(sb[:, 0:BW], db)
    bk1, bk2 = _halves_fm(sb[:, BW:2 * BW], db)
    args = [
        oa_fm, p["an_c"],
        bq1, bq2, bk1, bk2, _to_fm(sb[:, 2 * BW:3 * BW], db, 64), _to_fm(sb[:, 3 * BW:4 * BW], db, 64),
        ret0.transpose(2, 3, 0, 1).reshape(64, 64, n), cos.reshape(32, 1), sin.reshape(32, 1), gam, p["rn_c"],
        _to_fm(sd[:, 0:128], db, 32), _to_fm(sd[:, 128:256], db, 32), _to_fm(sd[:, 256:512], db, 64),
        _to_fm(sd[:, 512:768], db, 64), _to_fm(sd[:, 768:896], db, 32),
        gla0.transpose(2, 3, 0, 1).reshape(32, 64, n), p["gn_c"],
        sc[:, 0:BW], sc[:, BW:2 * BW], conv_buf.transpose(1, 0, 2), h0,
        p["conv_w"], p["conv_b"], p["lru_wa_bd"], p["lru_wx_bd"], p["lru_ba"], p["lru_bx"], p["lru_lam"],
    ]
    layered = {1, 12, 19, 24, 25, 26, 27, 28, 29, 30}
    in_specs = [_layer_spec(a, layer) if i in layered else _whole_spec(a) for i, a in enumerate(args)]
    outs = [jax.ShapeDtypeStruct((64, n), F32)] * 3 + [jax.ShapeDtypeStruct((db, BW), F32),
            jax.ShapeDtypeStruct((64, 64, n), F32), jax.ShapeDtypeStruct((32, 64, n), F32),
            jax.ShapeDtypeStruct((db, BW), F32)]
    oa, ob, od, oc, s_new, g_new, h_new = pl.pallas_call(
        functools.partial(_step_kernel, lam_init=_lam_init(p["layer"])),
        grid=(1,),
        in_specs=in_specs,
        out_specs=[pl.BlockSpec(o.shape, lambda i, nd=len(o.shape): (0,) * nd) for o in outs],
        out_shape=outs,
        compiler_params=_cparams(("arbitrary",)),
        name="step_sample",
    )(*args)
    ret_new = s_new.reshape(64, 64, db, HEADS).transpose(2, 3, 0, 1)
    gla_new = g_new.reshape(32, 64, db, HEADS).transpose(2, 3, 0, 1)
    return _from_fm(oa, db, 64), _from_fm(ob, db, 64), oc, _from_fm(od, db, 64), ret_new, h_new, gla_new


def _block_diag(w):
    d, h, n, _ = w.shape
    eye = jnp.eye(h, dtype=w.dtype)
    return jnp.einsum("dhij,hg->dhigj", w, eye).reshape(d, h * n, h * n)


def _seg_matrix(seg):
    idx = np.arange(BW)
    return jnp.asarray((idx[:, None] // seg == idx[None, :] // seg) / float(seg), F32)


def _parts(w, prec):
    if prec == 2:
        return w[:, None]
    if prec == 0:
        return w.astype(BF16)[:, None]
    return jnp.stack(_split(w), axis=1)


def _prepare(raw, layers, prec, moe_prec, layer):
    sl = lambda a: a[layers]
    (norm1, w_in, qn, kn, lam, an, rn, conv_w, conv_b, lru_wa, lru_ba, lru_wx, lru_bx, lru_lam, gla_w2, gla_b,
     gn, w_branch, w_merge, b_merge, w_out, norm2, rg_w, rg_b, re_w, re_b, ex_w1, ex_w3, ex_w2) = raw
    nl = sl(norm1).shape[0]
    r3 = lambda a: sl(a).reshape(nl, 1, -1)
    tile3 = lambda a, k: jnp.tile(sl(a), (1, k)).reshape(nl, 1, -1)
    pad_cols = lambda a, w: jnp.pad(a, ((0, 0), (0, 0), (0, w - a.shape[-1])))
    w_in = sl(w_in)
    c = np.cumsum([0, 768, 1024, 512, 784])
    rw = jnp.concatenate([pad_cols(sl(rg_w), 128), pad_cols(sl(re_w), 128)], axis=-1)
    return dict(
        idx=layer if nl > 1 else 0, layer=layer,
        norm1=r3(norm1), qn_t=tile3(qn, 8), kn_t=tile3(kn, 8), lam=sl(lam), an_t=tile3(an, 4), rn_t=tile3(rn, 4),
        gn_t=tile3(gn, 4), an_c=sl(an).reshape(nl, 64, 1), rn_c=sl(rn).reshape(nl, 64, 1),
        gn_c=sl(gn).reshape(nl, 64, 1), conv_w=sl(conv_w), conv_b=r3(conv_b), lru_ba=r3(lru_ba),
        lru_bx=r3(lru_bx), lru_lam=r3(lru_lam), gla_b=r3(gla_b), b_merge=r3(b_merge), norm2=r3(norm2),
        rb=jnp.concatenate([pad_cols(r3(rg_b), 128), pad_cols(r3(re_b), 128)], axis=-1),
        m32=_seg_matrix(32), m64=_seg_matrix(64),
        wa=_parts(w_in[:, :, c[0]:c[1]], prec), wb=_parts(w_in[:, :, c[1]:c[2]], prec),
        wc=_parts(w_in[:, :, c[2]:c[3]], prec), wd=_parts(pad_cols(w_in[:, :, c[3]:c[4]], W_D), prec),
        gla_w2p=_parts(jnp.pad(sl(gla_w2), ((0, 0), (0, 128 - GLA_RANK), (0, 0))), prec),
        lru_wa_bd=_parts(_block_diag(sl(lru_wa)), prec), lru_wx_bd=_parts(_block_diag(sl(lru_wx)), prec),
        w_merge=_parts(sl(w_merge), prec), w_branch=_parts(sl(w_branch), prec), w_out=_parts(sl(w_out), prec),
        rw=_parts(rw, max(moe_prec, 1)),
        ex_w1=_parts(sl(ex_w1), moe_prec), ex_w3=_parts(sl(ex_w3), moe_prec), ex_w2=_parts(sl(ex_w2), moe_prec),
    )


def _diag_blocks(s, rows, cols):
    b = s.shape[0]
    s5 = s.reshape(b, HEADS, rows, HEADS, cols)
    return jnp.stack([s5[:, h, :, h, :] for h in range(HEADS)], axis=1)


def _prompt_tile(n, prec):
    for tm in ((256, 128) if prec else (512, 256, 128)):
        if n % tm == 0:
            return tm
    raise ValueError(n)


def kernel(x_prompt, x_sample, cache_k_a, cache_v_a, state_ret, state_lru_h, state_lru_conv, state_gla,
           page_table, meta, norm1, w_in, qn, kn, lam, an, rn, conv_w, conv_b, lru_wa, lru_ba, lru_wx,
           lru_bx, lru_lam, gla_w2, gla_b, gn, w_branch, w_merge, b_merge, w_out, norm2, rg_w, rg_b,
           re_w, re_b, ex_w1, ex_w3, ex_w2):
    bp, seq, _ = x_prompt.shape
    db = x_sample.shape[0]
    depth = w_in.shape[0]
    assert seq % BLK == 0 and x_sample.shape[1] == 1
    lp = BLK + seq
    past = page_table.shape[1] * PAGE
    raw = (norm1, w_in, qn, kn, lam, an, rn, conv_w, conv_b, lru_wa, lru_ba, lru_wx, lru_bx, lru_lam, gla_w2,
           gla_b, gn, w_branch, w_merge, b_merge, w_out, norm2, rg_w, rg_b, re_w, re_b, ex_w1, ex_w3, ex_w2)
    head = jnp.concatenate([jnp.zeros((PADL, D_MODEL), F32), meta.astype(F32)], axis=0)
    xp = jnp.concatenate([jnp.broadcast_to(head[None], (bp, BLK, D_MODEL)), x_prompt], axis=1)
    xp = xp.reshape(bp * lp, D_MODEL)
    xs = x_sample.reshape(db, D_MODEL)
    new_p, new_s = [], []
    for layer in range(depth):
        prec = 1 if layer == 0 else 0
        tm = _prompt_tile(bp * lp, prec)
        p = _prepare(raw, slice(layer, layer + 1), prec, 0, layer)
        _, ka, va, sb, sc, sd, qp, kp, vp = _in_proj(xp, p, tm, prec, lp)
        oa = _attn_prompt(qp, kp, vp, p, bp, lp, prec)
        ob, s_ret = _ret_prompt(sb, p, bp, lp, prec)
        oc, h_last = _lru_prompt(sc, p, bp, lp, prec)
        od, s_gla = _gla_prompt(sd, p, bp, lp, prec)
        hres = _merge(xp, oa, ob, oc, od, p, tm, prec)
        xp = _moe(hres, p, _prompt_tile(bp * lp, 0), 0)
        new_p.append((ka.reshape(bp, lp, HEADS, 64)[:, PADL:], va.reshape(bp, lp, HEADS, 64)[:, PADL:],
                      _diag_blocks(s_ret, 64, 64), h_last,
                      sc.reshape(bp, lp, W_C)[:, lp - (CONV_W - 1):, 0:BW],
                      _diag_blocks(s_gla, 64, 32).transpose(0, 1, 3, 2)))
        ps = _prepare(raw, slice(None), 2, 2, layer)
        qa, ka, va, sb, sc, sd = _in_proj(xs, ps, db, 2)
        oa_fm = _dec_attn(qa, ka, va, cache_k_a, cache_v_a, page_table, ps)
        oa, ob, oc, od, ret_new, h_new, gla_new = _sample_step(
            oa_fm, sb, sc, sd, state_ret[layer], state_gla[layer], state_lru_conv[layer], state_lru_h[layer],
            ps, past)
        hres = _merge(xs, oa, ob, oc, od, ps, db, 2)
        xs = _moe(hres, ps, db, 2)
        conv_new = jnp.concatenate([state_lru_conv[layer][:, 1:], sc[:, None, 0:BW]], axis=1)
        new_s.append((ka.reshape(db, 1, HEADS, 64), va.reshape(db, 1, HEADS, 64), ret_new, h_new, conv_new,
                      gla_new))
    stk = lambda states, i: jnp.stack([s[i] for s in states])
    y_prompt = xp.reshape(bp, lp, D_MODEL)[:, BLK:]
    return ((y_prompt, xs.reshape(db, 1, D_MODEL))
            + tuple(stk(new_p, i) for i in range(6)) + tuple(stk(new_s, i) for i in range(6)))
```

```python
import functools
import math

import numpy as np
import jax
import jax.numpy as jnp
from jax import lax
from jax.experimental import pallas as pl
from jax.experimental.pallas import tpu as pltpu

F32 = jnp.float32
BF16 = jnp.bfloat16
EPS = 1e-6
D_MODEL = 1024
N_META = 16
PAGE = 128
BLK = 128
PADL = BLK - N_META
HEADS = 4
BW = 256
DQA = 32
ROPE_BASE = 10000.0
CONV_W = 4
LRU_C = 8.0
GLA_RANK = 16
GLA_TAU = 16.0
GLA_CHUNK = 16
N_GROUPS = 4
EXP_PER_GROUP = 4
N_EXPERTS = 16
D_EXPERT = 256
W_A, W_B, W_C, W_D = 768, 1024, 512, 896
QK_PACK = 8 * 128
V_PACK = HEADS * 128
VMEM_LIMIT = 56 * 1024 * 1024
NEG = -1e30
PAGES_PER_STEP = 16
ATTN_KEY_STEP = 512
PAD_LANE = 96
LOG2E = math.log2(math.e)


def _cparams(sem):
    return pltpu.CompilerParams(dimension_semantics=sem, vmem_limit_bytes=VMEM_LIMIT)


def _layer_spec(arr, layer):
    nd = arr.ndim
    return pl.BlockSpec((1,) + arr.shape[1:], lambda *_: (layer,) + (0,) * (nd - 1),
                        pipeline_mode=pl.Buffered(1))


def _whole_spec(arr):
    nd = arr.ndim
    return pl.BlockSpec(arr.shape, lambda *_: (0,) * nd, pipeline_mode=pl.Buffered(1))


def _bf16_prefix(x):
    bits = lax.bitcast_convert_type(x, jnp.uint32)
    bits = (bits + jnp.uint32(0x7FFF) + ((bits >> 16) & jnp.uint32(1))) & jnp.uint32(0xFFFF0000)
    return lax.bitcast_convert_type(bits, F32)


def _split_bits(x):
    hi = _bf16_prefix(x)
    return hi.astype(BF16), (x - hi).astype(BF16)


def _split(x):
    hi = x.astype(BF16)
    return hi, (x - hi.astype(F32)).astype(BF16)


def _split3(x):
    hi = x.astype(BF16)
    r = x - hi.astype(F32)
    mid = r.astype(BF16)
    return hi, mid, (r - mid.astype(F32)).astype(BF16)


def _bdot(a, b, dims=None):
    if dims is None:
        return jnp.dot(a, b, preferred_element_type=F32)
    return lax.dot_general(a, b, (dims, ((), ())), preferred_element_type=F32)


def _branch_dtype(prec):
    return BF16 if prec == 0 else F32


def _operand(a, prec):
    if prec == 2:
        return a
    return _split(a) if prec == 1 else a.astype(BF16)


def _wdot(a, w_ref, idx, prec):
    get = lambda part: w_ref[(0, part) + idx]
    if prec == 2:
        return jnp.dot(a, get(0), precision=lax.Precision.HIGHEST, preferred_element_type=F32)
    if prec == 1:
        a_hi, a_lo = a if isinstance(a, tuple) else _split(a)
        return _bdot(a_hi, get(0)) + _bdot(a_lo, get(0)) + _bdot(a_hi, get(1))
    assert w_ref.shape[1] in (1, 3), "level 0 needs the plain bf16 cast: parts [cast] or [hi, lo, cast]"
    return _bdot(a.astype(BF16), get(w_ref.shape[1] - 1))


def _adot(a, b, prec, dims=None):
    if prec == 1:
        a_hi, a_lo = _split(a)
        b_hi, b_lo = _split(b)
        return _bdot(a_hi, b_hi, dims) + _bdot(a_lo, b_hi, dims) + _bdot(a_hi, b_lo, dims)
    return _bdot(a.astype(BF16), b.astype(BF16), dims)


NT = ((1,), (1,))
TN = ((0,), (0,))


def _seg_mean(x, m, prec=0):
    if prec == 2:
        return jnp.dot(x, m, precision=lax.Precision.HIGHEST, preferred_element_type=F32)
    hi, lo = _split(x)
    mb = m.astype(BF16)
    return _bdot(hi, mb) + _bdot(lo, mb)


def _rms(x, g):
    return x * lax.rsqrt(jnp.mean(x * x, axis=-1, keepdims=True) + EPS) * g


def _sigmoid(x):
    return 1.0 / (1.0 + jnp.exp(-x))


def _silu(x):
    return x * _sigmoid(x)


def _log_sigmoid(x):
    return jnp.minimum(x, 0.0) - jnp.log(1.0 + jnp.exp(-jnp.abs(x)))


def _softplus(x):
    return jnp.maximum(x, 0.0) + jnp.log(1.0 + jnp.exp(-jnp.abs(x)))


def _gelu_tanh(x):
    return 0.5 * x * (1.0 + jnp.tanh(math.sqrt(2.0 / math.pi) * (x + 0.044715 * (x * x * x))))


def _lane_mask(width, lo, hi):
    lane = lax.broadcasted_iota(jnp.int32, (1, width), 1)
    return (lane >= lo) & (lane < hi)


def _lam_scalar(lv, lam_init):
    s01 = jnp.sum(lv[0:1] * lv[1:2], axis=-1, keepdims=True)
    s23 = jnp.sum(lv[2:3] * lv[3:4], axis=-1, keepdims=True)
    return jnp.exp(s01) - jnp.exp(s23) + lam_init


def _lam_init(layer):
    return 0.8 - 0.6 * math.exp(-0.3 * layer)


def _in_proj_kernel(x_ref, g_ref, wa_ref, wb_ref, wc_ref, wd_ref, qn_ref, kn_ref, m32_ref, w2_ref, gb_ref,
                    pq_ref, pk_ref, pv_ref, ka_ref, va_ref, sb_ref, sc_ref, sd_ref, *q_out, prec, seq_len):
    xn = _rms(x_ref[...], g_ref[0])
    xs = _operand(xn, prec)
    a = _wdot(xs, wa_ref, (), prec)
    sb_ref[...] = _wdot(xs, wb_ref, (), prec)
    sc_ref[...] = _wdot(xs, wc_ref, (), prec)
    d = _wdot(xs, wd_ref, (), prec)
    q, k, v = a[:, 0:BW], a[:, BW:2 * BW], a[:, 2 * BW:3 * BW]
    m32 = m32_ref[...]
    qa = q * lax.rsqrt(_seg_mean(q * q, m32, prec) + EPS) * qn_ref[0] * (DQA ** -0.5)
    ka = k * lax.rsqrt(_seg_mean(k * k, m32, prec) + EPS) * kn_ref[0]
    ka_ref[...] = ka
    va_ref[...] = v
    sd_ref[:, 0:768] = d[:, 0:768]
    z = _wdot(d[:, 768:896], w2_ref, (), prec) + gb_ref[0]
    sd_ref[:, 768:896] = _log_sigmoid(z) * (1.0 / GLA_TAU)
    if len(q_out) == 1:
        q_out[0][...] = qa
    else:
        def place(src, p_ref):
            hi, lo = _split(src)
            return _bdot(hi, p_ref[0]) + _bdot(lo, p_ref[1])

        tm = x_ref.shape[0]
        row = (pl.program_id(0) * tm + lax.broadcasted_iota(jnp.int32, (tm, 1), 0)).astype(F32)
        in_seq = row - jnp.floor((row + 0.5) * (1.0 / seq_len)) * seq_len
        is_pad = in_seq < PADL
        flag = (lax.broadcasted_iota(jnp.int32, (1, QK_PACK), 1) % 128 == PAD_LANE).astype(F32)
        qp_ref, kp_ref, vp_ref = q_out
        qp_ref[...] = (place(qa * LOG2E, pq_ref) + jnp.where(is_pad, 0.0, 1.0) * flag).astype(BF16)
        kp_ref[...] = (place(ka, pk_ref) + jnp.where(is_pad, NEG, 0.0) * flag).astype(BF16)
        vp_ref[...] = place(v, pv_ref).astype(BF16)


def _pack_matrices():
    pq = np.zeros((2, BW, QK_PACK), np.float32)
    pk = np.zeros((2, BW, QK_PACK), np.float32)
    pv = np.zeros((2, BW, V_PACK), np.float32)
    for g in range(8):
        for d in range(DQA):
            r, c = DQA * g + d, 128 * g + d
            pq[0, r, c] = pq[0, r, c + 64] = 1.0
            pq[1, r, c + 32] = 1.0
            pk[0, r, c] = pk[0, r, c + 32] = 1.0
            pk[1, r, c + 64] = 1.0
    for h in range(HEADS):
        for d in range(64):
            pv[0, 64 * h + d, 128 * h + d] = 1.0
            pv[1, 64 * h + d, 128 * h + 64 + d] = 1.0
    return [jnp.asarray(a, BF16) for a in (pq, pk, pv)]


def _in_proj(x, p, tm, prec, seq_len=None):
    n = x.shape[0]
    packed = seq_len is not None
    layer = p["idx"]
    outs = [jax.ShapeDtypeStruct((n, BW), F32)] * 2 + [jax.ShapeDtypeStruct((n, W_B), F32),
            jax.ShapeDtypeStruct((n, W_C), F32), jax.ShapeDtypeStruct((n, W_D), F32)]
    if packed:
        outs += [jax.ShapeDtypeStruct((n, QK_PACK), BF16)] * 2 + [jax.ShapeDtypeStruct((n, V_PACK), BF16)]
    else:
        outs += [jax.ShapeDtypeStruct((n, BW), F32)]
    row = lambda w: pl.BlockSpec((tm, w), lambda i: (i, 0))
    params = [p["norm1"], p["wa"], p["wb"], p["wc"], p["wd"], p["qn_t"], p["kn_t"]]
    tail = [p["gla_w2p"], p["gla_b"]]
    consts = _pack_matrices()
    return pl.pallas_call(
        functools.partial(_in_proj_kernel, prec=prec, seq_len=seq_len),
        grid=(n // tm,),
        in_specs=([row(D_MODEL)] + [_layer_spec(a, layer) for a in params] + [_whole_spec(p["m32"])]
                  + [_layer_spec(a, layer) for a in tail] + [_whole_spec(a) for a in consts]),
        out_specs=[row(o.shape[1]) for o in outs],
        out_shape=outs,
        compiler_params=_cparams(("parallel",)),
        name="in_proj_p%d" % prec,
    )(x, *params, p["m32"], *tail, *consts)


def _attn_body(q_ref, k_ref, v_ref, lam, an_ref, o_ref, s_buf, w_buf, i, nk, *, lam_init, tq, prec):
    n_free = max(nk - ATTN_KEY_STEP, 0)
    pos_q = i * tq + lax.broadcasted_iota(jnp.int32, (tq, 1), 0)
    causal = n_free + lax.broadcasted_iota(jnp.int32, (1, nk - n_free), 1) <= pos_q
    lane = lax.broadcasted_iota(jnp.int32, (1, 128), 1)
    for g in range(2 * HEADS):
        s_buf[g, :, 0:nk] = _bdot(q_ref[:, 128 * g:128 * (g + 1)], k_ref[0:nk, 128 * g:128 * (g + 1)], NT)
    outs = []
    for h in range(HEADS):
        scale = []
        for m in range(2):
            g = 2 * h + m
            tail = jnp.where(causal, s_buf[g, :, n_free:nk], NEG)
            mx = jnp.max(tail, axis=-1, keepdims=True)
            if n_free:
                mx = jnp.maximum(mx, jnp.max(s_buf[g, :, 0:n_free], axis=-1, keepdims=True))
                head_p = jnp.exp2(s_buf[g, :, 0:n_free] - mx)
                s_buf[g, :, 0:n_free] = head_p
            tail_p = jnp.exp2(tail - mx)
            s_buf[g, :, n_free:nk] = tail_p
            total = jnp.sum(tail_p, axis=-1, keepdims=True)
            if n_free:
                total = total + jnp.sum(head_p, axis=-1, keepdims=True)
            scale.append(1.0 / total)
        w = s_buf[2 * h, :, 0:nk] * scale[0] - s_buf[2 * h + 1, :, 0:nk] * (lam * scale[1])
        vt = v_ref[0:nk, 128 * h:128 * (h + 1)]
        if prec == 1:
            w_hi, w_lo = _split(w)
            w_buf[h, 0:tq, 0:nk] = w_hi
            w_buf[h, tq:2 * tq, 0:nk] = w_lo
            o2 = _bdot(w_buf[h, :, 0:nk], vt)
            o = o2[0:tq] + o2[tq:2 * tq]
        else:
            w_buf[h, 0:tq, 0:nk] = w.astype(BF16)
            o = _bdot(w_buf[h, 0:tq, 0:nk], vt)
        o = o + pltpu.roll(o, 64, 1)
        outs.append(o * lax.rsqrt(jnp.mean(o * o, axis=-1, keepdims=True) + EPS))
    for t in range(2):
        pair = jnp.where(lane < 64, outs[2 * t], outs[2 * t + 1])
        o_ref[:, 128 * t:128 * (t + 1)] = (pair * an_ref[0][:, 128 * t:128 * (t + 1)]
                                           * (1.0 - lam_init)).astype(o_ref.dtype)


def _attn_kernel(q_ref, k_ref, v_ref, lam_ref, an_ref, o_ref, s_buf, w_buf, *, lam_init, tq, key_ranges, prec):
    i = pl.program_id(1)
    lam = _lam_scalar(lam_ref[0], lam_init)
    need = (i + 1) * tq
    lo = 0
    for nk in key_ranges:
        @pl.when((need > lo) & (need <= nk))
        def _(nk=nk):
            _attn_body(q_ref, k_ref, v_ref, lam, an_ref, o_ref, s_buf, w_buf, i, nk, lam_init=lam_init, tq=tq,
                       prec=prec)
        lo = nk


def _attn_prompt(qp, kp, vp, p, b, lp, prec):
    nq = lp // BLK
    layer = p["idx"]
    key_ranges = tuple(sorted({min(lp, ATTN_KEY_STEP * j) for j in range(1, -(-lp // ATTN_KEY_STEP) + 1)}))
    return pl.pallas_call(
        functools.partial(_attn_kernel, lam_init=_lam_init(p["layer"]), tq=BLK, key_ranges=key_ranges,
                          prec=prec),
        grid=(b, nq),
        in_specs=[pl.BlockSpec((BLK, QK_PACK), lambda bi, i: (bi * nq + i, 0)),
                  pl.BlockSpec((lp, QK_PACK), lambda bi, i: (bi, 0)),
                  pl.BlockSpec((lp, V_PACK), lambda bi, i: (bi, 0)),
                  _layer_spec(p["lam"], layer), _layer_spec(p["an_t"], layer)],
        out_specs=pl.BlockSpec((BLK, BW), lambda bi, i: (bi * nq + i, 0)),
        out_shape=jax.ShapeDtypeStruct((b * lp, BW), _branch_dtype(prec)),
        scratch_shapes=[pltpu.VMEM((2 * HEADS, BLK, lp), F32), pltpu.VMEM((HEADS, 2 * BLK, lp), BF16)],
        compiler_params=_cparams(("parallel", "parallel")),
        name="attn_prompt_p%d" % prec,
    )(qp, kp, vp, p["lam"], p["an_t"])


def _swap_halves(x):
    w = x.shape[-1]
    lane = lax.broadcasted_iota(jnp.int32, (1, w), 1)
    return jnp.where((lane % 64) < 32, pltpu.roll(x, w - 32, 1), pltpu.roll(x, 32, 1))


def _ret_kernel(sb_ref, cos_ref, sin_ref, dmat_ref, dq_ref, dk_ref, dstate_ref, rn_ref, m64_ref,
                o_ref, s_out_ref, s_ref, *, prec):
    c = pl.program_id(1)

    @pl.when(c == 0)
    def _():
        s_ref[...] = jnp.zeros_like(s_ref)

    cos, sin = cos_ref[...], sin_ref[...]
    q, k, v = sb_ref[:, 0:BW], sb_ref[:, BW:2 * BW], sb_ref[:, 2 * BW:3 * BW]
    qr = q * cos + _swap_halves(q) * sin
    kr = (k * cos + _swap_halves(k) * sin) * 0.125
    o = _adot(qr, s_ref[...], prec) * dq_ref[...]
    upd = _adot(kr * dk_ref[...], v, prec, TN)
    masks = [_lane_mask(BW, 64 * h, 64 * (h + 1)) for h in range(HEADS)]
    scores = [_adot(jnp.where(hm, qr, 0.0), kr, prec, NT) for hm in masks]
    for h, hm in enumerate(masks):
        o = o + jnp.where(hm, _adot(scores[h] * dmat_ref[h], v, prec), 0.0)
    row_head = lax.broadcasted_iota(jnp.int32, (BW, BW), 0) // 64
    col_head = lax.broadcasted_iota(jnp.int32, (BW, BW), 1) // 64
    s_new = s_ref[...] * dstate_ref[...] + jnp.where(row_head == col_head, upd, 0.0)
    s_ref[...] = s_new
    s_out_ref[0] = s_new
    m64 = m64_ref[...]
    mu = _seg_mean(o, m64)
    var = _seg_mean((o - mu) * (o - mu), m64)
    g = sb_ref[:, 3 * BW:4 * BW]
    o_ref[...] = ((o - mu) * lax.rsqrt(var + EPS) * rn_ref[0] * _silu(g)).astype(o_ref.dtype)


def _ret_gammas():
    return 1.0 - 2.0 ** (-5.0 - np.arange(HEADS, dtype=np.float64))


def _ret_tables():
    lg = np.log(_ret_gammas())
    t = np.arange(BLK, dtype=np.float64)
    rel = t[:, None] - t[None, :]
    dmat = np.where(rel >= 0, np.exp(np.maximum(rel, 0.0)[None] * lg[:, None, None]), 0.0)
    lane_lg = np.repeat(lg, 64)[None, :]
    dq = np.exp((t[:, None] + 1.0) * lane_lg)
    dk = np.exp((BLK - 1.0 - t)[:, None] * lane_lg)
    dstate = np.exp(BLK * lane_lg)
    return [jnp.asarray(a, F32) for a in (dmat, dq, dk, dstate)]


def _rope_cos_sin(pos):
    inv = ROPE_BASE ** (-jnp.linspace(0.0, 1.0, 32))
    ang = pos.astype(F32)[:, None] * inv[None]
    return jnp.cos(ang), jnp.sin(ang)


def _ret_prompt(sb, p, b, lp, prec):
    nc = lp // BLK
    cos, sin = _rope_cos_sin(jnp.arange(lp) - PADL)
    cos_t = jnp.tile(jnp.concatenate([cos, cos], -1), (1, HEADS))
    sin_t = jnp.tile(jnp.concatenate([-sin, sin], -1), (1, HEADS))
    tables = _ret_tables()
    return pl.pallas_call(
        functools.partial(_ret_kernel, prec=prec),
        grid=(b, nc),
        in_specs=[pl.BlockSpec((BLK, W_B), lambda bi, c: (bi * nc + c, 0)),
                  pl.BlockSpec((BLK, BW), lambda bi, c: (c, 0)),
                  pl.BlockSpec((BLK, BW), lambda bi, c: (c, 0))]
                 + [_whole_spec(a) for a in tables] + [_layer_spec(p["rn_t"], p["idx"]), _whole_spec(p["m64"])],
        out_specs=[pl.BlockSpec((BLK, BW), lambda bi, c: (bi * nc + c, 0)),
                   pl.BlockSpec((1, BW, BW), lambda bi, c: (bi, 0, 0))],
        out_shape=[jax.ShapeDtypeStruct((b * lp, BW), _branch_dtype(prec)), jax.ShapeDtypeStruct((b, BW, BW), F32)],
        scratch_shapes=[pltpu.VMEM((BW, BW), F32)],
        compiler_params=_cparams(("parallel", "arbitrary")),
        name="ret_prompt_p%d" % prec,
    )(sb, cos_t, sin_t, *tables, p["rn_t"], p["m64"])


def _lru_gates(y, wa_ref, wx_ref, ba, bx, sp, prec):
    ys = _operand(y, prec)
    r = _sigmoid(_wdot(ys, wa_ref, (), prec) + ba)
    ig = _sigmoid(_wdot(ys, wx_ref, (), prec) + bx)
    log_a = -LRU_C * r * sp
    a = jnp.exp(log_a)
    u = jnp.sqrt(1.0 - jnp.exp(2.0 * log_a)) * (ig * y)
    return a, u


def _lru_kernel(sc_ref, cw_ref, cb_ref, wa_ref, wx_ref, ba_ref, bx_ref, lam_ref,
                o_ref, hl_ref, cbuf, a_s, u_s, h_s, hcar, *, nb, prec):
    t = pl.program_id(0)

    @pl.when(t == 0)
    def _():
        cbuf[...] = jnp.zeros_like(cbuf)
        hcar[...] = jnp.zeros_like(hcar)

    sp = _softplus(-lam_ref[0])
    row = t * BLK + lax.broadcasted_iota(jnp.int32, (BLK, 1), 0)
    for b in range(nb):
        x = sc_ref[b, :, 0:BW]
        cbuf[b, 8:8 + BLK, :] = x
        y = cb_ref[0] + cbuf[b, 5:5 + BLK, :] * cw_ref[0, 0:1, :]
        for i in range(1, CONV_W):
            y = y + cbuf[b, 5 + i:5 + i + BLK, :] * cw_ref[0, i:i + 1, :]
        cbuf[b, 0:8, :] = x[BLK - 8:BLK, :]
        a, u = _lru_gates(y, wa_ref, wx_ref, ba_ref[0], bx_ref[0], sp, prec)
        a_s[b] = a
        u_s[b] = jnp.where(row >= PADL, u, 0.0)

    def step(i, hs):
        out = []
        for b in range(nb):
            h = a_s[b, pl.ds(i, 1), :] * hs[b] + u_s[b, pl.ds(i, 1), :]
            h_s[b, pl.ds(i, 1), :] = h
            out.append(h)
        return tuple(out)

    hs = lax.fori_loop(0, BLK, step, tuple(hcar[b:b + 1, :] for b in range(nb)))
    for b in range(nb):
        hcar[b:b + 1, :] = hs[b]
        o_ref[b] = (h_s[b] * _gelu_tanh(sc_ref[b, :, BW:2 * BW])).astype(o_ref.dtype)
    hl_ref[...] = hcar[...]


def _lru_prompt(sc, p, b, lp, prec):
    nt = lp // BLK
    sc3 = sc.reshape(b, lp, W_C)
    params = [p["conv_w"], p["conv_b"], p["lru_wa_bd"], p["lru_wx_bd"], p["lru_ba"], p["lru_bx"], p["lru_lam"]]
    o, hl = pl.pallas_call(
        functools.partial(_lru_kernel, nb=b, prec=prec),
        grid=(nt,),
        in_specs=[pl.BlockSpec((b, BLK, W_C), lambda t: (0, t, 0))] + [_layer_spec(a, p["idx"]) for a in params],
        out_specs=[pl.BlockSpec((b, BLK, BW), lambda t: (0, t, 0)), pl.BlockSpec((b, BW), lambda t: (0, 0))],
        out_shape=[jax.ShapeDtypeStruct((b, lp, BW), _branch_dtype(prec)), jax.ShapeDtypeStruct((b, BW), F32)],
        scratch_shapes=[pltpu.VMEM((b, BLK + 8, BW), F32), pltpu.VMEM((b, BLK, BW), F32),
                        pltpu.VMEM((b, BLK, BW), F32), pltpu.VMEM((b, BLK, BW), F32), pltpu.VMEM((b, BW), F32)],
        compiler_params=_cparams(("arbitrary",)),
        name="lru_prompt_p%d" % prec,
    )(sc3, *params)
    return o.reshape(b * lp, BW), hl


def _gla_kernel(sd_ref, tri_ref, ind_ref, gn_ref, m64_ref, o_ref, st_out_ref, st_ref, o_s, upd_s, *, prec):
    c = pl.program_id(1)

    @pl.when(c == 0)
    def _():
        st_ref[...] = jnp.zeros_like(st_ref)

    q = sd_ref[:, 0:128] * (DQA ** -0.5)
    k = sd_ref[:, 128:256]
    v = sd_ref[:, 256:512]
    la = sd_ref[:, 768:896]
    tri = tri_ref[...]
    bc = sum(_bdot(tri, part) for part in _split3(la))
    ind = ind_ref[...]
    t_idx = lax.broadcasted_iota(jnp.int32, (GLA_CHUNK, 1), 0)
    row_head = lax.broadcasted_iota(jnp.int32, (BW, 128), 0) // 64
    col_head = lax.broadcasted_iota(jnp.int32, (BW, 128), 1) // 32
    bd = row_head == col_head
    n_sub = BLK // GLA_CHUNK

    def local_cumsum(s):
        r0 = s * GLA_CHUNK
        bcs = bc[r0:r0 + GLA_CHUNK]
        return bcs - bc[r0 - 1:r0] if s > 0 else bcs

    for s in range(n_sub):
        r0 = s * GLA_CHUNK
        bcs = local_cumsum(s)
        qs, ks, vs = q[r0:r0 + GLA_CHUNK], k[r0:r0 + GLA_CHUNK], v[r0:r0 + GLA_CHUNK]
        zs = []
        for j in range(GLA_CHUNK):
            dec = jnp.where(t_idx >= j, jnp.exp(jnp.minimum(bcs - bcs[j:j + 1], 0.0)), 0.0)
            zs.append(qs * dec * ks[j:j + 1])
        z = jnp.concatenate(zs, axis=0)
        if prec == 1:
            z_hi, z_lo = _split(z)
            aa = _bdot(z_hi, ind) + _bdot(z_lo, ind)
        else:
            aa = _bdot(z.astype(BF16), ind)
        o = aa[0:GLA_CHUNK] * vs[0:1]
        for j in range(1, GLA_CHUNK):
            o = o + aa[j * GLA_CHUNK:(j + 1) * GLA_CHUNK] * vs[j:j + 1]
        o_s[r0:r0 + GLA_CHUNK, :] = o
        last = bcs[GLA_CHUNK - 1:GLA_CHUNK]
        upd_s[s] = jnp.where(bd, _adot(vs, ks * jnp.exp(last - bcs), prec, TN), 0.0)
    st = st_ref[...]
    for s in range(n_sub):
        r0 = s * GLA_CHUNK
        bcs = local_cumsum(s)
        o_s[r0:r0 + GLA_CHUNK, :] += _adot(q[r0:r0 + GLA_CHUNK] * jnp.exp(bcs), st, prec, NT)
        st = st * jnp.exp(bcs[GLA_CHUNK - 1:GLA_CHUNK]) + upd_s[s]
    st_ref[...] = st
    st_out_ref[0] = st
    o = o_s[...]
    ms = _seg_mean(o * o, m64_ref[...])
    o_ref[...] = (o * lax.rsqrt(ms + EPS) * gn_ref[0] * _silu(sd_ref[:, 512:768])).astype(o_ref.dtype)


def _gla_prompt(sd, p, b, lp, prec):
    nc = lp // BLK
    t = np.arange(BLK)
    tri = jnp.asarray(t[:, None] >= t[None, :], BF16)
    ind = jnp.asarray((np.arange(128)[:, None] // 32) == (np.arange(BW)[None, :] // 64), BF16)
    return pl.pallas_call(
        functools.partial(_gla_kernel, prec=prec),
        grid=(b, nc),
        in_specs=[pl.BlockSpec((BLK, W_D), lambda bi, c: (bi * nc + c, 0)),
                  _whole_spec(tri), _whole_spec(ind), _layer_spec(p["gn_t"], p["idx"]), _whole_spec(p["m64"])],
        out_specs=[pl.BlockSpec((BLK, BW), lambda bi, c: (bi * nc + c, 0)),
                   pl.BlockSpec((1, BW, 128), lambda bi, c: (bi, 0, 0))],
        out_shape=[jax.ShapeDtypeStruct((b * lp, BW), _branch_dtype(prec)), jax.ShapeDtypeStruct((b, BW, 128), F32)],
        scratch_shapes=[pltpu.VMEM((BW, 128), F32), pltpu.VMEM((BLK, BW), F32),
                        pltpu.VMEM((BLK // GLA_CHUNK, BW, 128), F32)],
        compiler_params=_cparams(("parallel", "arbitrary")),
        name="gla_prompt_p%d" % prec,
    )(sd, tri, ind, p["gn_t"], p["m64"])


def _merge_kernel(fine_ref, x_ref, g_ref, oa_ref, ob_ref, oc_ref, od_ref, wm_ref, bm_ref, wbr_ref, wo_ref, y_ref,
                  *, prec):
    def body(level):
        x = x_ref[...]
        xs = _operand(_rms(x, g_ref[0]), level)
        acc = None
        for n, br in enumerate((oa_ref, ob_ref, oc_ref, od_ref)):
            cols = slice(n * D_MODEL, (n + 1) * D_MODEL)
            gate = _sigmoid(_wdot(xs, wm_ref, (slice(None), cols), level) + bm_ref[0, :, cols])
            term = gate * _wdot(br[...], wbr_ref, (n,), level)
            acc = term if acc is None else acc + term
        y_ref[...] = x + _wdot(acc, wo_ref, (), level)

    if prec == 1:
        fine = fine_ref[pl.program_id(0)]
        pl.when(fine != 0)(lambda: body(1))
        pl.when(fine == 0)(lambda: body(0))
    else:
        body(prec)


def _merge(x, oa, ob, oc, od, p, tm, prec, fine_rows=None):
    n = x.shape[0]
    layer = p["idx"]
    fine = np.ones((n // tm,), np.int32)
    if fine_rows is not None:
        fine = np.zeros((n // tm,), np.int32)
        fine[np.asarray(fine_rows) // tm] = 1
    row = lambda w: pl.BlockSpec((tm, w), lambda i, f: (i, 0))
    params = [p["w_merge"], p["b_merge"], p["w_branch"], p["w_out"]]
    grid_spec = pltpu.PrefetchScalarGridSpec(
        num_scalar_prefetch=1,
        grid=(n // tm,),
        in_specs=[row(D_MODEL), _layer_spec(p["norm1"], layer), row(BW), row(BW), row(BW), row(BW)]
                 + [_layer_spec(a, layer) for a in params],
        out_specs=row(D_MODEL),
    )
    return pl.pallas_call(
        functools.partial(_merge_kernel, prec=prec),
        grid_spec=grid_spec,
        out_shape=jax.ShapeDtypeStruct((n, D_MODEL), F32),
        compiler_params=_cparams(("parallel",)),
        name="merge_p%d" % prec,
    )(jnp.asarray(fine), x, p["norm1"], oa, ob, oc, od, *params)


def _first_argmax(vals, valid, lane):
    v = jnp.where(valid, vals, NEG)
    top = jnp.max(v, axis=-1, keepdims=True)
    idx = jnp.min(jnp.where(valid & (v == top), lane, 1 << 20), axis=-1, keepdims=True)
    return top, idx


def _router_gate(lg, le):
    lane = lax.broadcasted_iota(jnp.int32, lg.shape, 1)
    gvalid = lane < N_GROUPS
    lgm = jnp.where(gvalid, lg, NEG)
    pe = jnp.exp(lgm - jnp.max(lgm, axis=-1, keepdims=True))
    pg = pe / jnp.sum(pe, axis=-1, keepdims=True)
    pg_top, g_idx = _first_argmax(pg, gvalid, lane)
    sel = (lane // EXP_PER_GROUP == g_idx) & (lane < N_EXPERTS)
    v1, i1 = _first_argmax(le, sel, lane)
    v2, i2 = _first_argmax(le, sel & (lane != i1), lane)
    e2 = jnp.exp(v2 - v1)
    den = 1.0 + e2
    return jnp.where(lane == i1, pg_top / den, jnp.where(lane == i2, pg_top * e2 / den, 0.0))


def _moe_kernel(h_ref, g_ref, rw_ref, rb_ref, sel_ref, w1_ref, w3_ref, w2_ref, y_ref, t_s, gate_s, *, prec):
    grp = pl.program_id(1)

    @pl.when(grp == 0)
    def _():
        x = h_ref[...]
        t = _rms(x, g_ref[0])
        t_s[...] = t.astype(t_s.dtype)
        logits = _wdot(t, rw_ref, (), max(prec, 1)) + rb_ref[0]
        gate_s[...] = _router_gate(logits[:, 0:128], logits[:, 128:256])
        y_ref[...] = x

    t = t_s[...]
    gcols = _seg_mean(gate_s[...], sel_ref[grp], prec)
    hs = []
    for j in range(EXP_PER_GROUP):
        gj = gcols[:, 128 * j:128 * (j + 1)]
        hj = _silu(_wdot(t, w1_ref, (j,), prec)) * _wdot(t, w3_ref, (j,), prec)
        hs.append(hj * jnp.concatenate([gj, gj], axis=1))
    acc = _wdot(hs[0], w2_ref, (0,), prec)
    for j in range(1, EXP_PER_GROUP):
        acc = acc + _wdot(hs[j], w2_ref, (j,), prec)
    y_ref[...] += acc


def _moe(h, p, tm, prec):
    n = h.shape[0]
    layer = p["idx"]
    wspec = lambda a: pl.BlockSpec((1, a.shape[1], EXP_PER_GROUP) + a.shape[3:],
                                   lambda i, g: (layer, 0, g, 0, 0))
    sel = np.zeros((N_GROUPS, 128, EXP_PER_GROUP * 128), np.float32)
    for g in range(N_GROUPS):
        for j in range(EXP_PER_GROUP):
            sel[g, EXP_PER_GROUP * g + j, 128 * j:128 * (j + 1)] = 1.0
    sel = jnp.asarray(sel)
    return pl.pallas_call(
        functools.partial(_moe_kernel, prec=prec),
        grid=(n // tm, N_GROUPS),
        in_specs=[pl.BlockSpec((tm, D_MODEL), lambda i, g: (i, 0)), _layer_spec(p["norm2"], layer),
                  _layer_spec(p["rw"], layer), _layer_spec(p["rb"], layer), _whole_spec(sel),
                  wspec(p["ex_w1"]), wspec(p["ex_w3"]), wspec(p["ex_w2"])],
        out_specs=pl.BlockSpec((tm, D_MODEL), lambda i, g: (i, 0)),
        out_shape=jax.ShapeDtypeStruct((n, D_MODEL), F32),
        scratch_shapes=[pltpu.VMEM((tm, D_MODEL), F32 if prec == 2 else BF16), pltpu.VMEM((tm, 128), F32)],
        compiler_params=_cparams(("parallel", "arbitrary")),
        name="moe_p%d" % prec,
    )(h, p["norm2"], p["rw"], p["rb"], sel, p["ex_w1"], p["ex_w3"], p["ex_w2"])


def _sublane_total(x):
    for shift in (4, 2, 1):
        x = x + pltpu.roll(x, shift, 0)
    return x


def _lane_all(x, op):
    return jnp.broadcast_to(op(x, axis=-1, keepdims=True), x.shape)


def _dec_attn_kernel(pt_ref, q_ref, kn_ref, vn_ref, lam_ref, *rest, pp, lam_init):
    del pt_ref
    kp, vp = rest[:pp], rest[pp:2 * pp]
    o_ref, s_s, m_s, l_s, acc_s = rest[2 * pp:]
    step = pl.program_id(1)

    @pl.when(step == 0)
    def _():
        m_s[...] = jnp.full_like(m_s, NEG)
        l_s[...] = jnp.zeros_like(l_s)
        acc_s[...] = jnp.zeros_like(acc_s)

    def map_scores(t, m):
        part = t[4 * m] + t[4 * m + 1] + t[4 * m + 2] + t[4 * m + 3]
        return _sublane_total(part)

    tops = [None] * (2 * HEADS)
    for j in range(pp):
        for h in range(HEADS):
            t = kp[j][0, 0, h] * q_ref[0, h]
            for m in range(2):
                g = 2 * h + m
                sc = map_scores(t, m)
                s_s[g, j] = sc
                tops[g] = sc if tops[g] is None else jnp.maximum(tops[g], sc)
    for h in range(HEADS):
        m_new, accs, sums = [], [], []
        for m in range(2):
            g = 2 * h + m
            m_old = m_s[g]
            m_new.append(jnp.maximum(m_old, _lane_all(tops[g], jnp.max)))
            alpha = jnp.exp(m_old - m_new[m])
            accs.append(acc_s[g] * alpha)
            sums.append(None)
            l_s[g] = l_s[g] * alpha
            m_s[g] = m_new[m]
        for j in range(pp):
            vt = vp[j][0, 0, h]
            for m in range(2):
                pr = jnp.exp(s_s[2 * h + m, j] - m_new[m])
                accs[m] = accs[m] + vt * pr
                sums[m] = pr if sums[m] is None else sums[m] + pr
        for m in range(2):
            g = 2 * h + m
            acc_s[g] = accs[m]
            l_s[g] = l_s[g] + _lane_all(sums[m], jnp.sum)

    @pl.when(step == pl.num_programs(1) - 1)
    def _():
        lam = _lam_scalar(lam_ref[0], lam_init)
        for h in range(HEADS):
            t = q_ref[0, h] * kn_ref[0, h]
            maps = []
            for m in range(2):
                g = 2 * h + m
                sc = map_scores(t, m)
                m_o = m_s[g][:, 0:1]
                m_n = jnp.maximum(m_o, sc)
                al = jnp.exp(m_o - m_n)
                pn = jnp.exp(sc - m_n)
                num = al * jnp.sum(acc_s[g], axis=-1, keepdims=True) + pn * vn_ref[0, h]
                maps.append(num / (al * l_s[g][:, 0:1] + pn))
            o_ref[0, h] = maps[0] - lam * maps[1]


def _dec_attn(qa, ka, va, cache_k, cache_v, page_table, p):
    db = qa.shape[0]
    layer = p["layer"]
    n_pages = page_table.shape[1]
    n_phys = cache_k.shape[1]
    pp = math.gcd(PAGES_PER_STEP, n_pages)
    tiles = lambda c: c.transpose(0, 1, 3, 4, 2).reshape(c.shape[0], n_phys, HEADS, 8, 8, PAGE)
    ck, cv = tiles(cache_k), tiles(cache_v)
    page = lambda j: pl.BlockSpec((1, 1, HEADS, 8, 8, PAGE),
                                  lambda b, s, pt: (layer, pt[b, s * pp + j], 0, 0, 0, 0))
    col = pl.BlockSpec((1, HEADS, 8, 8, 1), lambda b, s, pt: (b, 0, 0, 0, 0))
    lam = p["lam"]
    grid_spec = pltpu.PrefetchScalarGridSpec(
        num_scalar_prefetch=1,
        grid=(db, n_pages // pp),
        in_specs=[col, col, col, pl.BlockSpec((1,) + lam.shape[1:], lambda b, s, pt: (p["idx"], 0, 0))]
                 + [page(j) for j in range(pp)] + [page(j) for j in range(pp)],
        out_specs=col,
        scratch_shapes=[pltpu.VMEM((2 * HEADS, pp, 8, PAGE), F32), pltpu.VMEM((2 * HEADS, 8, PAGE), F32),
                        pltpu.VMEM((2 * HEADS, 8, PAGE), F32), pltpu.VMEM((2 * HEADS, 8, 8, PAGE), F32)],
    )
    cols = lambda x: x.reshape(db, HEADS, 8, 8, 1)
    out = pl.pallas_call(
        functools.partial(_dec_attn_kernel, pp=pp, lam_init=_lam_init(layer)),
        grid_spec=grid_spec,
        out_shape=jax.ShapeDtypeStruct((db, HEADS, 8, 8, 1), F32),
        compiler_params=_cparams(("parallel", "arbitrary")),
        name="attn_sample",
    )(page_table, cols(qa), cols(ka), cols(va), lam, *([ck] * pp), *([cv] * pp))
    return out.reshape(db * HEADS, 64).T


def _step_kernel(oa_ref, an_ref,
                 bq1_ref, bq2_ref, bk1_ref, bk2_ref, bv_ref, bg_ref, s0_ref, cos_ref, sin_ref, gam_ref, rn_ref,
                 dq_ref, dk_ref, dv_ref, dr_ref, la_ref, g0_ref, gn_ref,
                 cx_ref, cg_ref, buf_ref, h0_ref, cw_ref, cb_ref, wa_ref, wx_ref, ba_ref, bx_ref, lam_ref,
                 oa_o, ob_o, od_o, oc_o, s_o, g_o, h_o, *, lam_init):
    x = oa_ref[...]
    oa_o[...] = x * lax.rsqrt(jnp.mean(x * x, axis=0, keepdims=True) + EPS) * an_ref[0] * (1.0 - lam_init)

    cos, sin, gam = cos_ref[...], sin_ref[...], gam_ref[...]
    rot = lambda x1, x2: jnp.concatenate([x1 * cos - x2 * sin, x1 * sin + x2 * cos], axis=0)
    qb = rot(bq1_ref[...], bq2_ref[...])
    kb = rot(bk1_ref[...], bk2_ref[...]) * 0.125
    v = bv_ref[...]
    o = jnp.sum(qb * kb, axis=0, keepdims=True) * v
    st = jnp.zeros_like(v)
    for k in range(64):
        s0k = s0_ref[k]
        st = st + s0k * qb[k:k + 1]
        s_o[k] = gam * s0k + kb[k:k + 1] * v
    o = o + gam * st
    mu = jnp.mean(o, axis=0, keepdims=True)
    var = jnp.mean((o - mu) * (o - mu), axis=0, keepdims=True)
    ob_o[...] = (o - mu) * lax.rsqrt(var + EPS) * rn_ref[0] * _silu(bg_ref[...])

    qd = dq_ref[...] * (DQA ** -0.5)
    kd = dk_ref[...]
    vd = dv_ref[...]
    dec = jnp.exp(la_ref[...])
    o = jnp.sum(qd * kd, axis=0, keepdims=True) * vd
    qe = qd * dec
    for k in range(32):
        g0k = g0_ref[k]
        o = o + g0k * qe[k:k + 1]
        g_o[k] = dec[k:k + 1] * g0k + kd[k:k + 1] * vd
    od_o[...] = o * lax.rsqrt(jnp.mean(o * o, axis=0, keepdims=True) + EPS) * gn_ref[0] * _silu(dr_ref[...])

    cx = cx_ref[...]
    y = cb_ref[0] + cx * cw_ref[0, CONV_W - 1:CONV_W, :]
    for i in range(CONV_W - 1):
        y = y + buf_ref[i] * cw_ref[0, i:i + 1, :]
    a, u = _lru_gates(y, wa_ref, wx_ref, ba_ref[0], bx_ref[0], _softplus(-lam_ref[0]), 2)
    h = a * h0_ref[...] + u
    h_o[...] = h
    oc_o[...] = h * _gelu_tanh(cg_ref[...])


def _to_fm(x, db, width):
    return x.reshape(db, HEADS, width).transpose(2, 0, 1).reshape(width, db * HEADS)


def _from_fm(x, db, width):
    return x.reshape(width, db, HEADS).transpose(1, 2, 0).reshape(db, HEADS * width)


def _halves_fm(x, db):
    x4 = x.reshape(db, HEADS, 2, 32).transpose(2, 3, 0, 1).reshape(2, 32, db * HEADS)
    return x4[0], x4[1]


def _sample_step(oa_fm, sb, sc, sd, ret0, gla0, conv_buf, h0, p, pos):
    db = sb.shape[0]
    n = db * HEADS
    layer = p["idx"]
    cos, sin = _rope_cos_sin(jnp.asarray([pos]))
    gam = jnp.tile(jnp.asarray(_ret_gammas(), F32), db)[None, :]
    bq1, bq2 = _halves_fm(sb[:, 0:BW], db)
    bk1, bk2 = _halves_fm(sb[:, BW:2 * BW], db)
    args = [
        oa_fm, p["an_c"],
        bq1, bq2, bk1, bk2, _to_fm(sb[:, 2 * BW:3 * BW], db, 64), _to_fm(sb[:, 3 * BW:4 * BW], db, 64),
        ret0.transpose(2, 3, 0, 1).reshape(64, 64, n), cos.reshape(32, 1), sin.reshape(32, 1), gam, p["rn_c"],
        _to_fm(sd[:, 0:128], db, 32), _to_fm(sd[:, 128:256], db, 32), _to_fm(sd[:, 256:512], db, 64),
        _to_fm(sd[:, 512:768], db, 64), _to_fm(sd[:, 768:896], db, 32),
        gla0.transpose(2, 3, 0, 1).reshape(32, 64, n), p["gn_c"],
        sc[:, 0:BW], sc[:, BW:2 * BW], conv_buf.transpose(1, 0, 2), h0,
        p["conv_w"], p["conv_b"], p["lru_wa_bd"], p["lru_wx_bd"], p["lru_ba"], p["lru_bx"], p["lru_lam"],
    ]
    layered = {1, 12, 19, 24, 25, 26, 27, 28, 29, 30}
    in_specs = [_layer_spec(a, layer) if i in layered else _whole_spec(a) for i, a in enumerate(args)]
    outs = [jax.ShapeDtypeStruct((64, n), F32)] * 3 + [jax.ShapeDtypeStruct((db, BW), F32),
            jax.ShapeDtypeStruct((64, 64, n), F32), jax.ShapeDtypeStruct((32, 64, n), F32),
            jax.ShapeDtypeStruct((db, BW), F32)]
    oa, ob, od, oc, s_new, g_new, h_new = pl.pallas_call(
        functools.partial(_step_kernel, lam_init=_lam_init(p["layer"])),
        grid=(1,),
        in_specs=in_specs,
        out_specs=[pl.BlockSpec(o.shape, lambda i, nd=len(o.shape): (0,) * nd) for o in outs],
        out_shape=outs,
        compiler_params=_cparams(("arbitrary",)),
        name="step_sample",
    )(*args)
    ret_new = s_new.reshape(64, 64, db, HEADS).transpose(2, 3, 0, 1)
    gla_new = g_new.reshape(32, 64, db, HEADS).transpose(2, 3, 0, 1)
    return _from_fm(oa, db, 64), _from_fm(ob, db, 64), oc, _from_fm(od, db, 64), ret_new, h_new, gla_new


def _block_diag(w):
    d, h, n, _ = w.shape
    eye = jnp.eye(h, dtype=w.dtype)
    return jnp.einsum("dhij,hg->dhigj", w, eye).reshape(d, h * n, h * n)


def _seg_matrix(seg):
    idx = np.arange(BW)
    return jnp.asarray((idx[:, None] // seg == idx[None, :] // seg) / float(seg), F32)


def _parts(w, prec, with_cast=False):
    if prec == 2:
        return w[:, None]
    if prec == 0:
        return w.astype(BF16)[:, None]
    return jnp.stack(_split_bits(w) + ((w.astype(BF16),) if with_cast else ()), axis=1)


def _prepare(raw, layers, prec, moe_prec, layer):
    sl = lambda a: a[layers]
    (norm1, w_in, qn, kn, lam, an, rn, conv_w, conv_b, lru_wa, lru_ba, lru_wx, lru_bx, lru_lam, gla_w2, gla_b,
     gn, w_branch, w_merge, b_merge, w_out, norm2, rg_w, rg_b, re_w, re_b, ex_w1, ex_w3, ex_w2) = raw
    nl = sl(norm1).shape[0]
    r3 = lambda a: sl(a).reshape(nl, 1, -1)
    tile3 = lambda a, k: jnp.tile(sl(a), (1, k)).reshape(nl, 1, -1)
    pad_cols = lambda a, w: jnp.pad(a, ((0, 0), (0, 0), (0, w - a.shape[-1])))
    w_in = sl(w_in)
    c = np.cumsum([0, 768, 1024, 512, 784])
    rw = jnp.concatenate([pad_cols(sl(rg_w), 128), pad_cols(sl(re_w), 128)], axis=-1)
    return dict(
        idx=layer if nl > 1 else 0, layer=layer,
        norm1=r3(norm1), qn_t=tile3(qn, 8), kn_t=tile3(kn, 8), lam=sl(lam), an_t=tile3(an, 4), rn_t=tile3(rn, 4),
        gn_t=tile3(gn, 4), an_c=sl(an).reshape(nl, 64, 1), rn_c=sl(rn).reshape(nl, 64, 1),
        gn_c=sl(gn).reshape(nl, 64, 1), conv_w=sl(conv_w), conv_b=r3(conv_b), lru_ba=r3(lru_ba),
        lru_bx=r3(lru_bx), lru_lam=r3(lru_lam), gla_b=r3(gla_b), b_merge=r3(b_merge), norm2=r3(norm2),
        rb=jnp.concatenate([pad_cols(r3(rg_b), 128), pad_cols(r3(re_b), 128)], axis=-1),
        m32=_seg_matrix(32), m64=_seg_matrix(64),
        wa=_parts(w_in[:, :, c[0]:c[1]], prec), wb=_parts(w_in[:, :, c[1]:c[2]], prec),
        wc=_parts(w_in[:, :, c[2]:c[3]], prec), wd=_parts(pad_cols(w_in[:, :, c[3]:c[4]], W_D), prec),
        gla_w2p=_parts(jnp.pad(sl(gla_w2), ((0, 0), (0, 128 - GLA_RANK), (0, 0))), prec),
        lru_wa_bd=_parts(_block_diag(sl(lru_wa)), prec), lru_wx_bd=_parts(_block_diag(sl(lru_wx)), prec),
        w_merge=_parts(sl(w_merge), prec, True), w_branch=_parts(sl(w_branch), prec, True),
        w_out=_parts(sl(w_out), prec, True),
        rw=_parts(rw, max(moe_prec, 1)),
        ex_w1=_parts(sl(ex_w1), moe_prec), ex_w3=_parts(sl(ex_w3), moe_prec), ex_w2=_parts(sl(ex_w2), moe_prec),
    )


def _diag_blocks(s, rows, cols):
    b = s.shape[0]
    s5 = s.reshape(b, HEADS, rows, HEADS, cols)
    return jnp.stack([s5[:, h, :, h, :] for h in range(HEADS)], axis=1)


def _prompt_tile(n, prec):
    for tm in ((256, 128) if prec else (512, 256, 128)):
        if n % tm == 0:
            return tm
    raise ValueError(n)


def kernel(x_prompt, x_sample, cache_k_a, cache_v_a, state_ret, state_lru_h, state_lru_conv, state_gla,
           page_table, meta, norm1, w_in, qn, kn, lam, an, rn, conv_w, conv_b, lru_wa, lru_ba, lru_wx,
           lru_bx, lru_lam, gla_w2, gla_b, gn, w_branch, w_merge, b_merge, w_out, norm2, rg_w, rg_b,
           re_w, re_b, ex_w1, ex_w3, ex_w2):
    bp, seq, _ = x_prompt.shape
    db = x_sample.shape[0]
    depth = w_in.shape[0]
    assert seq % BLK == 0 and x_sample.shape[1] == 1
    lp = BLK + seq
    past = page_table.shape[1] * PAGE
    raw = (norm1, w_in, qn, kn, lam, an, rn, conv_w, conv_b, lru_wa, lru_ba, lru_wx, lru_bx, lru_lam, gla_w2,
           gla_b, gn, w_branch, w_merge, b_merge, w_out, norm2, rg_w, rg_b, re_w, re_b, ex_w1, ex_w3, ex_w2)
    head = jnp.concatenate([jnp.zeros((PADL, D_MODEL), F32), meta.astype(F32)], axis=0)
    xp = jnp.concatenate([jnp.broadcast_to(head[None], (bp, BLK, D_MODEL)), x_prompt], axis=1)
    xp = xp.reshape(bp * lp, D_MODEL)
    xs = x_sample.reshape(db, D_MODEL)
    new_p, new_s = [], []
    for layer in range(depth):
        prec = 1 if layer == 0 else 0
        tm = _prompt_tile(bp * lp, prec)
        p = _prepare(raw, slice(layer, layer + 1), prec, 0, layer)
        ka, va, sb, sc, sd, qp, kp, vp = _in_proj(xp, p, tm, prec, lp)
        oa = _attn_prompt(qp, kp, vp, p, bp, lp, prec)
        ob, s_ret = _ret_prompt(sb, p, bp, lp, prec)
        oc, h_last = _lru_prompt(sc, p, bp, lp, prec)
        od, s_gla = _gla_prompt(sd, p, bp, lp, prec)
        tail_rows = [b * lp + lp - 1 - r for b in range(bp) for r in range(CONV_W - 1)]
        hres = _merge(xp, oa, ob, oc, od, p, tm, prec, tail_rows)
        xp = _moe(hres, p, _prompt_tile(bp * lp, 0), 0)
        new_p.append((ka.reshape(bp, lp, HEADS, 64)[:, PADL:], va.reshape(bp, lp, HEADS, 64)[:, PADL:],
                      _diag_blocks(s_ret, 64, 64), h_last,
                      sc.reshape(bp, lp, W_C)[:, lp - (CONV_W - 1):, 0:BW],
                      _diag_blocks(s_gla, 64, 32).transpose(0, 1, 3, 2)))
        ps = _prepare(raw, slice(None), 2, 2, layer)
        ka, va, sb, sc, sd, qa = _in_proj(xs, ps, db, 2)
        oa_fm = _dec_attn(qa, ka, va, cache_k_a, cache_v_a, page_table, ps)
        oa, ob, oc, od, ret_new, h_new, gla_new = _sample_step(
            oa_fm, sb, sc, sd, state_ret[layer], state_gla[layer], state_lru_conv[layer], state_lru_h[layer],
            ps, past)
        hres = _merge(xs, oa, ob, oc, od, ps, db, 2)
        xs = _moe(hres, ps, db, 2)
        conv_new = jnp.concatenate([state_lru_conv[layer][:, 1:], sc[:, None, 0:BW]], axis=1)
        new_s.append((ka.reshape(db, 1, HEADS, 64), va.reshape(db, 1, HEADS, 64), ret_new, h_new, conv_new,
                      gla_new))
    stk = lambda states, i: jnp.stack([s[i] for s in states])
    y_prompt = xp.reshape(bp, lp, D_MODEL)[:, BLK:]
    return ((y_prompt, xs.reshape(db, 1, D_MODEL))
            + tuple(stk(new_p, i) for i in range(6)) + tuple(stk(new_s, i) for i in range(6)))
```

```python
import functools
import math

import numpy as np
import jax
import jax.numpy as jnp
from jax import lax
from jax.experimental import pallas as pl
from jax.experimental.pallas import tpu as pltpu

F32 = jnp.float32
BF16 = jnp.bfloat16
EPS = 1e-6
D_MODEL = 1024
N_META = 16
PAGE = 128
BLK = 128
PADL = BLK - N_META
HEADS = 4
BW = 256
DQA = 32
ROPE_BASE = 10000.0
CONV_W = 4
LRU_C = 8.0
GLA_RANK = 16
GLA_TAU = 16.0
GLA_CHUNK = 16
N_GROUPS = 4
EXP_PER_GROUP = 4
N_EXPERTS = 16
D_EXPERT = 256
W_A, W_B, W_C, W_D = 768, 1024, 512, 896
QK_PACK = 8 * 128
V_PACK = HEADS * 128
VMEM_LIMIT = 56 * 1024 * 1024
NEG = -1e30
PAGES_PER_STEP = 16
ATTN_KEY_STEP = 512
PAD_LANE = 96
LOG2E = math.log2(math.e)


def _cparams(sem):
    return pltpu.CompilerParams(dimension_semantics=sem, vmem_limit_bytes=VMEM_LIMIT)


def _layer_spec(arr, layer):
    nd = arr.ndim
    return pl.BlockSpec((1,) + arr.shape[1:], lambda *_: (layer,) + (0,) * (nd - 1),
                        pipeline_mode=pl.Buffered(1))


def _whole_spec(arr):
    nd = arr.ndim
    return pl.BlockSpec(arr.shape, lambda *_: (0,) * nd, pipeline_mode=pl.Buffered(1))


def _bf16_prefix(x):
    bits = lax.bitcast_convert_type(x, jnp.uint32)
    bits = (bits + jnp.uint32(0x7FFF) + ((bits >> 16) & jnp.uint32(1))) & jnp.uint32(0xFFFF0000)
    return lax.bitcast_convert_type(bits, F32)


def _split_bits(x):
    hi = _bf16_prefix(x)
    return hi.astype(BF16), (x - hi).astype(BF16)


def _split(x):
    hi = x.astype(BF16)
    return hi, (x - hi.astype(F32)).astype(BF16)


def _split3(x):
    hi = x.astype(BF16)
    r = x - hi.astype(F32)
    mid = r.astype(BF16)
    return hi, mid, (r - mid.astype(F32)).astype(BF16)


def _bdot(a, b, dims=None):
    if dims is None:
        return jnp.dot(a, b, preferred_element_type=F32)
    return lax.dot_general(a, b, (dims, ((), ())), preferred_element_type=F32)


def _branch_dtype(prec):
    return BF16 if prec == 0 else F32


def _operand(a, prec):
    if prec == 2:
        return a
    return _split(a) if prec == 1 else a.astype(BF16)


def _wdot(a, w_ref, idx, prec):
    get = lambda part: w_ref[(0, part) + idx]
    if prec == 2:
        return jnp.dot(a, get(0), precision=lax.Precision.HIGHEST, preferred_element_type=F32)
    if prec == 1:
        a_hi, a_lo = a if isinstance(a, tuple) else _split(a)
        return _bdot(a_hi, get(0)) + _bdot(a_lo, get(0)) + _bdot(a_hi, get(1))
    assert w_ref.shape[1] in (1, 3), "level 0 needs the plain bf16 cast: parts [cast] or [hi, lo, cast]"
    return _bdot(a.astype(BF16), get(w_ref.shape[1] - 1))


def _adot(a, b, prec, dims=None):
    if prec == 1:
        a_hi, a_lo = _split(a)
        b_hi, b_lo = _split(b)
        return _bdot(a_hi, b_hi, dims) + _bdot(a_lo, b_hi, dims) + _bdot(a_hi, b_lo, dims)
    return _bdot(a.astype(BF16), b.astype(BF16), dims)


NT = ((1,), (1,))
TN = ((0,), (0,))


def _seg_mean(x, m, prec=0):
    if prec == 2:
        return jnp.dot(x, m, precision=lax.Precision.HIGHEST, preferred_element_type=F32)
    hi, lo = _split(x)
    mb = m.astype(BF16)
    return _bdot(hi, mb) + _bdot(lo, mb)


def _rms(x, g):
    return x * lax.rsqrt(jnp.mean(x * x, axis=-1, keepdims=True) + EPS) * g


def _sigmoid(x):
    return 1.0 / (1.0 + jnp.exp(-x))


def _silu(x):
    return x * _sigmoid(x)


def _log_sigmoid(x):
    return jnp.minimum(x, 0.0) - jnp.log(1.0 + jnp.exp(-jnp.abs(x)))


def _softplus(x):
    return jnp.maximum(x, 0.0) + jnp.log(1.0 + jnp.exp(-jnp.abs(x)))


def _gelu_tanh(x):
    return 0.5 * x * (1.0 + jnp.tanh(math.sqrt(2.0 / math.pi) * (x + 0.044715 * (x * x * x))))


def _lane_mask(width, lo, hi):
    lane = lax.broadcasted_iota(jnp.int32, (1, width), 1)
    return (lane >= lo) & (lane < hi)


def _lam_scalar(lv, lam_init):
    s01 = jnp.sum(lv[0:1] * lv[1:2], axis=-1, keepdims=True)
    s23 = jnp.sum(lv[2:3] * lv[3:4], axis=-1, keepdims=True)
    return jnp.exp(s01) - jnp.exp(s23) + lam_init


def _lam_init(layer):
    return 0.8 - 0.6 * math.exp(-0.3 * layer)


def _in_proj_kernel(x_ref, g_ref, wa_ref, wb_ref, wc_ref, wd_ref, qn_ref, kn_ref, m32_ref, w2_ref, gb_ref,
                    pq_ref, pk_ref, pv_ref, ka_ref, va_ref, sb_ref, sc_ref, sd_ref, *q_out, prec, seq_len):
    xn = _rms(x_ref[...], g_ref[0])
    xs = _operand(xn, prec)
    a = _wdot(xs, wa_ref, (), prec)
    sb_ref[...] = _wdot(xs, wb_ref, (), prec)
    sc_ref[...] = _wdot(xs, wc_ref, (), prec)
    d = _wdot(xs, wd_ref, (), prec)
    q, k, v = a[:, 0:BW], a[:, BW:2 * BW], a[:, 2 * BW:3 * BW]
    m32 = m32_ref[...]
    qa = q * lax.rsqrt(_seg_mean(q * q, m32, prec) + EPS) * qn_ref[0] * (DQA ** -0.5)
    ka = k * lax.rsqrt(_seg_mean(k * k, m32, prec) + EPS) * kn_ref[0]
    ka_ref[...] = ka
    va_ref[...] = v
    sd_ref[:, 0:768] = d[:, 0:768]
    z = _wdot(d[:, 768:896], w2_ref, (), prec) + gb_ref[0]
    sd_ref[:, 768:896] = _log_sigmoid(z) * (1.0 / GLA_TAU)
    if len(q_out) == 1:
        q_out[0][...] = qa
    else:
        def place(src, p_ref):
            hi, lo = _split(src)
            return _bdot(hi, p_ref[0]) + _bdot(lo, p_ref[1])

        tm = x_ref.shape[0]
        row = (pl.program_id(0) * tm + lax.broadcasted_iota(jnp.int32, (tm, 1), 0)).astype(F32)
        in_seq = row - jnp.floor((row + 0.5) * (1.0 / seq_len)) * seq_len
        is_pad = in_seq < PADL
        flag = (lax.broadcasted_iota(jnp.int32, (1, QK_PACK), 1) % 128 == PAD_LANE).astype(F32)
        qp_ref, kp_ref, vp_ref = q_out
        qp_ref[...] = (place(qa * LOG2E, pq_ref) + jnp.where(is_pad, 0.0, 1.0) * flag).astype(BF16)
        kp_ref[...] = (place(ka, pk_ref) + jnp.where(is_pad, NEG, 0.0) * flag).astype(BF16)
        vp_ref[...] = place(v, pv_ref).astype(BF16)


def _pack_matrices():
    pq = np.zeros((2, BW, QK_PACK), np.float32)
    pk = np.zeros((2, BW, QK_PACK), np.float32)
    pv = np.zeros((2, BW, V_PACK), np.float32)
    for g in range(8):
        for d in range(DQA):
            r, c = DQA * g + d, 128 * g + d
            pq[0, r, c] = pq[0, r, c + 64] = 1.0
            pq[1, r, c + 32] = 1.0
            pk[0, r, c] = pk[0, r, c + 32] = 1.0
            pk[1, r, c + 64] = 1.0
    for h in range(HEADS):
        for d in range(64):
            pv[0, 64 * h + d, 128 * h + d] = 1.0
            pv[1, 64 * h + d, 128 * h + 64 + d] = 1.0
    return [jnp.asarray(a, BF16) for a in (pq, pk, pv)]


def _in_proj(x, p, tm, prec, seq_len=None):
    n = x.shape[0]
    packed = seq_len is not None
    layer = p["idx"]
    outs = [jax.ShapeDtypeStruct((n, BW), F32)] * 2 + [jax.ShapeDtypeStruct((n, W_B), F32),
            jax.ShapeDtypeStruct((n, W_C), F32), jax.ShapeDtypeStruct((n, W_D), F32)]
    if packed:
        outs += [jax.ShapeDtypeStruct((n, QK_PACK), BF16)] * 2 + [jax.ShapeDtypeStruct((n, V_PACK), BF16)]
    else:
        outs += [jax.ShapeDtypeStruct((n, BW), F32)]
    row = lambda w: pl.BlockSpec((tm, w), lambda i: (i, 0))
    params = [p["norm1"], p["wa"], p["wb"], p["wc"], p["wd"], p["qn_t"], p["kn_t"]]
    tail = [p["gla_w2p"], p["gla_b"]]
    consts = _pack_matrices()
    return pl.pallas_call(
        functools.partial(_in_proj_kernel, prec=prec, seq_len=seq_len),
        grid=(n // tm,),
        in_specs=([row(D_MODEL)] + [_layer_spec(a, layer) for a in params] + [_whole_spec(p["m32"])]
                  + [_layer_spec(a, layer) for a in tail] + [_whole_spec(a) for a in consts]),
        out_specs=[row(o.shape[1]) for o in outs],
        out_shape=outs,
        compiler_params=_cparams(("parallel",)),
        name="in_proj_p%d" % prec,
    )(x, *params, p["m32"], *tail, *consts)


def _attn_body(q_ref, k_ref, v_ref, lam, an_ref, o_ref, s_buf, w_buf, i, nk, *, lam_init, tq, prec):
    n_free = max(nk - ATTN_KEY_STEP, 0)
    pos_q = i * tq + lax.broadcasted_iota(jnp.int32, (tq, 1), 0)
    causal = n_free + lax.broadcasted_iota(jnp.int32, (1, nk - n_free), 1) <= pos_q
    lane = lax.broadcasted_iota(jnp.int32, (1, 128), 1)
    for g in range(2 * HEADS):
        s_buf[g, :, 0:nk] = _bdot(q_ref[:, 128 * g:128 * (g + 1)], k_ref[0:nk, 128 * g:128 * (g + 1)], NT)
    outs = []
    for h in range(HEADS):
        scale = []
        for m in range(2):
            g = 2 * h + m
            tail = jnp.where(causal, s_buf[g, :, n_free:nk], NEG)
            mx = jnp.max(tail, axis=-1, keepdims=True)
            if n_free:
                mx = jnp.maximum(mx, jnp.max(s_buf[g, :, 0:n_free], axis=-1, keepdims=True))
                head_p = jnp.exp2(s_buf[g, :, 0:n_free] - mx)
                s_buf[g, :, 0:n_free] = head_p
            tail_p = jnp.exp2(tail - mx)
            s_buf[g, :, n_free:nk] = tail_p
            total = jnp.sum(tail_p, axis=-1, keepdims=True)
            if n_free:
                total = total + jnp.sum(head_p, axis=-1, keepdims=True)
            scale.append(1.0 / total)
        w = s_buf[2 * h, :, 0:nk] * scale[0] - s_buf[2 * h + 1, :, 0:nk] * (lam * scale[1])
        vt = v_ref[0:nk, 128 * h:128 * (h + 1)]
        if prec == 1:
            w_hi, w_lo = _split(w)
            w_buf[h, 0:tq, 0:nk] = w_hi
            w_buf[h, tq:2 * tq, 0:nk] = w_lo
            o2 = _bdot(w_buf[h, :, 0:nk], vt)
            o = o2[0:tq] + o2[tq:2 * tq]
        else:
            w_buf[h, 0:tq, 0:nk] = w.astype(BF16)
            o = _bdot(w_buf[h, 0:tq, 0:nk], vt)
        o = o + pltpu.roll(o, 64, 1)
        outs.append(o * lax.rsqrt(jnp.mean(o * o, axis=-1, keepdims=True) + EPS))
    for t in range(2):
        pair = jnp.where(lane < 64, outs[2 * t], outs[2 * t + 1])
        o_ref[:, 128 * t:128 * (t + 1)] = (pair * an_ref[0][:, 128 * t:128 * (t + 1)]
                                           * (1.0 - lam_init)).astype(o_ref.dtype)


def _attn_kernel(q_ref, k_ref, v_ref, lam_ref, an_ref, o_ref, s_buf, w_buf, *, lam_init, tq, key_ranges, prec):
    i = pl.program_id(1)
    lam = _lam_scalar(lam_ref[0], lam_init)
    need = (i + 1) * tq
    lo = 0
    for nk in key_ranges:
        @pl.when((need > lo) & (need <= nk))
        def _(nk=nk):
            _attn_body(q_ref, k_ref, v_ref, lam, an_ref, o_ref, s_buf, w_buf, i, nk, lam_init=lam_init, tq=tq,
                       prec=prec if nk == key_ranges[-1] else 0)
        lo = nk


def _attn_prompt(qp, kp, vp, p, b, lp, prec):
    nq = lp // BLK
    layer = p["idx"]
    key_ranges = tuple(sorted({min(lp, ATTN_KEY_STEP * j) for j in range(1, -(-lp // ATTN_KEY_STEP) + 1)}))
    return pl.pallas_call(
        functools.partial(_attn_kernel, lam_init=_lam_init(p["layer"]), tq=BLK, key_ranges=key_ranges,
                          prec=prec),
        grid=(b, nq),
        in_specs=[pl.BlockSpec((BLK, QK_PACK), lambda bi, i: (bi * nq + i, 0)),
                  pl.BlockSpec((lp, QK_PACK), lambda bi, i: (bi, 0)),
                  pl.BlockSpec((lp, V_PACK), lambda bi, i: (bi, 0)),
                  _layer_spec(p["lam"], layer), _layer_spec(p["an_t"], layer)],
        out_specs=pl.BlockSpec((BLK, BW), lambda bi, i: (bi * nq + i, 0)),
        out_shape=jax.ShapeDtypeStruct((b * lp, BW), _branch_dtype(prec)),
        scratch_shapes=[pltpu.VMEM((2 * HEADS, BLK, lp), F32), pltpu.VMEM((HEADS, 2 * BLK, lp), BF16)],
        compiler_params=_cparams(("parallel", "parallel")),
        name="attn_prompt_p%d" % prec,
    )(qp, kp, vp, p["lam"], p["an_t"])


def _swap_halves(x):
    w = x.shape[-1]
    lane = lax.broadcasted_iota(jnp.int32, (1, w), 1)
    return jnp.where((lane % 64) < 32, pltpu.roll(x, w - 32, 1), pltpu.roll(x, 32, 1))


def _ret_kernel(sb_ref, cos_ref, sin_ref, dmat_ref, dq_ref, dk_ref, dstate_ref, rn_ref, m64_ref,
                o_ref, s_out_ref, s_ref, *, prec):
    c = pl.program_id(1)

    @pl.when(c == 0)
    def _():
        s_ref[...] = jnp.zeros_like(s_ref)

    cos, sin = cos_ref[...], sin_ref[...]
    q, k, v = sb_ref[:, 0:BW], sb_ref[:, BW:2 * BW], sb_ref[:, 2 * BW:3 * BW]
    qr = q * cos + _swap_halves(q) * sin
    kr = (k * cos + _swap_halves(k) * sin) * 0.125
    o = _adot(qr, s_ref[...], prec) * dq_ref[...]
    upd = _adot(kr * dk_ref[...], v, prec, TN)
    masks = [_lane_mask(BW, 64 * h, 64 * (h + 1)) for h in range(HEADS)]
    scores = [_adot(jnp.where(hm, qr, 0.0), kr, prec, NT) for hm in masks]
    for h, hm in enumerate(masks):
        o = o + jnp.where(hm, _adot(scores[h] * dmat_ref[h], v, prec), 0.0)
    row_head = lax.broadcasted_iota(jnp.int32, (BW, BW), 0) // 64
    col_head = lax.broadcasted_iota(jnp.int32, (BW, BW), 1) // 64
    s_new = s_ref[...] * dstate_ref[...] + jnp.where(row_head == col_head, upd, 0.0)
    s_ref[...] = s_new
    s_out_ref[0] = s_new
    m64 = m64_ref[...]
    mu = _seg_mean(o, m64)
    var = _seg_mean((o - mu) * (o - mu), m64)
    g = sb_ref[:, 3 * BW:4 * BW]
    o_ref[...] = ((o - mu) * lax.rsqrt(var + EPS) * rn_ref[0] * _silu(g)).astype(o_ref.dtype)


def _ret_gammas():
    return 1.0 - 2.0 ** (-5.0 - np.arange(HEADS, dtype=np.float64))


def _ret_tables():
    lg = np.log(_ret_gammas())
    t = np.arange(BLK, dtype=np.float64)
    rel = t[:, None] - t[None, :]
    dmat = np.where(rel >= 0, np.exp(np.maximum(rel, 0.0)[None] * lg[:, None, None]), 0.0)
    lane_lg = np.repeat(lg, 64)[None, :]
    dq = np.exp((t[:, None] + 1.0) * lane_lg)
    dk = np.exp((BLK - 1.0 - t)[:, None] * lane_lg)
    dstate = np.exp(BLK * lane_lg)
    return [jnp.asarray(a, F32) for a in (dmat, dq, dk, dstate)]


def _rope_cos_sin(pos):
    inv = ROPE_BASE ** (-jnp.linspace(0.0, 1.0, 32))
    ang = pos.astype(F32)[:, None] * inv[None]
    return jnp.cos(ang), jnp.sin(ang)


def _ret_prompt(sb, p, b, lp, prec):
    nc = lp // BLK
    cos, sin = _rope_cos_sin(jnp.arange(lp) - PADL)
    cos_t = jnp.tile(jnp.concatenate([cos, cos], -1), (1, HEADS))
    sin_t = jnp.tile(jnp.concatenate([-sin, sin], -1), (1, HEADS))
    tables = _ret_tables()
    return pl.pallas_call(
        functools.partial(_ret_kernel, prec=prec),
        grid=(b, nc),
        in_specs=[pl.BlockSpec((BLK, W_B), lambda bi, c: (bi * nc + c, 0)),
                  pl.BlockSpec((BLK, BW), lambda bi, c: (c, 0)),
                  pl.BlockSpec((BLK, BW), lambda bi, c: (c, 0))]
                 + [_whole_spec(a) for a in tables] + [_layer_spec(p["rn_t"], p["idx"]), _whole_spec(p["m64"])],
        out_specs=[pl.BlockSpec((BLK, BW), lambda bi, c: (bi * nc + c, 0)),
                   pl.BlockSpec((1, BW, BW), lambda bi, c: (bi, 0, 0))],
        out_shape=[jax.ShapeDtypeStruct((b * lp, BW), _branch_dtype(prec)), jax.ShapeDtypeStruct((b, BW, BW), F32)],
        scratch_shapes=[pltpu.VMEM((BW, BW), F32)],
        compiler_params=_cparams(("parallel", "arbitrary")),
        name="ret_prompt_p%d" % prec,
    )(sb, cos_t, sin_t, *tables, p["rn_t"], p["m64"])


def _lru_gates(y, wa_ref, wx_ref, ba, bx, sp, prec):
    ys = _operand(y, prec)
    r = _sigmoid(_wdot(ys, wa_ref, (), prec) + ba)
    ig = _sigmoid(_wdot(ys, wx_ref, (), prec) + bx)
    log_a = -LRU_C * r * sp
    a = jnp.exp(log_a)
    u = jnp.sqrt(1.0 - jnp.exp(2.0 * log_a)) * (ig * y)
    return a, u


def _lru_kernel(sc_ref, cw_ref, cb_ref, wa_ref, wx_ref, ba_ref, bx_ref, lam_ref,
                o_ref, hl_ref, cbuf, a_s, u_s, h_s, hcar, *, nb, prec):
    t = pl.program_id(0)

    @pl.when(t == 0)
    def _():
        cbuf[...] = jnp.zeros_like(cbuf)
        hcar[...] = jnp.zeros_like(hcar)

    sp = _softplus(-lam_ref[0])
    row = t * BLK + lax.broadcasted_iota(jnp.int32, (BLK, 1), 0)
    for b in range(nb):
        x = sc_ref[b, :, 0:BW]
        cbuf[b, 8:8 + BLK, :] = x
        y = cb_ref[0] + cbuf[b, 5:5 + BLK, :] * cw_ref[0, 0:1, :]
        for i in range(1, CONV_W):
            y = y + cbuf[b, 5 + i:5 + i + BLK, :] * cw_ref[0, i:i + 1, :]
        cbuf[b, 0:8, :] = x[BLK - 8:BLK, :]
        a, u = _lru_gates(y, wa_ref, wx_ref, ba_ref[0], bx_ref[0], sp, prec)
        a_s[b] = a
        u_s[b] = jnp.where(row >= PADL, u, 0.0)

    def step(i, hs):
        out = []
        for b in range(nb):
            h = a_s[b, pl.ds(i, 1), :] * hs[b] + u_s[b, pl.ds(i, 1), :]
            h_s[b, pl.ds(i, 1), :] = h
            out.append(h)
        return tuple(out)

    hs = lax.fori_loop(0, BLK, step, tuple(hcar[b:b + 1, :] for b in range(nb)))
    for b in range(nb):
        hcar[b:b + 1, :] = hs[b]
        o_ref[b] = (h_s[b] * _gelu_tanh(sc_ref[b, :, BW:2 * BW])).astype(o_ref.dtype)
    hl_ref[...] = hcar[...]


def _lru_prompt(sc, p, b, lp, prec):
    nt = lp // BLK
    sc3 = sc.reshape(b, lp, W_C)
    params = [p["conv_w"], p["conv_b"], p["lru_wa_bd"], p["lru_wx_bd"], p["lru_ba"], p["lru_bx"], p["lru_lam"]]
    o, hl = pl.pallas_call(
        functools.partial(_lru_kernel, nb=b, prec=prec),
        grid=(nt,),
        in_specs=[pl.BlockSpec((b, BLK, W_C), lambda t: (0, t, 0))] + [_layer_spec(a, p["idx"]) for a in params],
        out_specs=[pl.BlockSpec((b, BLK, BW), lambda t: (0, t, 0)), pl.BlockSpec((b, BW), lambda t: (0, 0))],
        out_shape=[jax.ShapeDtypeStruct((b, lp, BW), _branch_dtype(prec)), jax.ShapeDtypeStruct((b, BW), F32)],
        scratch_shapes=[pltpu.VMEM((b, BLK + 8, BW), F32), pltpu.VMEM((b, BLK, BW), F32),
                        pltpu.VMEM((b, BLK, BW), F32), pltpu.VMEM((b, BLK, BW), F32), pltpu.VMEM((b, BW), F32)],
        compiler_params=_cparams(("arbitrary",)),
        name="lru_prompt_p%d" % prec,
    )(sc3, *params)
    return o.reshape(b * lp, BW), hl


def _gla_kernel(sd_ref, tri_ref, ind_ref, gn_ref, m64_ref, o_ref, st_out_ref, st_ref, o_s, upd_s, *, prec):
    c = pl.program_id(1)

    @pl.when(c == 0)
    def _():
        st_ref[...] = jnp.zeros_like(st_ref)

    q = sd_ref[:, 0:128] * (DQA ** -0.5)
    k = sd_ref[:, 128:256]
    v = sd_ref[:, 256:512]
    la = sd_ref[:, 768:896]
    tri = tri_ref[...]
    bc = sum(_bdot(tri, part) for part in _split3(la))
    ind = ind_ref[...]
    t_idx = lax.broadcasted_iota(jnp.int32, (GLA_CHUNK, 1), 0)
    row_head = lax.broadcasted_iota(jnp.int32, (BW, 128), 0) // 64
    col_head = lax.broadcasted_iota(jnp.int32, (BW, 128), 1) // 32
    bd = row_head == col_head
    n_sub = BLK // GLA_CHUNK

    def local_cumsum(s):
        r0 = s * GLA_CHUNK
        bcs = bc[r0:r0 + GLA_CHUNK]
        return bcs - bc[r0 - 1:r0] if s > 0 else bcs

    def run(out_prec):
        for s in range(n_sub):
            r0 = s * GLA_CHUNK
            bcs = local_cumsum(s)
            qs, ks, vs = q[r0:r0 + GLA_CHUNK], k[r0:r0 + GLA_CHUNK], v[r0:r0 + GLA_CHUNK]
            zs = []
            for j in range(GLA_CHUNK):
                dec = jnp.where(t_idx >= j, jnp.exp(jnp.minimum(bcs - bcs[j:j + 1], 0.0)), 0.0)
                zs.append(qs * dec * ks[j:j + 1])
            z = jnp.concatenate(zs, axis=0)
            if out_prec == 1:
                z_hi, z_lo = _split(z)
                aa = _bdot(z_hi, ind) + _bdot(z_lo, ind)
            else:
                aa = _bdot(z.astype(BF16), ind)
            o = aa[0:GLA_CHUNK] * vs[0:1]
            for j in range(1, GLA_CHUNK):
                o = o + aa[j * GLA_CHUNK:(j + 1) * GLA_CHUNK] * vs[j:j + 1]
            o_s[r0:r0 + GLA_CHUNK, :] = o
            last = bcs[GLA_CHUNK - 1:GLA_CHUNK]
            upd_s[s] = jnp.where(bd, _adot(vs, ks * jnp.exp(last - bcs), prec, TN), 0.0)
        st = st_ref[...]
        for s in range(n_sub):
            r0 = s * GLA_CHUNK
            bcs = local_cumsum(s)
            o_s[r0:r0 + GLA_CHUNK, :] += _adot(q[r0:r0 + GLA_CHUNK] * jnp.exp(bcs), st, out_prec, NT)
            st = st * jnp.exp(bcs[GLA_CHUNK - 1:GLA_CHUNK]) + upd_s[s]
        st_ref[...] = st
        st_out_ref[0] = st
        o = o_s[...]
        ms = _seg_mean(o * o, m64_ref[...])
        o_ref[...] = (o * lax.rsqrt(ms + EPS) * gn_ref[0] * _silu(sd_ref[:, 512:768])).astype(o_ref.dtype)

    if prec == 1:
        is_last = c == pl.num_programs(1) - 1
        pl.when(is_last)(lambda: run(1))
        pl.when(jnp.logical_not(is_last))(lambda: run(0))
    else:
        run(prec)


def _gla_prompt(sd, p, b, lp, prec):
    nc = lp // BLK
    t = np.arange(BLK)
    tri = jnp.asarray(t[:, None] >= t[None, :], BF16)
    ind = jnp.asarray((np.arange(128)[:, None] // 32) == (np.arange(BW)[None, :] // 64), BF16)
    return pl.pallas_call(
        functools.partial(_gla_kernel, prec=prec),
        grid=(b, nc),
        in_specs=[pl.BlockSpec((BLK, W_D), lambda bi, c: (bi * nc + c, 0)),
                  _whole_spec(tri), _whole_spec(ind), _layer_spec(p["gn_t"], p["idx"]), _whole_spec(p["m64"])],
        out_specs=[pl.BlockSpec((BLK, BW), lambda bi, c: (bi * nc + c, 0)),
                   pl.BlockSpec((1, BW, 128), lambda bi, c: (bi, 0, 0))],
        out_shape=[jax.ShapeDtypeStruct((b * lp, BW), _branch_dtype(prec)), jax.ShapeDtypeStruct((b, BW, 128), F32)],
        scratch_shapes=[pltpu.VMEM((BW, 128), F32), pltpu.VMEM((BLK, BW), F32),
                        pltpu.VMEM((BLK // GLA_CHUNK, BW, 128), F32)],
        compiler_params=_cparams(("parallel", "arbitrary")),
        name="gla_prompt_p%d" % prec,
    )(sd, tri, ind, p["gn_t"], p["m64"])


def _merge_kernel(fine_ref, x_ref, g_ref, oa_ref, ob_ref, oc_ref, od_ref, wm_ref, bm_ref, wbr_ref, wo_ref, y_ref,
                  *, prec):
    def body(level):
        x = x_ref[...]
        xs = _operand(_rms(x, g_ref[0]), level)
        acc = None
        for n, br in enumerate((oa_ref, ob_ref, oc_ref, od_ref)):
            cols = slice(n * D_MODEL, (n + 1) * D_MODEL)
            gate = _sigmoid(_wdot(xs, wm_ref, (slice(None), cols), level) + bm_ref[0, :, cols])
            term = gate * _wdot(br[...], wbr_ref, (n,), level)
            acc = term if acc is None else acc + term
        y_ref[...] = x + _wdot(acc, wo_ref, (), level)

    if prec == 1:
        fine = fine_ref[pl.program_id(0)]
        pl.when(fine != 0)(lambda: body(1))
        pl.when(fine == 0)(lambda: body(0))
    else:
        body(prec)


def _merge(x, oa, ob, oc, od, p, tm, prec, fine_rows=None):
    n = x.shape[0]
    layer = p["idx"]
    fine = np.ones((n // tm,), np.int32)
    if fine_rows is not None:
        fine = np.zeros((n // tm,), np.int32)
        fine[np.asarray(fine_rows) // tm] = 1
    row = lambda w: pl.BlockSpec((tm, w), lambda i, f: (i, 0))
    params = [p["w_merge"], p["b_merge"], p["w_branch"], p["w_out"]]
    grid_spec = pltpu.PrefetchScalarGridSpec(
        num_scalar_prefetch=1,
        grid=(n // tm,),
        in_specs=[row(D_MODEL), _layer_spec(p["norm1"], layer), row(BW), row(BW), row(BW), row(BW)]
                 + [_layer_spec(a, layer) for a in params],
        out_specs=row(D_MODEL),
    )
    return pl.pallas_call(
        functools.partial(_merge_kernel, prec=prec),
        grid_spec=grid_spec,
        out_shape=jax.ShapeDtypeStruct((n, D_MODEL), F32),
        compiler_params=_cparams(("parallel",)),
        name="merge_p%d" % prec,
    )(jnp.asarray(fine), x, p["norm1"], oa, ob, oc, od, *params)


def _first_argmax(vals, valid, lane):
    v = jnp.where(valid, vals, NEG)
    top = jnp.max(v, axis=-1, keepdims=True)
    idx = jnp.min(jnp.where(valid & (v == top), lane, 1 << 20), axis=-1, keepdims=True)
    return top, idx


def _router_gate(lg, le):
    lane = lax.broadcasted_iota(jnp.int32, lg.shape, 1)
    gvalid = lane < N_GROUPS
    lgm = jnp.where(gvalid, lg, NEG)
    pe = jnp.exp(lgm - jnp.max(lgm, axis=-1, keepdims=True))
    pg = pe / jnp.sum(pe, axis=-1, keepdims=True)
    pg_top, g_idx = _first_argmax(pg, gvalid, lane)
    sel = (lane // EXP_PER_GROUP == g_idx) & (lane < N_EXPERTS)
    v1, i1 = _first_argmax(le, sel, lane)
    v2, i2 = _first_argmax(le, sel & (lane != i1), lane)
    e2 = jnp.exp(v2 - v1)
    den = 1.0 + e2
    return jnp.where(lane == i1, pg_top / den, jnp.where(lane == i2, pg_top * e2 / den, 0.0))


def _moe_kernel(h_ref, g_ref, rw_ref, rb_ref, sel_ref, w1_ref, w3_ref, w2_ref, y_ref, t_s, gate_s, *, prec):
    grp = pl.program_id(1)

    @pl.when(grp == 0)
    def _():
        x = h_ref[...]
        t = _rms(x, g_ref[0])
        t_s[...] = t.astype(t_s.dtype)
        logits = _wdot(t, rw_ref, (), max(prec, 1)) + rb_ref[0]
        gate_s[...] = _router_gate(logits[:, 0:128], logits[:, 128:256])
        y_ref[...] = x

    t = t_s[...]
    gcols = _seg_mean(gate_s[...], sel_ref[grp], prec)
    hs = []
    for j in range(EXP_PER_GROUP):
        gj = gcols[:, 128 * j:128 * (j + 1)]
        hj = _silu(_wdot(t, w1_ref, (j,), prec)) * _wdot(t, w3_ref, (j,), prec)
        hs.append(hj * jnp.concatenate([gj, gj], axis=1))
    acc = _wdot(hs[0], w2_ref, (0,), prec)
    for j in range(1, EXP_PER_GROUP):
        acc = acc + _wdot(hs[j], w2_ref, (j,), prec)
    y_ref[...] += acc


def _moe(h, p, tm, prec):
    n = h.shape[0]
    layer = p["idx"]
    wspec = lambda a: pl.BlockSpec((1, a.shape[1], EXP_PER_GROUP) + a.shape[3:],
                                   lambda i, g: (layer, 0, g, 0, 0))
    sel = np.zeros((N_GROUPS, 128, EXP_PER_GROUP * 128), np.float32)
    for g in range(N_GROUPS):
        for j in range(EXP_PER_GROUP):
            sel[g, EXP_PER_GROUP * g + j, 128 * j:128 * (j + 1)] = 1.0
    sel = jnp.asarray(sel)
    return pl.pallas_call(
        functools.partial(_moe_kernel, prec=prec),
        grid=(n // tm, N_GROUPS),
        in_specs=[pl.BlockSpec((tm, D_MODEL), lambda i, g: (i, 0)), _layer_spec(p["norm2"], layer),
                  _layer_spec(p["rw"], layer), _layer_spec(p["rb"], layer), _whole_spec(sel),
                  wspec(p["ex_w1"]), wspec(p["ex_w3"]), wspec(p["ex_w2"])],
        out_specs=pl.BlockSpec((tm, D_MODEL), lambda i, g: (i, 0)),
        out_shape=jax.ShapeDtypeStruct((n, D_MODEL), F32),
        scratch_shapes=[pltpu.VMEM((tm, D_MODEL), F32 if prec == 2 else BF16), pltpu.VMEM((tm, 128), F32)],
        compiler_params=_cparams(("parallel", "arbitrary")),
        name="moe_p%d" % prec,
    )(h, p["norm2"], p["rw"], p["rb"], sel, p["ex_w1"], p["ex_w3"], p["ex_w2"])


def _sublane_total(x):
    for shift in (4, 2, 1):
        x = x + pltpu.roll(x, shift, 0)
    return x


def _lane_all(x, op):
    return jnp.broadcast_to(op(x, axis=-1, keepdims=True), x.shape)


def _dec_attn_kernel(pt_ref, q_ref, kn_ref, vn_ref, lam_ref, *rest, pp, lam_init):
    del pt_ref
    kp, vp = rest[:pp], rest[pp:2 * pp]
    o_ref, s_s, m_s, l_s, acc_s = rest[2 * pp:]
    step = pl.program_id(1)

    @pl.when(step == 0)
    def _():
        m_s[...] = jnp.full_like(m_s, NEG)
        l_s[...] = jnp.zeros_like(l_s)
        acc_s[...] = jnp.zeros_like(acc_s)

    def map_scores(t, m):
        part = t[4 * m] + t[4 * m + 1] + t[4 * m + 2] + t[4 * m + 3]
        return _sublane_total(part)

    tops = [None] * (2 * HEADS)
    for j in range(pp):
        for h in range(HEADS):
            t = kp[j][0, 0, h] * q_ref[0, h]
            for m in range(2):
                g = 2 * h + m
                sc = map_scores(t, m)
                s_s[g, j] = sc
                tops[g] = sc if tops[g] is None else jnp.maximum(tops[g], sc)
    for h in range(HEADS):
        m_new, accs, sums = [], [], []
        for m in range(2):
            g = 2 * h + m
            m_old = m_s[g]
            m_new.append(jnp.maximum(m_old, _lane_all(tops[g], jnp.max)))
            alpha = jnp.exp(m_old - m_new[m])
            accs.append(acc_s[g] * alpha)
            sums.append(None)
            l_s[g] = l_s[g] * alpha
            m_s[g] = m_new[m]
        for j in range(pp):
            vt = vp[j][0, 0, h]
            for m in range(2):
                pr = jnp.exp(s_s[2 * h + m, j] - m_new[m])
                accs[m] = accs[m] + vt * pr
                sums[m] = pr if sums[m] is None else sums[m] + pr
        for m in range(2):
            g = 2 * h + m
            acc_s[g] = accs[m]
            l_s[g] = l_s[g] + _lane_all(sums[m], jnp.sum)

    @pl.when(step == pl.num_programs(1) - 1)
    def _():
        lam = _lam_scalar(lam_ref[0], lam_init)
        for h in range(HEADS):
            t = q_ref[0, h] * kn_ref[0, h]
            maps = []
            for m in range(2):
                g = 2 * h + m
                sc = map_scores(t, m)
                m_o = m_s[g][:, 0:1]
                m_n = jnp.maximum(m_o, sc)
                al = jnp.exp(m_o - m_n)
                pn = jnp.exp(sc - m_n)
                num = al * jnp.sum(acc_s[g], axis=-1, keepdims=True) + pn * vn_ref[0, h]
                maps.append(num / (al * l_s[g][:, 0:1] + pn))
            o_ref[0, h] = maps[0] - lam * maps[1]


def _dec_attn(qa, ka, va, cache_k, cache_v, page_table, p):
    db = qa.shape[0]
    layer = p["layer"]
    n_pages = page_table.shape[1]
    n_phys = cache_k.shape[1]
    pp = math.gcd(PAGES_PER_STEP, n_pages)
    tiles = lambda c: c.transpose(0, 1, 3, 4, 2).reshape(c.shape[0], n_phys, HEADS, 8, 8, PAGE)
    ck, cv = tiles(cache_k), tiles(cache_v)
    page = lambda j: pl.BlockSpec((1, 1, HEADS, 8, 8, PAGE),
                                  lambda b, s, pt: (layer, pt[b, s * pp + j], 0, 0, 0, 0))
    col = pl.BlockSpec((1, HEADS, 8, 8, 1), lambda b, s, pt: (b, 0, 0, 0, 0))
    lam = p["lam"]
    grid_spec = pltpu.PrefetchScalarGridSpec(
        num_scalar_prefetch=1,
        grid=(db, n_pages // pp),
        in_specs=[col, col, col, pl.BlockSpec((1,) + lam.shape[1:], lambda b, s, pt: (p["idx"], 0, 0))]
                 + [page(j) for j in range(pp)] + [page(j) for j in range(pp)],
        out_specs=col,
        scratch_shapes=[pltpu.VMEM((2 * HEADS, pp, 8, PAGE), F32), pltpu.VMEM((2 * HEADS, 8, PAGE), F32),
                        pltpu.VMEM((2 * HEADS, 8, PAGE), F32), pltpu.VMEM((2 * HEADS, 8, 8, PAGE), F32)],
    )
    cols = lambda x: x.reshape(db, HEADS, 8, 8, 1)
    out = pl.pallas_call(
        functools.partial(_dec_attn_kernel, pp=pp, lam_init=_lam_init(layer)),
        grid_spec=grid_spec,
        out_shape=jax.ShapeDtypeStruct((db, HEADS, 8, 8, 1), F32),
        compiler_params=_cparams(("parallel", "arbitrary")),
        name="attn_sample",
    )(page_table, cols(qa), cols(ka), cols(va), lam, *([ck] * pp), *([cv] * pp))
    return out.reshape(db * HEADS, 64).T


def _step_kernel(oa_ref, an_ref,
                 bq1_ref, bq2_ref, bk1_ref, bk2_ref, bv_ref, bg_ref, s0_ref, cos_ref, sin_ref, gam_ref, rn_ref,
                 dq_ref, dk_ref, dv_ref, dr_ref, la_ref, g0_ref, gn_ref,
                 cx_ref, cg_ref, buf_ref, h0_ref, cw_ref, cb_ref, wa_ref, wx_ref, ba_ref, bx_ref, lam_ref,
                 oa_o, ob_o, od_o, oc_o, s_o, g_o, h_o, *, lam_init):
    x = oa_ref[...]
    oa_o[...] = x * lax.rsqrt(jnp.mean(x * x, axis=0, keepdims=True) + EPS) * an_ref[0] * (1.0 - lam_init)

    cos, sin, gam = cos_ref[...], sin_ref[...], gam_ref[...]
    rot = lambda x1, x2: jnp.concatenate([x1 * cos - x2 * sin, x1 * sin + x2 * cos], axis=0)
    qb = rot(bq1_ref[...], bq2_ref[...])
    kb = rot(bk1_ref[...], bk2_ref[...]) * 0.125
    v = bv_ref[...]
    o = jnp.sum(qb * kb, axis=0, keepdims=True) * v
    st = jnp.zeros_like(v)
    for k in range(64):
        s0k = s0_ref[k]
        st = st + s0k * qb[k:k + 1]
        s_o[k] = gam * s0k + kb[k:k + 1] * v
    o = o + gam * st
    mu = jnp.mean(o, axis=0, keepdims=True)
    var = jnp.mean((o - mu) * (o - mu), axis=0, keepdims=True)
    ob_o[...] = (o - mu) * lax.rsqrt(var + EPS) * rn_ref[0] * _silu(bg_ref[...])

    qd = dq_ref[...] * (DQA ** -0.5)
    kd = dk_ref[...]
    vd = dv_ref[...]
    dec = jnp.exp(la_ref[...])
    o = jnp.sum(qd * kd, axis=0, keepdims=True) * vd
    qe = qd * dec
    for k in range(32):
        g0k = g0_ref[k]
        o = o + g0k * qe[k:k + 1]
        g_o[k] = dec[k:k + 1] * g0k + kd[k:k + 1] * vd
    od_o[...] = o * lax.rsqrt(jnp.mean(o * o, axis=0, keepdims=True) + EPS) * gn_ref[0] * _silu(dr_ref[...])

    cx = cx_ref[...]
    y = cb_ref[0] + cx * cw_ref[0, CONV_W - 1:CONV_W, :]
    for i in range(CONV_W - 1):
        y = y + buf_ref[i] * cw_ref[0, i:i + 1, :]
    a, u = _lru_gates(y, wa_ref, wx_ref, ba_ref[0], bx_ref[0], _softplus(-lam_ref[0]), 2)
    h = a * h0_ref[...] + u
    h_o[...] = h
    oc_o[...] = h * _gelu_tanh(cg_ref[...])


def _to_fm(x, db, width):
    return x.reshape(db, HEADS, width).transpose(2, 0, 1).reshape(width, db * HEADS)


def _from_fm(x, db, width):
    return x.reshape(width, db, HEADS).transpose(1, 2, 0).reshape(db, HEADS * width)


def _halves_fm(x, db):
    x4 = x.reshape(db, HEADS, 2, 32).transpose(2, 3, 0, 1).reshape(2, 32, db * HEADS)
    return x4[0], x4[1]


def _sample_step(oa_fm, sb, sc, sd, ret0, gla0, conv_buf, h0, p, pos):
    db = sb.shape[0]
    n = db * HEADS
    layer = p["idx"]
    cos, sin = _rope_cos_sin(jnp.asarray([pos]))
    gam = jnp.tile(jnp.asarray(_ret_gammas(), F32), db)[None, :]
    bq1, bq2 = _halves_fm(sb[:, 0:BW], db)
    bk1, bk2 = _halves_fm(sb[:, BW:2 * BW], db)
    args = [
        oa_fm, p["an_c"],
        bq1, bq2, bk1, bk2, _to_fm(sb[:, 2 * BW:3 * BW], db, 64), _to_fm(sb[:, 3 * BW:4 * BW], db, 64),
        ret0.transpose(2, 3, 0, 1).reshape(64, 64, n), cos.reshape(32, 1), sin.reshape(32, 1), gam, p["rn_c"],
        _to_fm(sd[:, 0:128], db, 32), _to_fm(sd[:, 128:256], db, 32), _to_fm(sd[:, 256:512], db, 64),
        _to_fm(sd[:, 512:768], db, 64), _to_fm(sd[:, 768:896], db, 32),
        gla0.transpose(2, 3, 0, 1).reshape(32, 64, n), p["gn_c"],
        sc[:, 0:BW], sc[:, BW:2 * BW], conv_buf.transpose(1, 0, 2), h0,
        p["conv_w"], p["conv_b"], p["lru_wa_bd"], p["lru_wx_bd"], p["lru_ba"], p["lru_bx"], p["lru_lam"],
    ]
    layered = {1, 12, 19, 24, 25, 26, 27, 28, 29, 30}
    in_specs = [_layer_spec(a, layer) if i in layered else _whole_spec(a) for i, a in enumerate(args)]
    outs = [jax.ShapeDtypeStruct((64, n), F32)] * 3 + [jax.ShapeDtypeStruct((db, BW), F32),
            jax.ShapeDtypeStruct((64, 64, n), F32), jax.ShapeDtypeStruct((32, 64, n), F32),
            jax.ShapeDtypeStruct((db, BW), F32)]
    oa, ob, od, oc, s_new, g_new, h_new = pl.pallas_call(
        functools.partial(_step_kernel, lam_init=_lam_init(p["layer"])),
        grid=(1,),
        in_specs=in_specs,
        out_specs=[pl.BlockSpec(o.shape, lambda i, nd=len(o.shape): (0,) * nd) for o in outs],
        out_shape=outs,
        compiler_params=_cparams(("arbitrary",)),
        name="step_sample",
    )(*args)
    ret_new = s_new.reshape(64, 64, db, HEADS).transpose(2, 3, 0, 1)
    gla_new = g_new.reshape(32, 64, db, HEADS).transpose(2, 3, 0, 1)
    return _from_fm(oa, db, 64), _from_fm(ob, db, 64), oc, _from_fm(od, db, 64), ret_new, h_new, gla_new


def _block_diag(w):
    d, h, n, _ = w.shape
    eye = jnp.eye(h, dtype=w.dtype)
    return jnp.einsum("dhij,hg->dhigj", w, eye).reshape(d, h * n, h * n)


def _seg_matrix(seg):
    idx = np.arange(BW)
    return jnp.asarray((idx[:, None] // seg == idx[None, :] // seg) / float(seg), F32)


def _parts(w, prec, with_cast=False):
    if prec == 2:
        return w[:, None]
    if prec == 0:
        return w.astype(BF16)[:, None]
    return jnp.stack(_split_bits(w) + ((w.astype(BF16),) if with_cast else ()), axis=1)


def _prepare(raw, layers, prec, moe_prec, layer):
    sl = lambda a: a[layers]
    (norm1, w_in, qn, kn, lam, an, rn, conv_w, conv_b, lru_wa, lru_ba, lru_wx, lru_bx, lru_lam, gla_w2, gla_b,
     gn, w_branch, w_merge, b_merge, w_out, norm2, rg_w, rg_b, re_w, re_b, ex_w1, ex_w3, ex_w2) = raw
    nl = sl(norm1).shape[0]
    r3 = lambda a: sl(a).reshape(nl, 1, -1)
    tile3 = lambda a, k: jnp.tile(sl(a), (1, k)).reshape(nl, 1, -1)
    pad_cols = lambda a, w: jnp.pad(a, ((0, 0), (0, 0), (0, w - a.shape[-1])))
    w_in = sl(w_in)
    c = np.cumsum([0, 768, 1024, 512, 784])
    rw = jnp.concatenate([pad_cols(sl(rg_w), 128), pad_cols(sl(re_w), 128)], axis=-1)
    return dict(
        idx=layer if nl > 1 else 0, layer=layer,
        norm1=r3(norm1), qn_t=tile3(qn, 8), kn_t=tile3(kn, 8), lam=sl(lam), an_t=tile3(an, 4), rn_t=tile3(rn, 4),
        gn_t=tile3(gn, 4), an_c=sl(an).reshape(nl, 64, 1), rn_c=sl(rn).reshape(nl, 64, 1),
        gn_c=sl(gn).reshape(nl, 64, 1), conv_w=sl(conv_w), conv_b=r3(conv_b), lru_ba=r3(lru_ba),
        lru_bx=r3(lru_bx), lru_lam=r3(lru_lam), gla_b=r3(gla_b), b_merge=r3(b_merge), norm2=r3(norm2),
        rb=jnp.concatenate([pad_cols(r3(rg_b), 128), pad_cols(r3(re_b), 128)], axis=-1),
        m32=_seg_matrix(32), m64=_seg_matrix(64),
        wa=_parts(w_in[:, :, c[0]:c[1]], prec), wb=_parts(w_in[:, :, c[1]:c[2]], prec),
        wc=_parts(w_in[:, :, c[2]:c[3]], prec), wd=_parts(pad_cols(w_in[:, :, c[3]:c[4]], W_D), prec),
        gla_w2p=_parts(jnp.pad(sl(gla_w2), ((0, 0), (0, 128 - GLA_RANK), (0, 0))), prec),
        lru_wa_bd=_parts(_block_diag(sl(lru_wa)), prec), lru_wx_bd=_parts(_block_diag(sl(lru_wx)), prec),
        w_merge=_parts(sl(w_merge), prec, True), w_branch=_parts(sl(w_branch), prec, True),
        w_out=_parts(sl(w_out), prec, True),
        rw=_parts(rw, max(moe_prec, 1)),
        ex_w1=_parts(sl(ex_w1), moe_prec), ex_w3=_parts(sl(ex_w3), moe_prec), ex_w2=_parts(sl(ex_w2), moe_prec),
    )


def _diag_blocks(s, rows, cols):
    b = s.shape[0]
    s5 = s.reshape(b, HEADS, rows, HEADS, cols)
    return jnp.stack([s5[:, h, :, h, :] for h in range(HEADS)], axis=1)


def _prompt_tile(n, prec):
    for tm in ((256, 128) if prec else (512, 256, 128)):
        if n % tm == 0:
            return tm
    raise ValueError(n)


def kernel(x_prompt, x_sample, cache_k_a, cache_v_a, state_ret, state_lru_h, state_lru_conv, state_gla,
           page_table, meta, norm1, w_in, qn, kn, lam, an, rn, conv_w, conv_b, lru_wa, lru_ba, lru_wx,
           lru_bx, lru_lam, gla_w2, gla_b, gn, w_branch, w_merge, b_merge, w_out, norm2, rg_w, rg_b,
           re_w, re_b, ex_w1, ex_w3, ex_w2):
    bp, seq, _ = x_prompt.shape
    db = x_sample.shape[0]
    depth = w_in.shape[0]
    assert seq % BLK == 0 and x_sample.shape[1] == 1
    lp = BLK + seq
    past = page_table.shape[1] * PAGE
    raw = (norm1, w_in, qn, kn, lam, an, rn, conv_w, conv_b, lru_wa, lru_ba, lru_wx, lru_bx, lru_lam, gla_w2,
           gla_b, gn, w_branch, w_merge, b_merge, w_out, norm2, rg_w, rg_b, re_w, re_b, ex_w1, ex_w3, ex_w2)
    head = jnp.concatenate([jnp.zeros((PADL, D_MODEL), F32), meta.astype(F32)], axis=0)
    xp = jnp.concatenate([jnp.broadcast_to(head[None], (bp, BLK, D_MODEL)), x_prompt], axis=1)
    xp = xp.reshape(bp * lp, D_MODEL)
    xs = x_sample.reshape(db, D_MODEL)
    new_p, new_s = [], []
    for layer in range(depth):
        prec = 1 if layer == 0 else 0
        tm = _prompt_tile(bp * lp, prec)
        p = _prepare(raw, slice(layer, layer + 1), prec, 0, layer)
        ka, va, sb, sc, sd, qp, kp, vp = _in_proj(xp, p, tm, prec, lp)
        oa = _attn_prompt(qp, kp, vp, p, bp, lp, prec)
        ob, s_ret = _ret_prompt(sb, p, bp, lp, prec)
        oc, h_last = _lru_prompt(sc, p, bp, lp, prec)
        od, s_gla = _gla_prompt(sd, p, bp, lp, prec)
        tail_rows = [b * lp + lp - 1 - r for b in range(bp) for r in range(CONV_W - 1)]
        hres = _merge(xp, oa, ob, oc, od, p, tm, prec, tail_rows)
        xp = _moe(hres, p, _prompt_tile(bp * lp, 0), 0)
        new_p.append((ka.reshape(bp, lp, HEADS, 64)[:, PADL:], va.reshape(bp, lp, HEADS, 64)[:, PADL:],
                      _diag_blocks(s_ret, 64, 64), h_last,
                      sc.reshape(bp, lp, W_C)[:, lp - (CONV_W - 1):, 0:BW],
                      _diag_blocks(s_gla, 64, 32).transpose(0, 1, 3, 2)))
        ps = _prepare(raw, slice(None), 2, 2, layer)
        ka, va, sb, sc, sd, qa = _in_proj(xs, ps, db, 2)
        oa_fm = _dec_attn(qa, ka, va, cache_k_a, cache_v_a, page_table, ps)
        oa, ob, oc, od, ret_new, h_new, gla_new = _sample_step(
            oa_fm, sb, sc, sd, state_ret[layer], state_gla[layer], state_lru_conv[layer], state_lru_h[layer],
            ps, past)
        hres = _merge(xs, oa, ob, oc, od, ps, db, 2)
        xs = _moe(hres, ps, db, 2)
        conv_new = jnp.concatenate([state_lru_conv[layer][:, 1:], sc[:, None, 0:BW]], axis=1)
        new_s.append((ka.reshape(db, 1, HEADS, 64), va.reshape(db, 1, HEADS, 64), ret_new, h_new, conv_new,
                      gla_new))
    stk = lambda states, i: jnp.stack([s[i] for s in states])
    y_prompt = xp.reshape(bp, lp, D_MODEL)[:, BLK:]
    return ((y_prompt, xs.reshape(db, 1, D_MODEL))
            + tuple(stk(new_p, i) for i in range(6)) + tuple(stk(new_s, i) for i in range(6)))
```

```python
import functools
import math

import numpy as np
import jax
import jax.numpy as jnp
from jax import lax
from jax.experimental import pallas as pl
from jax.experimental.pallas import tpu as pltpu

F32 = jnp.float32
BF16 = jnp.bfloat16
EPS = 1e-6
D_MODEL = 1024
N_META = 16
PAGE = 128
BLK = 128
PADL = BLK - N_META
HEADS = 4
BW = 256
DQA = 32
ROPE_BASE = 10000.0
CONV_W = 4
LRU_C = 8.0
GLA_RANK = 16
GLA_TAU = 16.0
GLA_CHUNK = 16
N_GROUPS = 4
EXP_PER_GROUP = 4
N_EXPERTS = 16
D_EXPERT = 256
W_A, W_B, W_C, W_D = 768, 1024, 512, 896
QK_PACK = 8 * 128
V_PACK = HEADS * 128
VMEM_LIMIT = 56 * 1024 * 1024
NEG = -1e30
PAGES_PER_STEP = 16
ATTN_KEY_STEP = 512
PAD_LANE = 96
LOG2E = math.log2(math.e)


def _cparams(sem):
    return pltpu.CompilerParams(dimension_semantics=sem, vmem_limit_bytes=VMEM_LIMIT)


def _layer_spec(arr, layer):
    nd = arr.ndim
    return pl.BlockSpec((1,) + arr.shape[1:], lambda *_: (layer,) + (0,) * (nd - 1),
                        pipeline_mode=pl.Buffered(1))


def _whole_spec(arr):
    nd = arr.ndim
    return pl.BlockSpec(arr.shape, lambda *_: (0,) * nd, pipeline_mode=pl.Buffered(1))


def _bf16_prefix(x):
    bits = lax.bitcast_convert_type(x, jnp.uint32)
    bits = (bits + jnp.uint32(0x7FFF) + ((bits >> 16) & jnp.uint32(1))) & jnp.uint32(0xFFFF0000)
    return lax.bitcast_convert_type(bits, F32)


def _split_bits(x):
    hi = _bf16_prefix(x)
    return hi.astype(BF16), (x - hi).astype(BF16)


def _split(x):
    hi = x.astype(BF16)
    return hi, (x - hi.astype(F32)).astype(BF16)


def _split3(x):
    hi = x.astype(BF16)
    r = x - hi.astype(F32)
    mid = r.astype(BF16)
    return hi, mid, (r - mid.astype(F32)).astype(BF16)


def _bdot(a, b, dims=None):
    if dims is None:
        return jnp.dot(a, b, preferred_element_type=F32)
    return lax.dot_general(a, b, (dims, ((), ())), preferred_element_type=F32)


def _branch_dtype(prec):
    return BF16 if prec == 0 else F32


def _operand(a, prec):
    if prec == 2:
        return a
    return _split(a) if prec == 1 else a.astype(BF16)


def _wdot(a, w_ref, idx, prec):
    get = lambda part: w_ref[(0, part) + idx]
    if prec == 2:
        return jnp.dot(a, get(0), precision=lax.Precision.HIGHEST, preferred_element_type=F32)
    if prec == 1:
        a_hi, a_lo = a if isinstance(a, tuple) else _split(a)
        return _bdot(a_hi, get(0)) + _bdot(a_lo, get(0)) + _bdot(a_hi, get(1))
    assert w_ref.shape[1] in (1, 3), "level 0 needs the plain bf16 cast: parts [cast] or [hi, lo, cast]"
    return _bdot(a.astype(BF16), get(w_ref.shape[1] - 1))


def _adot(a, b, prec, dims=None):
    if prec == 1:
        a_hi, a_lo = _split(a)
        b_hi, b_lo = _split(b)
        return _bdot(a_hi, b_hi, dims) + _bdot(a_lo, b_hi, dims) + _bdot(a_hi, b_lo, dims)
    return _bdot(a.astype(BF16), b.astype(BF16), dims)


NT = ((1,), (1,))
TN = ((0,), (0,))


def _seg_mean(x, m, prec=0):
    if prec == 2:
        return jnp.dot(x, m, precision=lax.Precision.HIGHEST, preferred_element_type=F32)
    hi, lo = _split(x)
    mb = m.astype(BF16)
    return _bdot(hi, mb) + _bdot(lo, mb)


def _rms(x, g):
    return x * lax.rsqrt(jnp.mean(x * x, axis=-1, keepdims=True) + EPS) * g


def _sigmoid(x):
    return 1.0 / (1.0 + jnp.exp(-x))


def _silu(x):
    return x * _sigmoid(x)


def _log_sigmoid(x):
    return jnp.minimum(x, 0.0) - jnp.log(1.0 + jnp.exp(-jnp.abs(x)))


def _softplus(x):
    return jnp.maximum(x, 0.0) + jnp.log(1.0 + jnp.exp(-jnp.abs(x)))


def _gelu_tanh(x):
    return 0.5 * x * (1.0 + jnp.tanh(math.sqrt(2.0 / math.pi) * (x + 0.044715 * (x * x * x))))


def _lane_mask(width, lo, hi):
    lane = lax.broadcasted_iota(jnp.int32, (1, width), 1)
    return (lane >= lo) & (lane < hi)


def _lam_scalar(lv, lam_init):
    s01 = jnp.sum(lv[0:1] * lv[1:2], axis=-1, keepdims=True)
    s23 = jnp.sum(lv[2:3] * lv[3:4], axis=-1, keepdims=True)
    return jnp.exp(s01) - jnp.exp(s23) + lam_init


def _lam_init(layer):
    return 0.8 - 0.6 * math.exp(-0.3 * layer)


_SPANS_A = ((0, 256, False), (256, 768, True))
_SPANS_B = ((0, 256, False), (256, 768, True), (768, 1024, False))
_SPANS_C = ((0, 256, True), (256, 512, False))
_SPANS_D = ((0, 128, False), (128, 512, True), (512, 768, False), (768, 896, True))


def _in_proj_kernel(fine_ref, x_ref, g_ref, wa_ref, wb_ref, wc_ref, wd_ref, qn_ref, kn_ref, m32_ref, w2_ref,
                    gb_ref, pq_ref, pk_ref, pv_ref, ka_ref, va_ref, sb_ref, sc_ref, sd_ref, *rest, prec, seq_len):
    q_out, (a_s, d_s) = rest[:-2], rest[-2:]
    xn = _rms(x_ref[...], g_ref[0])
    slabs = ((wa_ref, a_s, _SPANS_A), (wb_ref, sb_ref, _SPANS_B), (wc_ref, sc_ref, _SPANS_C), (wd_ref, d_s, _SPANS_D))

    def project(mixed):
        x_full = _operand(xn, prec)
        x_cast = xn.astype(BF16) if mixed else None
        for w_ref, dst, spans in slabs:
            if not mixed:
                dst[...] = _wdot(x_full, w_ref, (), prec)
                continue
            for lo, hi, carried in spans:
                cols = (slice(None), slice(lo, hi))
                dst[:, lo:hi] = _wdot(x_full, w_ref, cols, prec) if carried else _wdot(x_cast, w_ref, cols, 0)

    if prec == 1:
        fine = fine_ref[pl.program_id(0)]
        pl.when(fine != 0)(lambda: project(False))
        pl.when(fine == 0)(lambda: project(True))
    else:
        project(False)
    a, d = a_s[...], d_s[...]
    q, k, v = a[:, 0:BW], a[:, BW:2 * BW], a[:, 2 * BW:3 * BW]
    m32 = m32_ref[...]
    qa = q * lax.rsqrt(_seg_mean(q * q, m32, prec) + EPS) * qn_ref[0] * (DQA ** -0.5)
    ka = k * lax.rsqrt(_seg_mean(k * k, m32, prec) + EPS) * kn_ref[0]
    ka_ref[...] = ka
    va_ref[...] = v
    sd_ref[:, 0:768] = d[:, 0:768]
    z = _wdot(d[:, 768:896], w2_ref, (), prec) + gb_ref[0]
    sd_ref[:, 768:896] = _log_sigmoid(z) * (1.0 / GLA_TAU)
    if len(q_out) == 1:
        q_out[0][...] = qa
    else:
        def place(src, p_ref):
            hi, lo = _split(src)
            return _bdot(hi, p_ref[0]) + _bdot(lo, p_ref[1])

        tm = x_ref.shape[0]
        row = (pl.program_id(0) * tm + lax.broadcasted_iota(jnp.int32, (tm, 1), 0)).astype(F32)
        in_seq = row - jnp.floor((row + 0.5) * (1.0 / seq_len)) * seq_len
        is_pad = in_seq < PADL
        flag = (lax.broadcasted_iota(jnp.int32, (1, QK_PACK), 1) % 128 == PAD_LANE).astype(F32)
        qp_ref, kp_ref, vp_ref = q_out
        qp_ref[...] = (place(qa * LOG2E, pq_ref) + jnp.where(is_pad, 0.0, 1.0) * flag).astype(BF16)
        kp_ref[...] = (place(ka, pk_ref) + jnp.where(is_pad, NEG, 0.0) * flag).astype(BF16)
        vp_ref[...] = place(v, pv_ref).astype(BF16)


def _pack_matrices():
    pq = np.zeros((2, BW, QK_PACK), np.float32)
    pk = np.zeros((2, BW, QK_PACK), np.float32)
    pv = np.zeros((2, BW, V_PACK), np.float32)
    for g in range(8):
        for d in range(DQA):
            r, c = DQA * g + d, 128 * g + d
            pq[0, r, c] = pq[0, r, c + 64] = 1.0
            pq[1, r, c + 32] = 1.0
            pk[0, r, c] = pk[0, r, c + 32] = 1.0
            pk[1, r, c + 64] = 1.0
    for h in range(HEADS):
        for d in range(64):
            pv[0, 64 * h + d, 128 * h + d] = 1.0
            pv[1, 64 * h + d, 128 * h + 64 + d] = 1.0
    return [jnp.asarray(a, BF16) for a in (pq, pk, pv)]


def _tile_flags(n_tiles, tm, rows):
    if rows is None:
        return jnp.ones((n_tiles,), jnp.int32)
    fine = np.zeros((n_tiles,), np.int32)
    fine[np.asarray(rows) // tm] = 1
    return jnp.asarray(fine)


def _in_proj(x, p, tm, prec, seq_len=None, fine_rows=None):
    n = x.shape[0]
    packed = seq_len is not None
    layer = p["idx"]
    outs = [jax.ShapeDtypeStruct((n, BW), F32)] * 2 + [jax.ShapeDtypeStruct((n, W_B), F32),
            jax.ShapeDtypeStruct((n, W_C), F32), jax.ShapeDtypeStruct((n, W_D), F32)]
    if packed:
        outs += [jax.ShapeDtypeStruct((n, QK_PACK), BF16)] * 2 + [jax.ShapeDtypeStruct((n, V_PACK), BF16)]
    else:
        outs += [jax.ShapeDtypeStruct((n, BW), F32)]
    row = lambda w: pl.BlockSpec((tm, w), lambda i, f: (i, 0))
    params = [p["norm1"], p["wa"], p["wb"], p["wc"], p["wd"], p["qn_t"], p["kn_t"]]
    tail = [p["gla_w2p"], p["gla_b"]]
    consts = _pack_matrices()
    grid_spec = pltpu.PrefetchScalarGridSpec(
        num_scalar_prefetch=1,
        grid=(n // tm,),
        in_specs=([row(D_MODEL)] + [_layer_spec(a, layer) for a in params] + [_whole_spec(p["m32"])]
                  + [_layer_spec(a, layer) for a in tail] + [_whole_spec(a) for a in consts]),
        out_specs=[row(o.shape[1]) for o in outs],
        scratch_shapes=[pltpu.VMEM((tm, W_A), F32), pltpu.VMEM((tm, W_D), F32)],
    )
    return pl.pallas_call(
        functools.partial(_in_proj_kernel, prec=prec, seq_len=seq_len),
        grid_spec=grid_spec,
        out_shape=outs,
        compiler_params=_cparams(("parallel",)),
        name="in_proj_p%d" % prec,
    )(_tile_flags(n // tm, tm, fine_rows), x, *params, p["m32"], *tail, *consts)


def _attn_body(q_ref, k_ref, v_ref, lam, an_ref, o_ref, s_buf, w_buf, i, nk, *, lam_init, tq, prec):
    n_free = max(nk - ATTN_KEY_STEP, 0)
    pos_q = i * tq + lax.broadcasted_iota(jnp.int32, (tq, 1), 0)
    causal = n_free + lax.broadcasted_iota(jnp.int32, (1, nk - n_free), 1) <= pos_q
    lane = lax.broadcasted_iota(jnp.int32, (1, 128), 1)
    for g in range(2 * HEADS):
        s_buf[g, :, 0:nk] = _bdot(q_ref[:, 128 * g:128 * (g + 1)], k_ref[0:nk, 128 * g:128 * (g + 1)], NT)
    outs = []
    for h in range(HEADS):
        scale = []
        for m in range(2):
            g = 2 * h + m
            tail = jnp.where(causal, s_buf[g, :, n_free:nk], NEG)
            mx = jnp.max(tail, axis=-1, keepdims=True)
            if n_free:
                mx = jnp.maximum(mx, jnp.max(s_buf[g, :, 0:n_free], axis=-1, keepdims=True))
                head_p = jnp.exp2(s_buf[g, :, 0:n_free] - mx)
                s_buf[g, :, 0:n_free] = head_p
            tail_p = jnp.exp2(tail - mx)
            s_buf[g, :, n_free:nk] = tail_p
            total = jnp.sum(tail_p, axis=-1, keepdims=True)
            if n_free:
                total = total + jnp.sum(head_p, axis=-1, keepdims=True)
            scale.append(1.0 / total)
        w = s_buf[2 * h, :, 0:nk] * scale[0] - s_buf[2 * h + 1, :, 0:nk] * (lam * scale[1])
        vt = v_ref[0:nk, 128 * h:128 * (h + 1)]
        if prec == 1:
            w_hi, w_lo = _split(w)
            w_buf[h, 0:tq, 0:nk] = w_hi
            w_buf[h, tq:2 * tq, 0:nk] = w_lo
            o2 = _bdot(w_buf[h, :, 0:nk], vt)
            o = o2[0:tq] + o2[tq:2 * tq]
        else:
            w_buf[h, 0:tq, 0:nk] = w.astype(BF16)
            o = _bdot(w_buf[h, 0:tq, 0:nk], vt)
        o = o + pltpu.roll(o, 64, 1)
        outs.append(o * lax.rsqrt(jnp.mean(o * o, axis=-1, keepdims=True) + EPS))
    for t in range(2):
        pair = jnp.where(lane < 64, outs[2 * t], outs[2 * t + 1])
        o_ref[:, 128 * t:128 * (t + 1)] = (pair * an_ref[0][:, 128 * t:128 * (t + 1)]
                                           * (1.0 - lam_init)).astype(o_ref.dtype)


def _attn_kernel(q_ref, k_ref, v_ref, lam_ref, an_ref, o_ref, s_buf, w_buf, *, lam_init, tq, key_ranges, prec):
    i = pl.program_id(1)
    lam = _lam_scalar(lam_ref[0], lam_init)
    need = (i + 1) * tq
    lo = 0
    for nk in key_ranges:
        @pl.when((need > lo) & (need <= nk))
        def _(nk=nk):
            _attn_body(q_ref, k_ref, v_ref, lam, an_ref, o_ref, s_buf, w_buf, i, nk, lam_init=lam_init, tq=tq,
                       prec=prec if nk == key_ranges[-1] else 0)
        lo = nk


def _attn_prompt(qp, kp, vp, p, b, lp, prec):
    nq = lp // BLK
    layer = p["idx"]
    key_ranges = tuple(sorted({min(lp, ATTN_KEY_STEP * j) for j in range(1, -(-lp // ATTN_KEY_STEP) + 1)}))
    return pl.pallas_call(
        functools.partial(_attn_kernel, lam_init=_lam_init(p["layer"]), tq=BLK, key_ranges=key_ranges,
                          prec=prec),
        grid=(b, nq),
        in_specs=[pl.BlockSpec((BLK, QK_PACK), lambda bi, i: (bi * nq + i, 0)),
                  pl.BlockSpec((lp, QK_PACK), lambda bi, i: (bi, 0)),
                  pl.BlockSpec((lp, V_PACK), lambda bi, i: (bi, 0)),
                  _layer_spec(p["lam"], layer), _layer_spec(p["an_t"], layer)],
        out_specs=pl.BlockSpec((BLK, BW), lambda bi, i: (bi * nq + i, 0)),
        out_shape=jax.ShapeDtypeStruct((b * lp, BW), _branch_dtype(prec)),
        scratch_shapes=[pltpu.VMEM((2 * HEADS, BLK, lp), F32), pltpu.VMEM((HEADS, 2 * BLK, lp), BF16)],
        compiler_params=_cparams(("parallel", "parallel")),
        name="attn_prompt_p%d" % prec,
    )(qp, kp, vp, p["lam"], p["an_t"])


def _swap_halves(x):
    w = x.shape[-1]
    lane = lax.broadcasted_iota(jnp.int32, (1, w), 1)
    return jnp.where((lane % 64) < 32, pltpu.roll(x, w - 32, 1), pltpu.roll(x, 32, 1))


def _ret_kernel(sb_ref, cos_ref, sin_ref, dmat_ref, dq_ref, dk_ref, dstate_ref, rn_ref, m64_ref,
                o_ref, s_out_ref, s_ref, *, prec):
    c = pl.program_id(1)

    @pl.when(c == 0)
    def _():
        s_ref[...] = jnp.zeros_like(s_ref)

    cos, sin = cos_ref[...], sin_ref[...]
    q, k, v = sb_ref[:, 0:BW], sb_ref[:, BW:2 * BW], sb_ref[:, 2 * BW:3 * BW]
    qr = q * cos + _swap_halves(q) * sin
    kr = (k * cos + _swap_halves(k) * sin) * 0.125
    o = _adot(qr, s_ref[...], prec) * dq_ref[...]
    upd = _adot(kr * dk_ref[...], v, prec, TN)
    masks = [_lane_mask(BW, 64 * h, 64 * (h + 1)) for h in range(HEADS)]
    scores = [_adot(jnp.where(hm, qr, 0.0), kr, prec, NT) for hm in masks]
    for h, hm in enumerate(masks):
        o = o + jnp.where(hm, _adot(scores[h] * dmat_ref[h], v, prec), 0.0)
    row_head = lax.broadcasted_iota(jnp.int32, (BW, BW), 0) // 64
    col_head = lax.broadcasted_iota(jnp.int32, (BW, BW), 1) // 64
    s_new = s_ref[...] * dstate_ref[...] + jnp.where(row_head == col_head, upd, 0.0)
    s_ref[...] = s_new
    s_out_ref[0] = s_new
    m64 = m64_ref[...]
    mu = _seg_mean(o, m64)
    var = _seg_mean((o - mu) * (o - mu), m64)
    g = sb_ref[:, 3 * BW:4 * BW]
    o_ref[...] = ((o - mu) * lax.rsqrt(var + EPS) * rn_ref[0] * _silu(g)).astype(o_ref.dtype)


def _ret_gammas():
    return 1.0 - 2.0 ** (-5.0 - np.arange(HEADS, dtype=np.float64))


def _ret_tables():
    lg = np.log(_ret_gammas())
    t = np.arange(BLK, dtype=np.float64)
    rel = t[:, None] - t[None, :]
    dmat = np.where(rel >= 0, np.exp(np.maximum(rel, 0.0)[None] * lg[:, None, None]), 0.0)
    lane_lg = np.repeat(lg, 64)[None, :]
    dq = np.exp((t[:, None] + 1.0) * lane_lg)
    dk = np.exp((BLK - 1.0 - t)[:, None] * lane_lg)
    dstate = np.exp(BLK * lane_lg)
    return [jnp.asarray(a, F32) for a in (dmat, dq, dk, dstate)]


def _rope_cos_sin(pos):
    inv = ROPE_BASE ** (-jnp.linspace(0.0, 1.0, 32))
    ang = pos.astype(F32)[:, None] * inv[None]
    return jnp.cos(ang), jnp.sin(ang)


def _ret_prompt(sb, p, b, lp, prec):
    nc = lp // BLK
    cos, sin = _rope_cos_sin(jnp.arange(lp) - PADL)
    cos_t = jnp.tile(jnp.concatenate([cos, cos], -1), (1, HEADS))
    sin_t = jnp.tile(jnp.concatenate([-sin, sin], -1), (1, HEADS))
    tables = _ret_tables()
    return pl.pallas_call(
        functools.partial(_ret_kernel, prec=prec),
        grid=(b, nc),
        in_specs=[pl.BlockSpec((BLK, W_B), lambda bi, c: (bi * nc + c, 0)),
                  pl.BlockSpec((BLK, BW), lambda bi, c: (c, 0)),
                  pl.BlockSpec((BLK, BW), lambda bi, c: (c, 0))]
                 + [_whole_spec(a) for a in tables] + [_layer_spec(p["rn_t"], p["idx"]), _whole_spec(p["m64"])],
        out_specs=[pl.BlockSpec((BLK, BW), lambda bi, c: (bi * nc + c, 0)),
                   pl.BlockSpec((1, BW, BW), lambda bi, c: (bi, 0, 0))],
        out_shape=[jax.ShapeDtypeStruct((b * lp, BW), _branch_dtype(prec)), jax.ShapeDtypeStruct((b, BW, BW), F32)],
        scratch_shapes=[pltpu.VMEM((BW, BW), F32)],
        compiler_params=_cparams(("parallel", "arbitrary")),
        name="ret_prompt_p%d" % prec,
    )(sb, cos_t, sin_t, *tables, p["rn_t"], p["m64"])


def _lru_gates(y, wa_ref, wx_ref, ba, bx, sp, prec):
    ys = _operand(y, prec)
    r = _sigmoid(_wdot(ys, wa_ref, (), prec) + ba)
    ig = _sigmoid(_wdot(ys, wx_ref, (), prec) + bx)
    log_a = -LRU_C * r * sp
    a = jnp.exp(log_a)
    u = jnp.sqrt(1.0 - jnp.exp(2.0 * log_a)) * (ig * y)
    return a, u


def _lru_kernel(sc_ref, cw_ref, cb_ref, wa_ref, wx_ref, ba_ref, bx_ref, lam_ref,
                o_ref, hl_ref, cbuf, a_s, u_s, h_s, hcar, *, nb, prec):
    t = pl.program_id(0)

    @pl.when(t == 0)
    def _():
        cbuf[...] = jnp.zeros_like(cbuf)
        hcar[...] = jnp.zeros_like(hcar)

    sp = _softplus(-lam_ref[0])
    row = t * BLK + lax.broadcasted_iota(jnp.int32, (BLK, 1), 0)
    for b in range(nb):
        x = sc_ref[b, :, 0:BW]
        cbuf[b, 8:8 + BLK, :] = x
        y = cb_ref[0] + cbuf[b, 5:5 + BLK, :] * cw_ref[0, 0:1, :]
        for i in range(1, CONV_W):
            y = y + cbuf[b, 5 + i:5 + i + BLK, :] * cw_ref[0, i:i + 1, :]
        cbuf[b, 0:8, :] = x[BLK - 8:BLK, :]
        a, u = _lru_gates(y, wa_ref, wx_ref, ba_ref[0], bx_ref[0], sp, prec)
        a_s[b] = a
        u_s[b] = jnp.where(row >= PADL, u, 0.0)

    def step(i, hs):
        out = []
        for b in range(nb):
            h = a_s[b, pl.ds(i, 1), :] * hs[b] + u_s[b, pl.ds(i, 1), :]
            h_s[b, pl.ds(i, 1), :] = h
            out.append(h)
        return tuple(out)

    hs = lax.fori_loop(0, BLK, step, tuple(hcar[b:b + 1, :] for b in range(nb)))
    for b in range(nb):
        hcar[b:b + 1, :] = hs[b]
        o_ref[b] = (h_s[b] * _gelu_tanh(sc_ref[b, :, BW:2 * BW])).astype(o_ref.dtype)
    hl_ref[...] = hcar[...]


def _lru_prompt(sc, p, b, lp, prec):
    nt = lp // BLK
    sc3 = sc.reshape(b, lp, W_C)
    params = [p["conv_w"], p["conv_b"], p["lru_wa_bd"], p["lru_wx_bd"], p["lru_ba"], p["lru_bx"], p["lru_lam"]]
    o, hl = pl.pallas_call(
        functools.partial(_lru_kernel, nb=b, prec=prec),
        grid=(nt,),
        in_specs=[pl.BlockSpec((b, BLK, W_C), lambda t: (0, t, 0))] + [_layer_spec(a, p["idx"]) for a in params],
        out_specs=[pl.BlockSpec((b, BLK, BW), lambda t: (0, t, 0)), pl.BlockSpec((b, BW), lambda t: (0, 0))],
        out_shape=[jax.ShapeDtypeStruct((b, lp, BW), _branch_dtype(prec)), jax.ShapeDtypeStruct((b, BW), F32)],
        scratch_shapes=[pltpu.VMEM((b, BLK + 8, BW), F32), pltpu.VMEM((b, BLK, BW), F32),
                        pltpu.VMEM((b, BLK, BW), F32), pltpu.VMEM((b, BLK, BW), F32), pltpu.VMEM((b, BW), F32)],
        compiler_params=_cparams(("arbitrary",)),
        name="lru_prompt_p%d" % prec,
    )(sc3, *params)
    return o.reshape(b * lp, BW), hl


def _gla_kernel(sd_ref, tri_ref, ind_ref, gn_ref, m64_ref, o_ref, st_out_ref, st_ref, o_s, upd_s, *, prec):
    c = pl.program_id(1)

    @pl.when(c == 0)
    def _():
        st_ref[...] = jnp.zeros_like(st_ref)

    q = sd_ref[:, 0:128] * (DQA ** -0.5)
    k = sd_ref[:, 128:256]
    v = sd_ref[:, 256:512]
    la = sd_ref[:, 768:896]
    tri = tri_ref[...]
    bc = sum(_bdot(tri, part) for part in _split3(la))
    ind = ind_ref[...]
    t_idx = lax.broadcasted_iota(jnp.int32, (GLA_CHUNK, 1), 0)
    row_head = lax.broadcasted_iota(jnp.int32, (BW, 128), 0) // 64
    col_head = lax.broadcasted_iota(jnp.int32, (BW, 128), 1) // 32
    bd = row_head == col_head
    n_sub = BLK // GLA_CHUNK

    def local_cumsum(s):
        r0 = s * GLA_CHUNK
        bcs = bc[r0:r0 + GLA_CHUNK]
        return bcs - bc[r0 - 1:r0] if s > 0 else bcs

    def run(out_prec):
        for s in range(n_sub):
            r0 = s * GLA_CHUNK
            bcs = local_cumsum(s)
            qs, ks, vs = q[r0:r0 + GLA_CHUNK], k[r0:r0 + GLA_CHUNK], v[r0:r0 + GLA_CHUNK]
            zs = []
            for j in range(GLA_CHUNK):
                dec = jnp.where(t_idx >= j, jnp.exp(jnp.minimum(bcs - bcs[j:j + 1], 0.0)), 0.0)
                zs.append(qs * dec * ks[j:j + 1])
            z = jnp.concatenate(zs, axis=0)
            if out_prec == 1:
                z_hi, z_lo = _split(z)
                aa = _bdot(z_hi, ind) + _bdot(z_lo, ind)
            else:
                aa = _bdot(z.astype(BF16), ind)
            o = aa[0:GLA_CHUNK] * vs[0:1]
            for j in range(1, GLA_CHUNK):
                o = o + aa[j * GLA_CHUNK:(j + 1) * GLA_CHUNK] * vs[j:j + 1]
            o_s[r0:r0 + GLA_CHUNK, :] = o
            last = bcs[GLA_CHUNK - 1:GLA_CHUNK]
            upd_s[s] = jnp.where(bd, _adot(vs, ks * jnp.exp(last - bcs), prec, TN), 0.0)
        st = st_ref[...]
        for s in range(n_sub):
            r0 = s * GLA_CHUNK
            bcs = local_cumsum(s)
            o_s[r0:r0 + GLA_CHUNK, :] += _adot(q[r0:r0 + GLA_CHUNK] * jnp.exp(bcs), st, out_prec, NT)
            st = st * jnp.exp(bcs[GLA_CHUNK - 1:GLA_CHUNK]) + upd_s[s]
        st_ref[...] = st
        st_out_ref[0] = st
        o = o_s[...]
        ms = _seg_mean(o * o, m64_ref[...])
        o_ref[...] = (o * lax.rsqrt(ms + EPS) * gn_ref[0] * _silu(sd_ref[:, 512:768])).astype(o_ref.dtype)

    if prec == 1:
        is_last = c == pl.num_programs(1) - 1
        pl.when(is_last)(lambda: run(1))
        pl.when(jnp.logical_not(is_last))(lambda: run(0))
    else:
        run(prec)


def _gla_prompt(sd, p, b, lp, prec):
    nc = lp // BLK
    t = np.arange(BLK)
    tri = jnp.asarray(t[:, None] >= t[None, :], BF16)
    ind = jnp.asarray((np.arange(128)[:, None] // 32) == (np.arange(BW)[None, :] // 64), BF16)
    return pl.pallas_call(
        functools.partial(_gla_kernel, prec=prec),
        grid=(b, nc),
        in_specs=[pl.BlockSpec((BLK, W_D), lambda bi, c: (bi * nc + c, 0)),
                  _whole_spec(tri), _whole_spec(ind), _layer_spec(p["gn_t"], p["idx"]), _whole_spec(p["m64"])],
        out_specs=[pl.BlockSpec((BLK, BW), lambda bi, c: (bi * nc + c, 0)),
                   pl.BlockSpec((1, BW, 128), lambda bi, c: (bi, 0, 0))],
        out_shape=[jax.ShapeDtypeStruct((b * lp, BW), _branch_dtype(prec)), jax.ShapeDtypeStruct((b, BW, 128), F32)],
        scratch_shapes=[pltpu.VMEM((BW, 128), F32), pltpu.VMEM((BLK, BW), F32),
                        pltpu.VMEM((BLK // GLA_CHUNK, BW, 128), F32)],
        compiler_params=_cparams(("parallel", "arbitrary")),
        name="gla_prompt_p%d" % prec,
    )(sd, tri, ind, p["gn_t"], p["m64"])


def _merge_kernel(fine_ref, x_ref, g_ref, oa_ref, ob_ref, oc_ref, od_ref, wm_ref, bm_ref, wbr_ref, wo_ref, y_ref,
                  *, prec):
    def body(level):
        x = x_ref[...]
        xs = _operand(_rms(x, g_ref[0]), level)
        acc = None
        for n, br in enumerate((oa_ref, ob_ref, oc_ref, od_ref)):
            cols = slice(n * D_MODEL, (n + 1) * D_MODEL)
            gate = _sigmoid(_wdot(xs, wm_ref, (slice(None), cols), level) + bm_ref[0, :, cols])
            term = gate * _wdot(br[...], wbr_ref, (n,), level)
            acc = term if acc is None else acc + term
        y_ref[...] = x + _wdot(acc, wo_ref, (), level)

    if prec == 1:
        fine = fine_ref[pl.program_id(0)]
        pl.when(fine != 0)(lambda: body(1))
        pl.when(fine == 0)(lambda: body(0))
    else:
        body(prec)


def _merge(x, oa, ob, oc, od, p, tm, prec, fine_rows=None):
    n = x.shape[0]
    layer = p["idx"]
    row = lambda w: pl.BlockSpec((tm, w), lambda i, f: (i, 0))
    params = [p["w_merge"], p["b_merge"], p["w_branch"], p["w_out"]]
    grid_spec = pltpu.PrefetchScalarGridSpec(
        num_scalar_prefetch=1,
        grid=(n // tm,),
        in_specs=[row(D_MODEL), _layer_spec(p["norm1"], layer), row(BW), row(BW), row(BW), row(BW)]
                 + [_layer_spec(a, layer) for a in params],
        out_specs=row(D_MODEL),
    )
    return pl.pallas_call(
        functools.partial(_merge_kernel, prec=prec),
        grid_spec=grid_spec,
        out_shape=jax.ShapeDtypeStruct((n, D_MODEL), F32),
        compiler_params=_cparams(("parallel",)),
        name="merge_p%d" % prec,
    )(_tile_flags(n // tm, tm, fine_rows), x, p["norm1"], oa, ob, oc, od, *params)


def _first_argmax(vals, valid, lane):
    v = jnp.where(valid, vals, NEG)
    top = jnp.max(v, axis=-1, keepdims=True)
    idx = jnp.min(jnp.where(valid & (v == top), lane, 1 << 20), axis=-1, keepdims=True)
    return top, idx


def _router_gate(lg, le):
    lane = lax.broadcasted_iota(jnp.int32, lg.shape, 1)
    gvalid = lane < N_GROUPS
    lgm = jnp.where(gvalid, lg, NEG)
    pe = jnp.exp(lgm - jnp.max(lgm, axis=-1, keepdims=True))
    pg = pe / jnp.sum(pe, axis=-1, keepdims=True)
    pg_top, g_idx = _first_argmax(pg, gvalid, lane)
    sel = (lane // EXP_PER_GROUP == g_idx) & (lane < N_EXPERTS)
    v1, i1 = _first_argmax(le, sel, lane)
    v2, i2 = _first_argmax(le, sel & (lane != i1), lane)
    e2 = jnp.exp(v2 - v1)
    den = 1.0 + e2
    return jnp.where(lane == i1, pg_top / den, jnp.where(lane == i2, pg_top * e2 / den, 0.0))


def _moe_kernel(h_ref, g_ref, rw_ref, rb_ref, sel_ref, w1_ref, w3_ref, w2_ref, y_ref, t_s, gate_s, *, prec):
    grp = pl.program_id(1)

    @pl.when(grp == 0)
    def _():
        x = h_ref[...]
        t = _rms(x, g_ref[0])
        t_s[...] = t.astype(t_s.dtype)
        logits = _wdot(t, rw_ref, (), max(prec, 1)) + rb_ref[0]
        gate_s[...] = _router_gate(logits[:, 0:128], logits[:, 128:256])
        y_ref[...] = x

    t = t_s[...]
    gcols = _seg_mean(gate_s[...], sel_ref[grp], prec)
    hs = []
    for j in range(EXP_PER_GROUP):
        gj = gcols[:, 128 * j:128 * (j + 1)]
        hj = _silu(_wdot(t, w1_ref, (j,), prec)) * _wdot(t, w3_ref, (j,), prec)
        hs.append(hj * jnp.concatenate([gj, gj], axis=1))
    acc = _wdot(hs[0], w2_ref, (0,), prec)
    for j in range(1, EXP_PER_GROUP):
        acc = acc + _wdot(hs[j], w2_ref, (j,), prec)
    y_ref[...] += acc


def _moe(h, p, tm, prec):
    n = h.shape[0]
    layer = p["idx"]
    wspec = lambda a: pl.BlockSpec((1, a.shape[1], EXP_PER_GROUP) + a.shape[3:],
                                   lambda i, g: (layer, 0, g, 0, 0))
    sel = np.zeros((N_GROUPS, 128, EXP_PER_GROUP * 128), np.float32)
    for g in range(N_GROUPS):
        for j in range(EXP_PER_GROUP):
            sel[g, EXP_PER_GROUP * g + j, 128 * j:128 * (j + 1)] = 1.0
    sel = jnp.asarray(sel)
    return pl.pallas_call(
        functools.partial(_moe_kernel, prec=prec),
        grid=(n // tm, N_GROUPS),
        in_specs=[pl.BlockSpec((tm, D_MODEL), lambda i, g: (i, 0)), _layer_spec(p["norm2"], layer),
                  _layer_spec(p["rw"], layer), _layer_spec(p["rb"], layer), _whole_spec(sel),
                  wspec(p["ex_w1"]), wspec(p["ex_w3"]), wspec(p["ex_w2"])],
        out_specs=pl.BlockSpec((tm, D_MODEL), lambda i, g: (i, 0)),
        out_shape=jax.ShapeDtypeStruct((n, D_MODEL), F32),
        scratch_shapes=[pltpu.VMEM((tm, D_MODEL), F32 if prec == 2 else BF16), pltpu.VMEM((tm, 128), F32)],
        compiler_params=_cparams(("parallel", "arbitrary")),
        name="moe_p%d" % prec,
    )(h, p["norm2"], p["rw"], p["rb"], sel, p["ex_w1"], p["ex_w3"], p["ex_w2"])


def _sublane_total(x):
    for shift in (4, 2, 1):
        x = x + pltpu.roll(x, shift, 0)
    return x


def _lane_all(x, op):
    return jnp.broadcast_to(op(x, axis=-1, keepdims=True), x.shape)


def _dec_attn_kernel(pt_ref, q_ref, kn_ref, vn_ref, lam_ref, *rest, pp, lam_init):
    del pt_ref
    kp, vp = rest[:pp], rest[pp:2 * pp]
    o_ref, s_s, m_s, l_s, acc_s = rest[2 * pp:]
    step = pl.program_id(1)

    @pl.when(step == 0)
    def _():
        m_s[...] = jnp.full_like(m_s, NEG)
        l_s[...] = jnp.zeros_like(l_s)
        acc_s[...] = jnp.zeros_like(acc_s)

    def map_scores(t, m):
        part = t[4 * m] + t[4 * m + 1] + t[4 * m + 2] + t[4 * m + 3]
        return _sublane_total(part)

    tops = [None] * (2 * HEADS)
    for j in range(pp):
        for h in range(HEADS):
            t = kp[j][0, 0, h] * q_ref[0, h]
            for m in range(2):
                g = 2 * h + m
                sc = map_scores(t, m)
                s_s[g, j] = sc
                tops[g] = sc if tops[g] is None else jnp.maximum(tops[g], sc)
    for h in range(HEADS):
        m_new, accs, sums = [], [], []
        for m in range(2):
            g = 2 * h + m
            m_old = m_s[g]
            m_new.append(jnp.maximum(m_old, _lane_all(tops[g], jnp.max)))
            alpha = jnp.exp(m_old - m_new[m])
            accs.append(acc_s[g] * alpha)
            sums.append(None)
            l_s[g] = l_s[g] * alpha
            m_s[g] = m_new[m]
        for j in range(pp):
            vt = vp[j][0, 0, h]
            for m in range(2):
                pr = jnp.exp(s_s[2 * h + m, j] - m_new[m])
                accs[m] = accs[m] + vt * pr
                sums[m] = pr if sums[m] is None else sums[m] + pr
        for m in range(2):
            g = 2 * h + m
            acc_s[g] = accs[m]
            l_s[g] = l_s[g] + _lane_all(sums[m], jnp.sum)

    @pl.when(step == pl.num_programs(1) - 1)
    def _():
        lam = _lam_scalar(lam_ref[0], lam_init)
        for h in range(HEADS):
            t = q_ref[0, h] * kn_ref[0, h]
            maps = []
            for m in range(2):
                g = 2 * h + m
                sc = map_scores(t, m)
                m_o = m_s[g][:, 0:1]
                m_n = jnp.maximum(m_o, sc)
                al = jnp.exp(m_o - m_n)
                pn = jnp.exp(sc - m_n)
                num = al * jnp.sum(acc_s[g], axis=-1, keepdims=True) + pn * vn_ref[0, h]
                maps.append(num / (al * l_s[g][:, 0:1] + pn))
            o_ref[0, h] = maps[0] - lam * maps[1]


def _dec_attn(qa, ka, va, cache_k, cache_v, page_table, p):
    db = qa.shape[0]
    layer = p["layer"]
    n_pages = page_table.shape[1]
    n_phys = cache_k.shape[1]
    pp = math.gcd(PAGES_PER_STEP, n_pages)
    tiles = lambda c: c.transpose(0, 1, 3, 4, 2).reshape(c.shape[0], n_phys, HEADS, 8, 8, PAGE)
    ck, cv = tiles(cache_k), tiles(cache_v)
    page = lambda j: pl.BlockSpec((1, 1, HEADS, 8, 8, PAGE),
                                  lambda b, s, pt: (layer, pt[b, s * pp + j], 0, 0, 0, 0))
    col = pl.BlockSpec((1, HEADS, 8, 8, 1), lambda b, s, pt: (b, 0, 0, 0, 0))
    lam = p["lam"]
    grid_spec = pltpu.PrefetchScalarGridSpec(
        num_scalar_prefetch=1,
        grid=(db, n_pages // pp),
        in_specs=[col, col, col, pl.BlockSpec((1,) + lam.shape[1:], lambda b, s, pt: (p["idx"], 0, 0))]
                 + [page(j) for j in range(pp)] + [page(j) for j in range(pp)],
        out_specs=col,
        scratch_shapes=[pltpu.VMEM((2 * HEADS, pp, 8, PAGE), F32), pltpu.VMEM((2 * HEADS, 8, PAGE), F32),
                        pltpu.VMEM((2 * HEADS, 8, PAGE), F32), pltpu.VMEM((2 * HEADS, 8, 8, PAGE), F32)],
    )
    cols = lambda x: x.reshape(db, HEADS, 8, 8, 1)
    out = pl.pallas_call(
        functools.partial(_dec_attn_kernel, pp=pp, lam_init=_lam_init(layer)),
        grid_spec=grid_spec,
        out_shape=jax.ShapeDtypeStruct((db, HEADS, 8, 8, 1), F32),
        compiler_params=_cparams(("parallel", "arbitrary")),
        name="attn_sample",
    )(page_table, cols(qa), cols(ka), cols(va), lam, *([ck] * pp), *([cv] * pp))
    return out.reshape(db * HEADS, 64).T


def _step_kernel(oa_ref, an_ref,
                 bq1_ref, bq2_ref, bk1_ref, bk2_ref, bv_ref, bg_ref, s0_ref, cos_ref, sin_ref, gam_ref, rn_ref,
                 dq_ref, dk_ref, dv_ref, dr_ref, la_ref, g0_ref, gn_ref,
                 cx_ref, cg_ref, buf_ref, h0_ref, cw_ref, cb_ref, wa_ref, wx_ref, ba_ref, bx_ref, lam_ref,
                 oa_o, ob_o, od_o, oc_o, s_o, g_o, h_o, *, lam_init):
    x = oa_ref[...]
    oa_o[...] = x * lax.rsqrt(jnp.mean(x * x, axis=0, keepdims=True) + EPS) * an_ref[0] * (1.0 - lam_init)

    cos, sin, gam = cos_ref[...], sin_ref[...], gam_ref[...]
    rot = lambda x1, x2: jnp.concatenate([x1 * cos - x2 * sin, x1 * sin + x2 * cos], axis=0)
    qb = rot(bq1_ref[...], bq2_ref[...])
    kb = rot(bk1_ref[...], bk2_ref[...]) * 0.125
    v = bv_ref[...]
    o = jnp.sum(qb * kb, axis=0, keepdims=True) * v
    st = jnp.zeros_like(v)
    for k in range(64):
        s0k = s0_ref[k]
        st = st + s0k * qb[k:k + 1]
        s_o[k] = gam * s0k + kb[k:k + 1] * v
    o = o + gam * st
    mu = jnp.mean(o, axis=0, keepdims=True)
    var = jnp.mean((o - mu) * (o - mu), axis=0, keepdims=True)
    ob_o[...] = (o - mu) * lax.rsqrt(var + EPS) * rn_ref[0] * _silu(bg_ref[...])

    qd = dq_ref[...] * (DQA ** -0.5)
    kd = dk_ref[...]
    vd = dv_ref[...]
    dec = jnp.exp(la_ref[...])
    o = jnp.sum(qd * kd, axis=0, keepdims=True) * vd
    qe = qd * dec
    for k in range(32):
        g0k = g0_ref[k]
        o = o + g0k * qe[k:k + 1]
        g_o[k] = dec[k:k + 1] * g0k + kd[k:k + 1] * vd
    od_o[...] = o * lax.rsqrt(jnp.mean(o * o, axis=0, keepdims=True) + EPS) * gn_ref[0] * _silu(dr_ref[...])

    cx = cx_ref[...]
    y = cb_ref[0] + cx * cw_ref[0, CONV_W - 1:CONV_W, :]
    for i in range(CONV_W - 1):
        y = y + buf_ref[i] * cw_ref[0, i:i + 1, :]
    a, u = _lru_gates(y, wa_ref, wx_ref, ba_ref[0], bx_ref[0], _softplus(-lam_ref[0]), 2)
    h = a * h0_ref[...] + u
    h_o[...] = h
    oc_o[...] = h * _gelu_tanh(cg_ref[...])


def _to_fm(x, db, width):
    return x.reshape(db, HEADS, width).transpose(2, 0, 1).reshape(width, db * HEADS)


def _from_fm(x, db, width):
    return x.reshape(width, db, HEADS).transpose(1, 2, 0).reshape(db, HEADS * width)


def _halves_fm(x, db):
    x4 = x.reshape(db, HEADS, 2, 32).transpose(2, 3, 0, 1).reshape(2, 32, db * HEADS)
    return x4[0], x4[1]


def _sample_step(oa_fm, sb, sc, sd, ret0, gla0, conv_buf, h0, p, pos):
    db = sb.shape[0]
    n = db * HEADS
    layer = p["idx"]
    cos, sin = _rope_cos_sin(jnp.asarray([pos]))
    gam = jnp.tile(jnp.asarray(_ret_gammas(), F32), db)[None, :]
    bq1, bq2 = _halves_fm(sb[:, 0:BW], db)
    bk1, bk2 = _halves_fm(sb[:, BW:2 * BW], db)
    args = [
        oa_fm, p["an_c"],
        bq1, bq2, bk1, bk2, _to_fm(sb[:, 2 * BW:3 * BW], db, 64), _to_fm(sb[:, 3 * BW:4 * BW], db, 64),
        ret0.transpose(2, 3, 0, 1).reshape(64, 64, n), cos.reshape(32, 1), sin.reshape(32, 1), gam, p["rn_c"],
        _to_fm(sd[:, 0:128], db, 32), _to_fm(sd[:, 128:256], db, 32), _to_fm(sd[:, 256:512], db, 64),
        _to_fm(sd[:, 512:768], db, 64), _to_fm(sd[:, 768:896], db, 32),
        gla0.transpose(2, 3, 0, 1).reshape(32, 64, n), p["gn_c"],
        sc[:, 0:BW], sc[:, BW:2 * BW], conv_buf.transpose(1, 0, 2), h0,
        p["conv_w"], p["conv_b"], p["lru_wa_bd"], p["lru_wx_bd"], p["lru_ba"], p["lru_bx"], p["lru_lam"],
    ]
    layered = {1, 12, 19, 24, 25, 26, 27, 28, 29, 30}
    in_specs = [_layer_spec(a, layer) if i in layered else _whole_spec(a) for i, a in enumerate(args)]
    outs = [jax.ShapeDtypeStruct((64, n), F32)] * 3 + [jax.ShapeDtypeStruct((db, BW), F32),
            jax.ShapeDtypeStruct((64, 64, n), F32), jax.ShapeDtypeStruct((32, 64, n), F32),
            jax.ShapeDtypeStruct((db, BW), F32)]
    oa, ob, od, oc, s_new, g_new, h_new = pl.pallas_call(
        functools.partial(_step_kernel, lam_init=_lam_init(p["layer"])),
        grid=(1,),
        in_specs=in_specs,
        out_specs=[pl.BlockSpec(o.shape, lambda i, nd=len(o.shape): (0,) * nd) for o in outs],
        out_shape=outs,
        compiler_params=_cparams(("arbitrary",)),
        name="step_sample",
    )(*args)
    ret_new = s_new.reshape(64, 64, db, HEADS).transpose(2, 3, 0, 1)
    gla_new = g_new.reshape(32, 64, db, HEADS).transpose(2, 3, 0, 1)
    return _from_fm(oa, db, 64), _from_fm(ob, db, 64), oc, _from_fm(od, db, 64), ret_new, h_new, gla_new


def _block_diag(w):
    d, h, n, _ = w.shape
    eye = jnp.eye(h, dtype=w.dtype)
    return jnp.einsum("dhij,hg->dhigj", w, eye).reshape(d, h * n, h * n)


def _seg_matrix(seg):
    idx = np.arange(BW)
    return jnp.asarray((idx[:, None] // seg == idx[None, :] // seg) / float(seg), F32)


def _parts(w, prec, with_cast=False):
    if prec == 2:
        return w[:, None]
    if prec == 0:
        return w.astype(BF16)[:, None]
    return jnp.stack(_split_bits(w) + ((w.astype(BF16),) if with_cast else ()), axis=1)


def _prepare(raw, layers, prec, moe_prec, layer):
    sl = lambda a: a[layers]
    (norm1, w_in, qn, kn, lam, an, rn, conv_w, conv_b, lru_wa, lru_ba, lru_wx, lru_bx, lru_lam, gla_w2, gla_b,
     gn, w_branch, w_merge, b_merge, w_out, norm2, rg_w, rg_b, re_w, re_b, ex_w1, ex_w3, ex_w2) = raw
    nl = sl(norm1).shape[0]
    r3 = lambda a: sl(a).reshape(nl, 1, -1)
    tile3 = lambda a, k: jnp.tile(sl(a), (1, k)).reshape(nl, 1, -1)
    pad_cols = lambda a, w: jnp.pad(a, ((0, 0), (0, 0), (0, w - a.shape[-1])))
    w_in = sl(w_in)
    c = np.cumsum([0, 768, 1024, 512, 784])
    rw = jnp.concatenate([pad_cols(sl(rg_w), 128), pad_cols(sl(re_w), 128)], axis=-1)
    return dict(
        idx=layer if nl > 1 else 0, layer=layer,
        norm1=r3(norm1), qn_t=tile3(qn, 8), kn_t=tile3(kn, 8), lam=sl(lam), an_t=tile3(an, 4), rn_t=tile3(rn, 4),
        gn_t=tile3(gn, 4), an_c=sl(an).reshape(nl, 64, 1), rn_c=sl(rn).reshape(nl, 64, 1),
        gn_c=sl(gn).reshape(nl, 64, 1), conv_w=sl(conv_w), conv_b=r3(conv_b), lru_ba=r3(lru_ba),
        lru_bx=r3(lru_bx), lru_lam=r3(lru_lam), gla_b=r3(gla_b), b_merge=r3(b_merge), norm2=r3(norm2),
        rb=jnp.concatenate([pad_cols(r3(rg_b), 128), pad_cols(r3(re_b), 128)], axis=-1),
        m32=_seg_matrix(32), m64=_seg_matrix(64),
        wa=_parts(w_in[:, :, c[0]:c[1]], prec, True), wb=_parts(w_in[:, :, c[1]:c[2]], prec, True),
        wc=_parts(w_in[:, :, c[2]:c[3]], prec, True),
        wd=_parts(pad_cols(w_in[:, :, c[3]:c[4]], W_D), prec, True),
        gla_w2p=_parts(jnp.pad(sl(gla_w2), ((0, 0), (0, 128 - GLA_RANK), (0, 0))), prec),
        lru_wa_bd=_parts(_block_diag(sl(lru_wa)), prec), lru_wx_bd=_parts(_block_diag(sl(lru_wx)), prec),
        w_merge=_parts(sl(w_merge), prec, True), w_branch=_parts(sl(w_branch), prec, True),
        w_out=_parts(sl(w_out), prec, True),
        rw=_parts(rw, max(moe_prec, 1)),
        ex_w1=_parts(sl(ex_w1), moe_prec), ex_w3=_parts(sl(ex_w3), moe_prec), ex_w2=_parts(sl(ex_w2), moe_prec),
    )


def _diag_blocks(s, rows, cols):
    b = s.shape[0]
    s5 = s.reshape(b, HEADS, rows, HEADS, cols)
    return jnp.stack([s5[:, h, :, h, :] for h in range(HEADS)], axis=1)


def _prompt_tile(n, prec):
    for tm in ((256, 128) if prec else (512, 256, 128)):
        if n % tm == 0:
            return tm
    raise ValueError(n)


def kernel(x_prompt, x_sample, cache_k_a, cache_v_a, state_ret, state_lru_h, state_lru_conv, state_gla,
           page_table, meta, norm1, w_in, qn, kn, lam, an, rn, conv_w, conv_b, lru_wa, lru_ba, lru_wx,
           lru_bx, lru_lam, gla_w2, gla_b, gn, w_branch, w_merge, b_merge, w_out, norm2, rg_w, rg_b,
           re_w, re_b, ex_w1, ex_w3, ex_w2):
    bp, seq, _ = x_prompt.shape
    db = x_sample.shape[0]
    depth = w_in.shape[0]
    assert seq % BLK == 0 and x_sample.shape[1] == 1
    lp = BLK + seq
    past = page_table.shape[1] * PAGE
    raw = (norm1, w_in, qn, kn, lam, an, rn, conv_w, conv_b, lru_wa, lru_ba, lru_wx, lru_bx, lru_lam, gla_w2,
           gla_b, gn, w_branch, w_merge, b_merge, w_out, norm2, rg_w, rg_b, re_w, re_b, ex_w1, ex_w3, ex_w2)
    head = jnp.concatenate([jnp.zeros((PADL, D_MODEL), F32), meta.astype(F32)], axis=0)
    xp = jnp.concatenate([jnp.broadcast_to(head[None], (bp, BLK, D_MODEL)), x_prompt], axis=1)
    xp = xp.reshape(bp * lp, D_MODEL)
    xs = x_sample.reshape(db, D_MODEL)
    new_p, new_s = [], []
    for layer in range(depth):
        prec = 1 if layer == 0 else 0
        tm = _prompt_tile(bp * lp, prec)
        p = _prepare(raw, slice(layer, layer + 1), prec, 0, layer)
        tail_rows = [b * lp + lp - 1 - r for b in range(bp) for r in range(CONV_W - 1)]
        ka, va, sb, sc, sd, qp, kp, vp = _in_proj(xp, p, tm, prec, lp, tail_rows)
        oa = _attn_prompt(qp, kp, vp, p, bp, lp, prec)
        ob, s_ret = _ret_prompt(sb, p, bp, lp, prec)
        oc, h_last = _lru_prompt(sc, p, bp, lp, prec)
        od, s_gla = _gla_prompt(sd, p, bp, lp, prec)
        hres = _merge(xp, oa, ob, oc, od, p, tm, prec, tail_rows)
        xp = _moe(hres, p, _prompt_tile(bp * lp, 0), 0)
        new_p.append((ka.reshape(bp, lp, HEADS, 64)[:, PADL:], va.reshape(bp, lp, HEADS, 64)[:, PADL:],
                      _diag_blocks(s_ret, 64, 64), h_last,
                      sc.reshape(bp, lp, W_C)[:, lp - (CONV_W - 1):, 0:BW],
                      _diag_blocks(s_gla, 64, 32).transpose(0, 1, 3, 2)))
        ps = _prepare(raw, slice(None), 2, 2, layer)
        ka, va, sb, sc, sd, qa = _in_proj(xs, ps, db, 2)
        oa_fm = _dec_attn(qa, ka, va, cache_k_a, cache_v_a, page_table, ps)
        oa, ob, oc, od, ret_new, h_new, gla_new = _sample_step(
            oa_fm, sb, sc, sd, state_ret[layer], state_gla[layer], state_lru_conv[layer], state_lru_h[layer],
            ps, past)
        hres = _merge(xs, oa, ob, oc, od, ps, db, 2)
        xs = _moe(hres, ps, db, 2)
        conv_new = jnp.concatenate([state_lru_conv[layer][:, 1:], sc[:, None, 0:BW]], axis=1)
        new_s.append((ka.reshape(db, 1, HEADS, 64), va.reshape(db, 1, HEADS, 64), ret_new, h_new, conv_new,
                      gla_new))
    stk = lambda states, i: jnp.stack([s[i] for s in states])
    y_prompt = xp.reshape(bp, lp, D_MODEL)[:, BLK:]
    return ((y_prompt, xs.reshape(db, 1, D_MODEL))
            + tuple(stk(new_p, i) for i in range(6)) + tuple(stk(new_s, i) for i in range(6)))
```

```python
import functools
import math

import numpy as np
import jax
import jax.numpy as jnp
from jax import lax
from jax.experimental import pallas as pl
from jax.experimental.pallas import tpu as pltpu

F32 = jnp.float32
BF16 = jnp.bfloat16
EPS = 1e-6
D_MODEL = 1024
N_META = 16
PAGE = 128
BLK = 128
PADL = BLK - N_META
HEADS = 4
BW = 256
DQA = 32
ROPE_BASE = 10000.0
CONV_W = 4
LRU_C = 8.0
GLA_RANK = 16
GLA_TAU = 16.0
GLA_CHUNK = 16
N_GROUPS = 4
EXP_PER_GROUP = 4
N_EXPERTS = 16
D_EXPERT = 256
W_A, W_B, W_C, W_D = 768, 1024, 512, 896
QK_PACK = 8 * 128
V_PACK = HEADS * 128
VMEM_LIMIT = 56 * 1024 * 1024
NEG = -1e30
PAGES_PER_STEP = 16
ATTN_KEY_STEP = 256
PAD_LANE = 96
LOG2E = math.log2(math.e)


def _cparams(sem):
    return pltpu.CompilerParams(dimension_semantics=sem, vmem_limit_bytes=VMEM_LIMIT)


def _layer_spec(arr, layer):
    nd = arr.ndim
    return pl.BlockSpec((1,) + arr.shape[1:], lambda *_: (layer,) + (0,) * (nd - 1),
                        pipeline_mode=pl.Buffered(1))


def _whole_spec(arr):
    nd = arr.ndim
    return pl.BlockSpec(arr.shape, lambda *_: (0,) * nd, pipeline_mode=pl.Buffered(1))


def _bf16_prefix(x):
    bits = lax.bitcast_convert_type(x, jnp.uint32)
    bits = (bits + jnp.uint32(0x7FFF) + ((bits >> 16) & jnp.uint32(1))) & jnp.uint32(0xFFFF0000)
    return lax.bitcast_convert_type(bits, F32)


def _split_bits(x):
    hi = _bf16_prefix(x)
    return hi.astype(BF16), (x - hi).astype(BF16)


def _split(x):
    hi = x.astype(BF16)
    return hi, (x - hi.astype(F32)).astype(BF16)


def _split3(x):
    hi = x.astype(BF16)
    r = x - hi.astype(F32)
    mid = r.astype(BF16)
    return hi, mid, (r - mid.astype(F32)).astype(BF16)


def _bdot(a, b, dims=None):
    if dims is None:
        return jnp.dot(a, b, preferred_element_type=F32)
    return lax.dot_general(a, b, (dims, ((), ())), preferred_element_type=F32)


def _branch_dtype(prec):
    return BF16 if prec == 0 else F32


def _operand(a, prec):
    if prec == 2:
        return a
    return _split(a) if prec == 1 else a.astype(BF16)


def _wdot(a, w_ref, idx, prec):
    get = lambda part: w_ref[(0, part) + idx]
    if prec == 2:
        return jnp.dot(a, get(0), precision=lax.Precision.HIGHEST, preferred_element_type=F32)
    if prec == 1:
        a_hi, a_lo = a if isinstance(a, tuple) else _split(a)
        return _bdot(a_hi, get(0)) + _bdot(a_lo, get(0)) + _bdot(a_hi, get(1))
    assert w_ref.shape[1] in (1, 3), "level 0 needs the plain bf16 cast: parts [cast] or [hi, lo, cast]"
    return _bdot(a.astype(BF16), get(w_ref.shape[1] - 1))


def _adot(a, b, prec, dims=None):
    if prec == 1:
        a_hi, a_lo = _split(a)
        b_hi, b_lo = _split(b)
        return _bdot(a_hi, b_hi, dims) + _bdot(a_lo, b_hi, dims) + _bdot(a_hi, b_lo, dims)
    return _bdot(a.astype(BF16), b.astype(BF16), dims)


NT = ((1,), (1,))
TN = ((0,), (0,))


def _seg_mean(x, m, prec=0):
    if prec == 2:
        return jnp.dot(x, m, precision=lax.Precision.HIGHEST, preferred_element_type=F32)
    hi, lo = _split(x)
    mb = m.astype(BF16)
    return _bdot(hi, mb) + _bdot(lo, mb)


def _rms(x, g):
    return x * lax.rsqrt(jnp.mean(x * x, axis=-1, keepdims=True) + EPS) * g


def _sigmoid(x):
    return 1.0 / (1.0 + jnp.exp(-x))


def _silu(x):
    return x * _sigmoid(x)


def _log_sigmoid(x):
    return jnp.minimum(x, 0.0) - jnp.log(1.0 + jnp.exp(-jnp.abs(x)))


def _softplus(x):
    return jnp.maximum(x, 0.0) + jnp.log(1.0 + jnp.exp(-jnp.abs(x)))


def _gelu_tanh(x):
    return 0.5 * x * (1.0 + jnp.tanh(math.sqrt(2.0 / math.pi) * (x + 0.044715 * (x * x * x))))


def _lane_mask(width, lo, hi):
    lane = lax.broadcasted_iota(jnp.int32, (1, width), 1)
    return (lane >= lo) & (lane < hi)


def _lam_scalar(lv, lam_init):
    s01 = jnp.sum(lv[0:1] * lv[1:2], axis=-1, keepdims=True)
    s23 = jnp.sum(lv[2:3] * lv[3:4], axis=-1, keepdims=True)
    return jnp.exp(s01) - jnp.exp(s23) + lam_init


def _lam_init(layer):
    return 0.8 - 0.6 * math.exp(-0.3 * layer)


_SPANS_A = ((0, 256, False), (256, 768, True))
_SPANS_B = ((0, 256, False), (256, 768, True), (768, 1024, False))
_SPANS_C = ((0, 256, True), (256, 512, False))
_SPANS_D = ((0, 128, False), (128, 512, True), (512, 768, False), (768, 896, True))


def _in_proj_kernel(fine_ref, x_ref, g_ref, wa_ref, wb_ref, wc_ref, wd_ref, qn_ref, kn_ref, m32_ref, w2_ref,
                    gb_ref, pq_ref, pk_ref, pv_ref, ka_ref, va_ref, sb_ref, sc_ref, sd_ref, *rest, prec, seq_len):
    q_out, (a_s, d_s) = rest[:-2], rest[-2:]
    xn = _rms(x_ref[...], g_ref[0])
    slabs = ((wa_ref, a_s, _SPANS_A), (wb_ref, sb_ref, _SPANS_B), (wc_ref, sc_ref, _SPANS_C), (wd_ref, d_s, _SPANS_D))

    def project(mixed):
        x_full = _operand(xn, prec)
        x_cast = xn.astype(BF16) if mixed else None
        for w_ref, dst, spans in slabs:
            if not mixed:
                dst[...] = _wdot(x_full, w_ref, (), prec)
                continue
            for lo, hi, carried in spans:
                cols = (slice(None), slice(lo, hi))
                dst[:, lo:hi] = _wdot(x_full, w_ref, cols, prec) if carried else _wdot(x_cast, w_ref, cols, 0)

    if prec == 1:
        fine = fine_ref[pl.program_id(0)]
        pl.when(fine != 0)(lambda: project(False))
        pl.when(fine == 0)(lambda: project(True))
    else:
        project(False)
    a, d = a_s[...], d_s[...]
    q, k, v = a[:, 0:BW], a[:, BW:2 * BW], a[:, 2 * BW:3 * BW]
    m32 = m32_ref[...]
    qa = q * lax.rsqrt(_seg_mean(q * q, m32, prec) + EPS) * qn_ref[0] * (DQA ** -0.5)
    ka = k * lax.rsqrt(_seg_mean(k * k, m32, prec) + EPS) * kn_ref[0]
    ka_ref[...] = ka
    va_ref[...] = v
    sd_ref[:, 0:768] = d[:, 0:768]
    z = _wdot(d[:, 768:896], w2_ref, (), prec) + gb_ref[0]
    sd_ref[:, 768:896] = _log_sigmoid(z) * (1.0 / GLA_TAU)
    if len(q_out) == 1:
        q_out[0][...] = qa
    else:
        def place(src, p_ref):
            hi, lo = _split(src)
            return _bdot(hi, p_ref[0]) + _bdot(lo, p_ref[1])

        tm = x_ref.shape[0]
        row = (pl.program_id(0) * tm + lax.broadcasted_iota(jnp.int32, (tm, 1), 0)).astype(F32)
        in_seq = row - jnp.floor((row + 0.5) * (1.0 / seq_len)) * seq_len
        is_pad = in_seq < PADL
        flag = (lax.broadcasted_iota(jnp.int32, (1, QK_PACK), 1) % 128 == PAD_LANE).astype(F32)
        qp_ref, kp_ref, vp_ref = q_out
        qp_ref[...] = (place(qa * LOG2E, pq_ref) + jnp.where(is_pad, 0.0, 1.0) * flag).astype(BF16)
        kp_ref[...] = (place(ka, pk_ref) + jnp.where(is_pad, NEG, 0.0) * flag).astype(BF16)
        vp_ref[...] = place(v, pv_ref).astype(BF16)


def _pack_matrices():
    pq = np.zeros((2, BW, QK_PACK), np.float32)
    pk = np.zeros((2, BW, QK_PACK), np.float32)
    pv = np.zeros((2, BW, V_PACK), np.float32)
    for g in range(8):
        for d in range(DQA):
            r, c = DQA * g + d, 128 * g + d
            pq[0, r, c] = pq[0, r, c + 64] = 1.0
            pq[1, r, c + 32] = 1.0
            pk[0, r, c] = pk[0, r, c + 32] = 1.0
            pk[1, r, c + 64] = 1.0
    for h in range(HEADS):
        for d in range(64):
            pv[0, 64 * h + d, 128 * h + d] = 1.0
            pv[1, 64 * h + d, 128 * h + 64 + d] = 1.0
    return [jnp.asarray(a, BF16) for a in (pq, pk, pv)]


def _tile_flags(n_tiles, tm, rows):
    if rows is None:
        return jnp.ones((n_tiles,), jnp.int32)
    fine = np.zeros((n_tiles,), np.int32)
    fine[np.asarray(rows) // tm] = 1
    return jnp.asarray(fine)


def _in_proj(x, p, tm, prec, seq_len=None, fine_rows=None):
    n = x.shape[0]
    packed = seq_len is not None
    layer = p["idx"]
    outs = [jax.ShapeDtypeStruct((n, BW), F32)] * 2 + [jax.ShapeDtypeStruct((n, W_B), F32),
            jax.ShapeDtypeStruct((n, W_C), F32), jax.ShapeDtypeStruct((n, W_D), F32)]
    if packed:
        outs += [jax.ShapeDtypeStruct((n, QK_PACK), BF16)] * 2 + [jax.ShapeDtypeStruct((n, V_PACK), BF16)]
    else:
        outs += [jax.ShapeDtypeStruct((n, BW), F32)]
    row = lambda w: pl.BlockSpec((tm, w), lambda i, f: (i, 0))
    params = [p["norm1"], p["wa"], p["wb"], p["wc"], p["wd"], p["qn_t"], p["kn_t"]]
    tail = [p["gla_w2p"], p["gla_b"]]
    consts = _pack_matrices()
    grid_spec = pltpu.PrefetchScalarGridSpec(
        num_scalar_prefetch=1,
        grid=(n // tm,),
        in_specs=([row(D_MODEL)] + [_layer_spec(a, layer) for a in params] + [_whole_spec(p["m32"])]
                  + [_layer_spec(a, layer) for a in tail] + [_whole_spec(a) for a in consts]),
        out_specs=[row(o.shape[1]) for o in outs],
        scratch_shapes=[pltpu.VMEM((tm, W_A), F32), pltpu.VMEM((tm, W_D), F32)],
    )
    return pl.pallas_call(
        functools.partial(_in_proj_kernel, prec=prec, seq_len=seq_len),
        grid_spec=grid_spec,
        out_shape=outs,
        compiler_params=_cparams(("parallel",)),
        name="in_proj_p%d" % prec,
    )(_tile_flags(n // tm, tm, fine_rows), x, *params, p["m32"], *tail, *consts)


def _attn_body(q_ref, k_ref, v_ref, lam, an_ref, o_ref, s_buf, w_buf, i, nk, *, lam_init, tq, prec):
    n_free = max(nk - ATTN_KEY_STEP, 0)
    pos_q = i * tq + lax.broadcasted_iota(jnp.int32, (tq, 1), 0)
    causal = n_free + lax.broadcasted_iota(jnp.int32, (1, nk - n_free), 1) <= pos_q
    lane = lax.broadcasted_iota(jnp.int32, (1, 128), 1)
    for g in range(2 * HEADS):
        s_buf[g, :, 0:nk] = _bdot(q_ref[:, 128 * g:128 * (g + 1)], k_ref[0:nk, 128 * g:128 * (g + 1)], NT)
    outs = []
    for h in range(HEADS):
        scale = []
        for m in range(2):
            g = 2 * h + m
            tail = jnp.where(causal, s_buf[g, :, n_free:nk], NEG)
            mx = jnp.max(tail, axis=-1, keepdims=True)
            if n_free:
                mx = jnp.maximum(mx, jnp.max(s_buf[g, :, 0:n_free], axis=-1, keepdims=True))
                head_p = jnp.exp2(s_buf[g, :, 0:n_free] - mx)
                s_buf[g, :, 0:n_free] = head_p
            tail_p = jnp.exp2(tail - mx)
            s_buf[g, :, n_free:nk] = tail_p
            total = jnp.sum(tail_p, axis=-1, keepdims=True)
            if n_free:
                total = total + jnp.sum(head_p, axis=-1, keepdims=True)
            scale.append(1.0 / total)
        w = s_buf[2 * h, :, 0:nk] * scale[0] - s_buf[2 * h + 1, :, 0:nk] * (lam * scale[1])
        vt = v_ref[0:nk, 128 * h:128 * (h + 1)]
        if prec == 1:
            w_hi, w_lo = _split(w)
            w_buf[h, 0:tq, 0:nk] = w_hi
            w_buf[h, tq:2 * tq, 0:nk] = w_lo
            o2 = _bdot(w_buf[h, :, 0:nk], vt)
            o = o2[0:tq] + o2[tq:2 * tq]
        else:
            w_buf[h, 0:tq, 0:nk] = w.astype(BF16)
            o = _bdot(w_buf[h, 0:tq, 0:nk], vt)
        o = o + pltpu.roll(o, 64, 1)
        outs.append(o * lax.rsqrt(jnp.mean(o * o, axis=-1, keepdims=True) + EPS))
    for t in range(2):
        pair = jnp.where(lane < 64, outs[2 * t], outs[2 * t + 1])
        o_ref[:, 128 * t:128 * (t + 1)] = (pair * an_ref[0][:, 128 * t:128 * (t + 1)]
                                           * (1.0 - lam_init)).astype(o_ref.dtype)


def _attn_kernel(q_ref, k_ref, v_ref, lam_ref, an_ref, o_ref, s_buf, w_buf, *, lam_init, tq, key_ranges, prec):
    i = pl.program_id(1)
    lam = _lam_scalar(lam_ref[0], lam_init)
    need = (i + 1) * tq
    lo = 0
    for nk in key_ranges:
        @pl.when((need > lo) & (need <= nk))
        def _(nk=nk):
            _attn_body(q_ref, k_ref, v_ref, lam, an_ref, o_ref, s_buf, w_buf, i, nk, lam_init=lam_init, tq=tq,
                       prec=prec if nk == key_ranges[-1] else 0)
        lo = nk


def _attn_prompt(qp, kp, vp, p, b, lp, prec):
    nq = lp // BLK
    layer = p["idx"]
    key_ranges = tuple(sorted({min(lp, ATTN_KEY_STEP * j) for j in range(1, -(-lp // ATTN_KEY_STEP) + 1)}))
    return pl.pallas_call(
        functools.partial(_attn_kernel, lam_init=_lam_init(p["layer"]), tq=BLK, key_ranges=key_ranges,
                          prec=prec),
        grid=(b, nq),
        in_specs=[pl.BlockSpec((BLK, QK_PACK), lambda bi, i: (bi * nq + i, 0)),
                  pl.BlockSpec((lp, QK_PACK), lambda bi, i: (bi, 0)),
                  pl.BlockSpec((lp, V_PACK), lambda bi, i: (bi, 0)),
                  _layer_spec(p["lam"], layer), _layer_spec(p["an_t"], layer)],
        out_specs=pl.BlockSpec((BLK, BW), lambda bi, i: (bi * nq + i, 0)),
        out_shape=jax.ShapeDtypeStruct((b * lp, BW), _branch_dtype(prec)),
        scratch_shapes=[pltpu.VMEM((2 * HEADS, BLK, lp), F32), pltpu.VMEM((HEADS, 2 * BLK, lp), BF16)],
        compiler_params=_cparams(("parallel", "parallel")),
        name="attn_prompt_p%d" % prec,
    )(qp, kp, vp, p["lam"], p["an_t"])


def _swap_halves(x):
    w = x.shape[-1]
    lane = lax.broadcasted_iota(jnp.int32, (1, w), 1)
    return jnp.where((lane % 64) < 32, pltpu.roll(x, w - 32, 1), pltpu.roll(x, 32, 1))


def _ret_kernel(sb_ref, cos_ref, sin_ref, dmat_ref, dq_ref, dk_ref, dstate_ref, rn_ref, m64_ref,
                o_ref, s_out_ref, s_ref, *, prec):
    c = pl.program_id(1)

    @pl.when(c == 0)
    def _():
        s_ref[...] = jnp.zeros_like(s_ref)

    cos, sin = cos_ref[...], sin_ref[...]
    q, k, v = sb_ref[:, 0:BW], sb_ref[:, BW:2 * BW], sb_ref[:, 2 * BW:3 * BW]
    qr = q * cos + _swap_halves(q) * sin
    kr = (k * cos + _swap_halves(k) * sin) * 0.125
    o = _adot(qr, s_ref[...], prec) * dq_ref[...]
    upd = _adot(kr * dk_ref[...], v, prec, TN)
    masks = [_lane_mask(BW, 64 * h, 64 * (h + 1)) for h in range(HEADS)]
    scores = [_adot(jnp.where(hm, qr, 0.0), kr, prec, NT) for hm in masks]
    for h, hm in enumerate(masks):
        o = o + jnp.where(hm, _adot(scores[h] * dmat_ref[h], v, prec), 0.0)
    row_head = lax.broadcasted_iota(jnp.int32, (BW, BW), 0) // 64
    col_head = lax.broadcasted_iota(jnp.int32, (BW, BW), 1) // 64
    s_new = s_ref[...] * dstate_ref[...] + jnp.where(row_head == col_head, upd, 0.0)
    s_ref[...] = s_new
    s_out_ref[0] = s_new
    m64 = m64_ref[...]
    mu = _seg_mean(o, m64)
    var = _seg_mean((o - mu) * (o - mu), m64)
    g = sb_ref[:, 3 * BW:4 * BW]
    o_ref[...] = ((o - mu) * lax.rsqrt(var + EPS) * rn_ref[0] * _silu(g)).astype(o_ref.dtype)


def _ret_gammas():
    return 1.0 - 2.0 ** (-5.0 - np.arange(HEADS, dtype=np.float64))


def _ret_tables():
    lg = np.log(_ret_gammas())
    t = np.arange(BLK, dtype=np.float64)
    rel = t[:, None] - t[None, :]
    dmat = np.where(rel >= 0, np.exp(np.maximum(rel, 0.0)[None] * lg[:, None, None]), 0.0)
    lane_lg = np.repeat(lg, 64)[None, :]
    dq = np.exp((t[:, None] + 1.0) * lane_lg)
    dk = np.exp((BLK - 1.0 - t)[:, None] * lane_lg)
    dstate = np.exp(BLK * lane_lg)
    return [jnp.asarray(a, F32) for a in (dmat, dq, dk, dstate)]


def _rope_cos_sin(pos):
    inv = ROPE_BASE ** (-jnp.linspace(0.0, 1.0, 32))
    ang = pos.astype(F32)[:, None] * inv[None]
    return jnp.cos(ang), jnp.sin(ang)


def _ret_prompt(sb, p, b, lp, prec):
    nc = lp // BLK
    cos, sin = _rope_cos_sin(jnp.arange(lp) - PADL)
    cos_t = jnp.tile(jnp.concatenate([cos, cos], -1), (1, HEADS))
    sin_t = jnp.tile(jnp.concatenate([-sin, sin], -1), (1, HEADS))
    tables = _ret_tables()
    return pl.pallas_call(
        functools.partial(_ret_kernel, prec=prec),
        grid=(b, nc),
        in_specs=[pl.BlockSpec((BLK, W_B), lambda bi, c: (bi * nc + c, 0)),
                  pl.BlockSpec((BLK, BW), lambda bi, c: (c, 0)),
                  pl.BlockSpec((BLK, BW), lambda bi, c: (c, 0))]
                 + [_whole_spec(a) for a in tables] + [_layer_spec(p["rn_t"], p["idx"]), _whole_spec(p["m64"])],
        out_specs=[pl.BlockSpec((BLK, BW), lambda bi, c: (bi * nc + c, 0)),
                   pl.BlockSpec((1, BW, BW), lambda bi, c: (bi, 0, 0))],
        out_shape=[jax.ShapeDtypeStruct((b * lp, BW), _branch_dtype(prec)), jax.ShapeDtypeStruct((b, BW, BW), F32)],
        scratch_shapes=[pltpu.VMEM((BW, BW), F32)],
        compiler_params=_cparams(("parallel", "arbitrary")),
        name="ret_prompt_p%d" % prec,
    )(sb, cos_t, sin_t, *tables, p["rn_t"], p["m64"])


def _lru_gates(y, wa_ref, wx_ref, ba, bx, sp, prec):
    ys = _operand(y, prec)
    r = _sigmoid(_wdot(ys, wa_ref, (), prec) + ba)
    ig = _sigmoid(_wdot(ys, wx_ref, (), prec) + bx)
    log_a = -LRU_C * r * sp
    a = jnp.exp(log_a)
    u = jnp.sqrt(1.0 - jnp.exp(2.0 * log_a)) * (ig * y)
    return a, u


def _lru_kernel(sc_ref, cw_ref, cb_ref, wa_ref, wx_ref, ba_ref, bx_ref, lam_ref,
                o_ref, hl_ref, cbuf, a_s, u_s, h_s, hcar, *, nb, prec):
    t = pl.program_id(0)

    @pl.when(t == 0)
    def _():
        cbuf[...] = jnp.zeros_like(cbuf)
        hcar[...] = jnp.zeros_like(hcar)

    sp = _softplus(-lam_ref[0])
    row = t * BLK + lax.broadcasted_iota(jnp.int32, (BLK, 1), 0)
    for b in range(nb):
        x = sc_ref[b, :, 0:BW]
        cbuf[b, 8:8 + BLK, :] = x
        y = cb_ref[0] + cbuf[b, 5:5 + BLK, :] * cw_ref[0, 0:1, :]
        for i in range(1, CONV_W):
            y = y + cbuf[b, 5 + i:5 + i + BLK, :] * cw_ref[0, i:i + 1, :]
        cbuf[b, 0:8, :] = x[BLK - 8:BLK, :]
        a, u = _lru_gates(y, wa_ref, wx_ref, ba_ref[0], bx_ref[0], sp, prec)
        a_s[b] = a
        u_s[b] = jnp.where(row >= PADL, u, 0.0)

    def step(i, hs):
        out = []
        for b in range(nb):
            h = a_s[b, pl.ds(i, 1), :] * hs[b] + u_s[b, pl.ds(i, 1), :]
            h_s[b, pl.ds(i, 1), :] = h
            out.append(h)
        return tuple(out)

    hs = lax.fori_loop(0, BLK, step, tuple(hcar[b:b + 1, :] for b in range(nb)))
    for b in range(nb):
        hcar[b:b + 1, :] = hs[b]
        o_ref[b] = (h_s[b] * _gelu_tanh(sc_ref[b, :, BW:2 * BW])).astype(o_ref.dtype)
    hl_ref[...] = hcar[...]


def _lru_prompt(sc, p, b, lp, prec):
    nt = lp // BLK
    sc3 = sc.reshape(b, lp, W_C)
    params = [p["conv_w"], p["conv_b"], p["lru_wa_bd"], p["lru_wx_bd"], p["lru_ba"], p["lru_bx"], p["lru_lam"]]
    o, hl = pl.pallas_call(
        functools.partial(_lru_kernel, nb=b, prec=prec),
        grid=(nt,),
        in_specs=[pl.BlockSpec((b, BLK, W_C), lambda t: (0, t, 0))] + [_layer_spec(a, p["idx"]) for a in params],
        out_specs=[pl.BlockSpec((b, BLK, BW), lambda t: (0, t, 0)), pl.BlockSpec((b, BW), lambda t: (0, 0))],
        out_shape=[jax.ShapeDtypeStruct((b, lp, BW), _branch_dtype(prec)), jax.ShapeDtypeStruct((b, BW), F32)],
        scratch_shapes=[pltpu.VMEM((b, BLK + 8, BW), F32), pltpu.VMEM((b, BLK, BW), F32),
                        pltpu.VMEM((b, BLK, BW), F32), pltpu.VMEM((b, BLK, BW), F32), pltpu.VMEM((b, BW), F32)],
        compiler_params=_cparams(("arbitrary",)),
        name="lru_prompt_p%d" % prec,
    )(sc3, *params)
    return o.reshape(b * lp, BW), hl


def _gla_kernel(sd_ref, tri_ref, ind_ref, gn_ref, m64_ref, o_ref, st_out_ref, st_ref, o_s, upd_s, *, prec):
    c = pl.program_id(1)

    @pl.when(c == 0)
    def _():
        st_ref[...] = jnp.zeros_like(st_ref)

    q = sd_ref[:, 0:128] * (DQA ** -0.5)
    k = sd_ref[:, 128:256]
    v = sd_ref[:, 256:512]
    la = sd_ref[:, 768:896]
    tri = tri_ref[...]
    bc = sum(_bdot(tri, part) for part in _split3(la))
    ind = ind_ref[...]
    t_idx = lax.broadcasted_iota(jnp.int32, (GLA_CHUNK, 1), 0)
    row_head = lax.broadcasted_iota(jnp.int32, (BW, 128), 0) // 64
    col_head = lax.broadcasted_iota(jnp.int32, (BW, 128), 1) // 32
    bd = row_head == col_head
    n_sub = BLK // GLA_CHUNK

    def local_cumsum(s):
        r0 = s * GLA_CHUNK
        bcs = bc[r0:r0 + GLA_CHUNK]
        return bcs - bc[r0 - 1:r0] if s > 0 else bcs

    def run(out_prec):
        for s in range(n_sub):
            r0 = s * GLA_CHUNK
            bcs = local_cumsum(s)
            qs, ks, vs = q[r0:r0 + GLA_CHUNK], k[r0:r0 + GLA_CHUNK], v[r0:r0 + GLA_CHUNK]
            zs = []
            for j in range(GLA_CHUNK):
                dec = jnp.where(t_idx >= j, jnp.exp(jnp.minimum(bcs - bcs[j:j + 1], 0.0)), 0.0)
                zs.append(qs * dec * ks[j:j + 1])
            z = jnp.concatenate(zs, axis=0)
            if out_prec == 1:
                z_hi, z_lo = _split(z)
                aa = _bdot(z_hi, ind) + _bdot(z_lo, ind)
            else:
                aa = _bdot(z.astype(BF16), ind)
            o = aa[0:GLA_CHUNK] * vs[0:1]
            for j in range(1, GLA_CHUNK):
                o = o + aa[j * GLA_CHUNK:(j + 1) * GLA_CHUNK] * vs[j:j + 1]
            o_s[r0:r0 + GLA_CHUNK, :] = o
            last = bcs[GLA_CHUNK - 1:GLA_CHUNK]
            upd_s[s] = jnp.where(bd, _adot(vs, ks * jnp.exp(last - bcs), prec, TN), 0.0)
        st = st_ref[...]
        for s in range(n_sub):
            r0 = s * GLA_CHUNK
            bcs = local_cumsum(s)
            o_s[r0:r0 + GLA_CHUNK, :] += _adot(q[r0:r0 + GLA_CHUNK] * jnp.exp(bcs), st, out_prec, NT)
            st = st * jnp.exp(bcs[GLA_CHUNK - 1:GLA_CHUNK]) + upd_s[s]
        st_ref[...] = st
        st_out_ref[0] = st
        o = o_s[...]
        ms = _seg_mean(o * o, m64_ref[...])
        o_ref[...] = (o * lax.rsqrt(ms + EPS) * gn_ref[0] * _silu(sd_ref[:, 512:768])).astype(o_ref.dtype)

    if prec == 1:
        is_last = c == pl.num_programs(1) - 1
        pl.when(is_last)(lambda: run(1))
        pl.when(jnp.logical_not(is_last))(lambda: run(0))
    else:
        run(prec)


def _gla_prompt(sd, p, b, lp, prec):
    nc = lp // BLK
    t = np.arange(BLK)
    tri = jnp.asarray(t[:, None] >= t[None, :], BF16)
    ind = jnp.asarray((np.arange(128)[:, None] // 32) == (np.arange(BW)[None, :] // 64), BF16)
    return pl.pallas_call(
        functools.partial(_gla_kernel, prec=prec),
        grid=(b, nc),
        in_specs=[pl.BlockSpec((BLK, W_D), lambda bi, c: (bi * nc + c, 0)),
                  _whole_spec(tri), _whole_spec(ind), _layer_spec(p["gn_t"], p["idx"]), _whole_spec(p["m64"])],
        out_specs=[pl.BlockSpec((BLK, BW), lambda bi, c: (bi * nc + c, 0)),
                   pl.BlockSpec((1, BW, 128), lambda bi, c: (bi, 0, 0))],
        out_shape=[jax.ShapeDtypeStruct((b * lp, BW), _branch_dtype(prec)), jax.ShapeDtypeStruct((b, BW, 128), F32)],
        scratch_shapes=[pltpu.VMEM((BW, 128), F32), pltpu.VMEM((BLK, BW), F32),
                        pltpu.VMEM((BLK // GLA_CHUNK, BW, 128), F32)],
        compiler_params=_cparams(("parallel", "arbitrary")),
        name="gla_prompt_p%d" % prec,
    )(sd, tri, ind, p["gn_t"], p["m64"])


def _merge_kernel(fine_ref, x_ref, g_ref, oa_ref, ob_ref, oc_ref, od_ref, wm_ref, bm_ref, wbr_ref, wo_ref, y_ref,
                  *, prec):
    def body(level):
        x = x_ref[...]
        xs = _operand(_rms(x, g_ref[0]), level)
        acc = None
        for n, br in enumerate((oa_ref, ob_ref, oc_ref, od_ref)):
            cols = slice(n * D_MODEL, (n + 1) * D_MODEL)
            gate = _sigmoid(_wdot(xs, wm_ref, (slice(None), cols), level) + bm_ref[0, :, cols])
            term = gate * _wdot(br[...], wbr_ref, (n,), level)
            acc = term if acc is None else acc + term
        y_ref[...] = x + _wdot(acc, wo_ref, (), level)

    if prec == 1:
        fine = fine_ref[pl.program_id(0)]
        pl.when(fine != 0)(lambda: body(1))
        pl.when(fine == 0)(lambda: body(0))
    else:
        body(prec)


def _merge(x, oa, ob, oc, od, p, tm, prec, fine_rows=None):
    n = x.shape[0]
    layer = p["idx"]
    row = lambda w: pl.BlockSpec((tm, w), lambda i, f: (i, 0))
    params = [p["w_merge"], p["b_merge"], p["w_branch"], p["w_out"]]
    grid_spec = pltpu.PrefetchScalarGridSpec(
        num_scalar_prefetch=1,
        grid=(n // tm,),
        in_specs=[row(D_MODEL), _layer_spec(p["norm1"], layer), row(BW), row(BW), row(BW), row(BW)]
                 + [_layer_spec(a, layer) for a in params],
        out_specs=row(D_MODEL),
    )
    return pl.pallas_call(
        functools.partial(_merge_kernel, prec=prec),
        grid_spec=grid_spec,
        out_shape=jax.ShapeDtypeStruct((n, D_MODEL), F32),
        compiler_params=_cparams(("parallel",)),
        name="merge_p%d" % prec,
    )(_tile_flags(n // tm, tm, fine_rows), x, p["norm1"], oa, ob, oc, od, *params)


def _first_argmax(vals, valid, lane):
    v = jnp.where(valid, vals, NEG)
    top = jnp.max(v, axis=-1, keepdims=True)
    idx = jnp.min(jnp.where(valid & (v == top), lane, 1 << 20), axis=-1, keepdims=True)
    return top, idx


def _router_gate(lg, le):
    lane = lax.broadcasted_iota(jnp.int32, lg.shape, 1)
    gvalid = lane < N_GROUPS
    lgm = jnp.where(gvalid, lg, NEG)
    pe = jnp.exp(lgm - jnp.max(lgm, axis=-1, keepdims=True))
    pg = pe / jnp.sum(pe, axis=-1, keepdims=True)
    pg_top, g_idx = _first_argmax(pg, gvalid, lane)
    sel = (lane // EXP_PER_GROUP == g_idx) & (lane < N_EXPERTS)
    v1, i1 = _first_argmax(le, sel, lane)
    v2, i2 = _first_argmax(le, sel & (lane != i1), lane)
    e2 = jnp.exp(v2 - v1)
    den = 1.0 + e2
    return jnp.where(lane == i1, pg_top / den, jnp.where(lane == i2, pg_top * e2 / den, 0.0))


def _moe_kernel(h_ref, g_ref, rw_ref, rb_ref, sel_ref, w1_ref, w3_ref, w2_ref, y_ref, t_s, gate_s, *, prec):
    grp = pl.program_id(1)

    @pl.when(grp == 0)
    def _():
        x = h_ref[...]
        t = _rms(x, g_ref[0])
        t_s[...] = t.astype(t_s.dtype)
        logits = _wdot(t, rw_ref, (), max(prec, 1)) + rb_ref[0]
        gate_s[...] = _router_gate(logits[:, 0:128], logits[:, 128:256])
        y_ref[...] = x

    t = t_s[...]
    gcols = _seg_mean(gate_s[...], sel_ref[grp], prec)
    hs = []
    for j in range(EXP_PER_GROUP):
        gj = gcols[:, 128 * j:128 * (j + 1)]
        hj = _silu(_wdot(t, w1_ref, (j,), prec)) * _wdot(t, w3_ref, (j,), prec)
        hs.append(hj * jnp.concatenate([gj, gj], axis=1))
    acc = _wdot(hs[0], w2_ref, (0,), prec)
    for j in range(1, EXP_PER_GROUP):
        acc = acc + _wdot(hs[j], w2_ref, (j,), prec)
    y_ref[...] += acc


def _moe(h, p, tm, prec):
    n = h.shape[0]
    layer = p["idx"]
    wspec = lambda a: pl.BlockSpec((1, a.shape[1], EXP_PER_GROUP) + a.shape[3:],
                                   lambda i, g: (layer, 0, g, 0, 0))
    sel = np.zeros((N_GROUPS, 128, EXP_PER_GROUP * 128), np.float32)
    for g in range(N_GROUPS):
        for j in range(EXP_PER_GROUP):
            sel[g, EXP_PER_GROUP * g + j, 128 * j:128 * (j + 1)] = 1.0
    sel = jnp.asarray(sel)
    return pl.pallas_call(
        functools.partial(_moe_kernel, prec=prec),
        grid=(n // tm, N_GROUPS),
        in_specs=[pl.BlockSpec((tm, D_MODEL), lambda i, g: (i, 0)), _layer_spec(p["norm2"], layer),
                  _layer_spec(p["rw"], layer), _layer_spec(p["rb"], layer), _whole_spec(sel),
                  wspec(p["ex_w1"]), wspec(p["ex_w3"]), wspec(p["ex_w2"])],
        out_specs=pl.BlockSpec((tm, D_MODEL), lambda i, g: (i, 0)),
        out_shape=jax.ShapeDtypeStruct((n, D_MODEL), F32),
        scratch_shapes=[pltpu.VMEM((tm, D_MODEL), F32 if prec == 2 else BF16), pltpu.VMEM((tm, 128), F32)],
        compiler_params=_cparams(("parallel", "arbitrary")),
        name="moe_p%d" % prec,
    )(h, p["norm2"], p["rw"], p["rb"], sel, p["ex_w1"], p["ex_w3"], p["ex_w2"])


def _sublane_total(x):
    for shift in (4, 2, 1):
        x = x + pltpu.roll(x, shift, 0)
    return x


def _lane_all(x, op):
    return jnp.broadcast_to(op(x, axis=-1, keepdims=True), x.shape)


def _dec_attn_kernel(pt_ref, q_ref, kn_ref, vn_ref, lam_ref, *rest, pp, lam_init):
    del pt_ref
    kp, vp = rest[:pp], rest[pp:2 * pp]
    o_ref, s_s, m_s, l_s, acc_s = rest[2 * pp:]
    step = pl.program_id(1)

    @pl.when(step == 0)
    def _():
        m_s[...] = jnp.full_like(m_s, NEG)
        l_s[...] = jnp.zeros_like(l_s)
        acc_s[...] = jnp.zeros_like(acc_s)

    def map_scores(t, m):
        part = t[4 * m] + t[4 * m + 1] + t[4 * m + 2] + t[4 * m + 3]
        return _sublane_total(part)

    tops = [None] * (2 * HEADS)
    for j in range(pp):
        for h in range(HEADS):
            t = kp[j][0, 0, h] * q_ref[0, h]
            for m in range(2):
                g = 2 * h + m
                sc = map_scores(t, m)
                s_s[g, j] = sc
                tops[g] = sc if tops[g] is None else jnp.maximum(tops[g], sc)
    for h in range(HEADS):
        m_new, accs, sums = [], [], []
        for m in range(2):
            g = 2 * h + m
            m_old = m_s[g]
            m_new.append(jnp.maximum(m_old, _lane_all(tops[g], jnp.max)))
            alpha = jnp.exp(m_old - m_new[m])
            accs.append(acc_s[g] * alpha)
            sums.append(None)
            l_s[g] = l_s[g] * alpha
            m_s[g] = m_new[m]
        for j in range(pp):
            vt = vp[j][0, 0, h]
            for m in range(2):
                pr = jnp.exp(s_s[2 * h + m, j] - m_new[m])
                accs[m] = accs[m] + vt * pr
                sums[m] = pr if sums[m] is None else sums[m] + pr
        for m in range(2):
            g = 2 * h + m
            acc_s[g] = accs[m]
            l_s[g] = l_s[g] + _lane_all(sums[m], jnp.sum)

    @pl.when(step == pl.num_programs(1) - 1)
    def _():
        lam = _lam_scalar(lam_ref[0], lam_init)
        for h in range(HEADS):
            t = q_ref[0, h] * kn_ref[0, h]
            maps = []
            for m in range(2):
                g = 2 * h + m
                sc = map_scores(t, m)
                m_o = m_s[g][:, 0:1]
                m_n = jnp.maximum(m_o, sc)
                al = jnp.exp(m_o - m_n)
                pn = jnp.exp(sc - m_n)
                num = al * jnp.sum(acc_s[g], axis=-1, keepdims=True) + pn * vn_ref[0, h]
                maps.append(num / (al * l_s[g][:, 0:1] + pn))
            o_ref[0, h] = maps[0] - lam * maps[1]


def _dec_attn(qa, ka, va, cache_k, cache_v, page_table, p):
    db = qa.shape[0]
    layer = p["layer"]
    n_pages = page_table.shape[1]
    n_phys = cache_k.shape[1]
    pp = math.gcd(PAGES_PER_STEP, n_pages)
    tiles = lambda c: c.transpose(0, 1, 3, 4, 2).reshape(c.shape[0], n_phys, HEADS, 8, 8, PAGE)
    ck, cv = tiles(cache_k), tiles(cache_v)
    page = lambda j: pl.BlockSpec((1, 1, HEADS, 8, 8, PAGE),
                                  lambda b, s, pt: (layer, pt[b, s * pp + j], 0, 0, 0, 0))
    col = pl.BlockSpec((1, HEADS, 8, 8, 1), lambda b, s, pt: (b, 0, 0, 0, 0))
    lam = p["lam"]
    grid_spec = pltpu.PrefetchScalarGridSpec(
        num_scalar_prefetch=1,
        grid=(db, n_pages // pp),
        in_specs=[col, col, col, pl.BlockSpec((1,) + lam.shape[1:], lambda b, s, pt: (p["idx"], 0, 0))]
                 + [page(j) for j in range(pp)] + [page(j) for j in range(pp)],
        out_specs=col,
        scratch_shapes=[pltpu.VMEM((2 * HEADS, pp, 8, PAGE), F32), pltpu.VMEM((2 * HEADS, 8, PAGE), F32),
                        pltpu.VMEM((2 * HEADS, 8, PAGE), F32), pltpu.VMEM((2 * HEADS, 8, 8, PAGE), F32)],
    )
    cols = lambda x: x.reshape(db, HEADS, 8, 8, 1)
    out = pl.pallas_call(
        functools.partial(_dec_attn_kernel, pp=pp, lam_init=_lam_init(layer)),
        grid_spec=grid_spec,
        out_shape=jax.ShapeDtypeStruct((db, HEADS, 8, 8, 1), F32),
        compiler_params=_cparams(("parallel", "arbitrary")),
        name="attn_sample",
    )(page_table, cols(qa), cols(ka), cols(va), lam, *([ck] * pp), *([cv] * pp))
    return out.reshape(db * HEADS, 64).T


def _step_kernel(oa_ref, an_ref,
                 bq1_ref, bq2_ref, bk1_ref, bk2_ref, bv_ref, bg_ref, s0_ref, cos_ref, sin_ref, gam_ref, rn_ref,
                 dq_ref, dk_ref, dv_ref, dr_ref, la_ref, g0_ref, gn_ref,
                 cx_ref, cg_ref, buf_ref, h0_ref, cw_ref, cb_ref, wa_ref, wx_ref, ba_ref, bx_ref, lam_ref,
                 oa_o, ob_o, od_o, oc_o, s_o, g_o, h_o, *, lam_init):
    x = oa_ref[...]
    oa_o[...] = x * lax.rsqrt(jnp.mean(x * x, axis=0, keepdims=True) + EPS) * an_ref[0] * (1.0 - lam_init)

    cos, sin, gam = cos_ref[...], sin_ref[...], gam_ref[...]
    rot = lambda x1, x2: jnp.concatenate([x1 * cos - x2 * sin, x1 * sin + x2 * cos], axis=0)
    qb = rot(bq1_ref[...], bq2_ref[...])
    kb = rot(bk1_ref[...], bk2_ref[...]) * 0.125
    v = bv_ref[...]
    o = jnp.sum(qb * kb, axis=0, keepdims=True) * v
    st = jnp.zeros_like(v)
    for k in range(64):
        s0k = s0_ref[k]
        st = st + s0k * qb[k:k + 1]
        s_o[k] = gam * s0k + kb[k:k + 1] * v
    o = o + gam * st
    mu = jnp.mean(o, axis=0, keepdims=True)
    var = jnp.mean((o - mu) * (o - mu), axis=0, keepdims=True)
    ob_o[...] = (o - mu) * lax.rsqrt(var + EPS) * rn_ref[0] * _silu(bg_ref[...])

    qd = dq_ref[...] * (DQA ** -0.5)
    kd = dk_ref[...]
    vd = dv_ref[...]
    dec = jnp.exp(la_ref[...])
    o = jnp.sum(qd * kd, axis=0, keepdims=True) * vd
    qe = qd * dec
    for k in range(32):
        g0k = g0_ref[k]
        o = o + g0k * qe[k:k + 1]
        g_o[k] = dec[k:k + 1] * g0k + kd[k:k + 1] * vd
    od_o[...] = o * lax.rsqrt(jnp.mean(o * o, axis=0, keepdims=True) + EPS) * gn_ref[0] * _silu(dr_ref[...])

    cx = cx_ref[...]
    y = cb_ref[0] + cx * cw_ref[0, CONV_W - 1:CONV_W, :]
    for i in range(CONV_W - 1):
        y = y + buf_ref[i] * cw_ref[0, i:i + 1, :]
    a, u = _lru_gates(y, wa_ref, wx_ref, ba_ref[0], bx_ref[0], _softplus(-lam_ref[0]), 2)
    h = a * h0_ref[...] + u
    h_o[...] = h
    oc_o[...] = h * _gelu_tanh(cg_ref[...])


def _to_fm(x, db, width):
    return x.reshape(db, HEADS, width).transpose(2, 0, 1).reshape(width, db * HEADS)


def _from_fm(x, db, width):
    return x.reshape(width, db, HEADS).transpose(1, 2, 0).reshape(db, HEADS * width)


def _halves_fm(x, db):
    x4 = x.reshape(db, HEADS, 2, 32).transpose(2, 3, 0, 1).reshape(2, 32, db * HEADS)
    return x4[0], x4[1]


def _sample_step(oa_fm, sb, sc, sd, ret0, gla0, conv_buf, h0, p, pos):
    db = sb.shape[0]
    n = db * HEADS
    layer = p["idx"]
    cos, sin = _rope_cos_sin(jnp.asarray([pos]))
    gam = jnp.tile(jnp.asarray(_ret_gammas(), F32), db)[None, :]
    bq1, bq2 = _halves_fm(sb[:, 0:BW], db)
    bk1, bk2 = _halves_fm(sb[:, BW:2 * BW], db)
    args = [
        oa_fm, p["an_c"],
        bq1, bq2, bk1, bk2, _to_fm(sb[:, 2 * BW:3 * BW], db, 64), _to_fm(sb[:, 3 * BW:4 * BW], db, 64),
        ret0.transpose(2, 3, 0, 1).reshape(64, 64, n), cos.reshape(32, 1), sin.reshape(32, 1), gam, p["rn_c"],
        _to_fm(sd[:, 0:128], db, 32), _to_fm(sd[:, 128:256], db, 32), _to_fm(sd[:, 256:512], db, 64),
        _to_fm(sd[:, 512:768], db, 64), _to_fm(sd[:, 768:896], db, 32),
        gla0.transpose(2, 3, 0, 1).reshape(32, 64, n), p["gn_c"],
        sc[:, 0:BW], sc[:, BW:2 * BW], conv_buf.transpose(1, 0, 2), h0,
        p["conv_w"], p["conv_b"], p["lru_wa_bd"], p["lru_wx_bd"], p["lru_ba"], p["lru_bx"], p["lru_lam"],
    ]
    layered = {1, 12, 19, 24, 25, 26, 27, 28, 29, 30}
    in_specs = [_layer_spec(a, layer) if i in layered else _whole_spec(a) for i, a in enumerate(args)]
    outs = [jax.ShapeDtypeStruct((64, n), F32)] * 3 + [jax.ShapeDtypeStruct((db, BW), F32),
            jax.ShapeDtypeStruct((64, 64, n), F32), jax.ShapeDtypeStruct((32, 64, n), F32),
            jax.ShapeDtypeStruct((db, BW), F32)]
    oa, ob, od, oc, s_new, g_new, h_new = pl.pallas_call(
        functools.partial(_step_kernel, lam_init=_lam_init(p["layer"])),
        grid=(1,),
        in_specs=in_specs,
        out_specs=[pl.BlockSpec(o.shape, lambda i, nd=len(o.shape): (0,) * nd) for o in outs],
        out_shape=outs,
        compiler_params=_cparams(("arbitrary",)),
        name="step_sample",
    )(*args)
    ret_new = s_new.reshape(64, 64, db, HEADS).transpose(2, 3, 0, 1)
    gla_new = g_new.reshape(32, 64, db, HEADS).transpose(2, 3, 0, 1)
    return _from_fm(oa, db, 64), _from_fm(ob, db, 64), oc, _from_fm(od, db, 64), ret_new, h_new, gla_new


def _block_diag(w):
    d, h, n, _ = w.shape
    eye = jnp.eye(h, dtype=w.dtype)
    return jnp.einsum("dhij,hg->dhigj", w, eye).reshape(d, h * n, h * n)


def _seg_matrix(seg):
    idx = np.arange(BW)
    return jnp.asarray((idx[:, None] // seg == idx[None, :] // seg) / float(seg), F32)


def _parts(w, prec, with_cast=False):
    if prec == 2:
        return w[:, None]
    if prec == 0:
        return w.astype(BF16)[:, None]
    return jnp.stack(_split_bits(w) + ((w.astype(BF16),) if with_cast else ()), axis=1)


def _prepare(raw, layers, prec, moe_prec, layer):
    sl = lambda a: a[layers]
    (norm1, w_in, qn, kn, lam, an, rn, conv_w, conv_b, lru_wa, lru_ba, lru_wx, lru_bx, lru_lam, gla_w2, gla_b,
     gn, w_branch, w_merge, b_merge, w_out, norm2, rg_w, rg_b, re_w, re_b, ex_w1, ex_w3, ex_w2) = raw
    nl = sl(norm1).shape[0]
    r3 = lambda a: sl(a).reshape(nl, 1, -1)
    tile3 = lambda a, k: jnp.tile(sl(a), (1, k)).reshape(nl, 1, -1)
    pad_cols = lambda a, w: jnp.pad(a, ((0, 0), (0, 0), (0, w - a.shape[-1])))
    w_in = sl(w_in)
    c = np.cumsum([0, 768, 1024, 512, 784])
    rw = jnp.concatenate([pad_cols(sl(rg_w), 128), pad_cols(sl(re_w), 128)], axis=-1)
    return dict(
        idx=layer if nl > 1 else 0, layer=layer,
        norm1=r3(norm1), qn_t=tile3(qn, 8), kn_t=tile3(kn, 8), lam=sl(lam), an_t=tile3(an, 4), rn_t=tile3(rn, 4),
        gn_t=tile3(gn, 4), an_c=sl(an).reshape(nl, 64, 1), rn_c=sl(rn).reshape(nl, 64, 1),
        gn_c=sl(gn).reshape(nl, 64, 1), conv_w=sl(conv_w), conv_b=r3(conv_b), lru_ba=r3(lru_ba),
        lru_bx=r3(lru_bx), lru_lam=r3(lru_lam), gla_b=r3(gla_b), b_merge=r3(b_merge), norm2=r3(norm2),
        rb=jnp.concatenate([pad_cols(r3(rg_b), 128), pad_cols(r3(re_b), 128)], axis=-1),
        m32=_seg_matrix(32), m64=_seg_matrix(64),
        wa=_parts(w_in[:, :, c[0]:c[1]], prec, True), wb=_parts(w_in[:, :, c[1]:c[2]], prec, True),
        wc=_parts(w_in[:, :, c[2]:c[3]], prec, True),
        wd=_parts(pad_cols(w_in[:, :, c[3]:c[4]], W_D), prec, True),
        gla_w2p=_parts(jnp.pad(sl(gla_w2), ((0, 0), (0, 128 - GLA_RANK), (0, 0))), prec),
        lru_wa_bd=_parts(_block_diag(sl(lru_wa)), prec), lru_wx_bd=_parts(_block_diag(sl(lru_wx)), prec),
        w_merge=_parts(sl(w_merge), prec, True), w_branch=_parts(sl(w_branch), prec, True),
        w_out=_parts(sl(w_out), prec, True),
        rw=_parts(rw, max(moe_prec, 1)),
        ex_w1=_parts(sl(ex_w1), moe_prec), ex_w3=_parts(sl(ex_w3), moe_prec), ex_w2=_parts(sl(ex_w2), moe_prec),
    )


def _diag_blocks(s, rows, cols):
    b = s.shape[0]
    s5 = s.reshape(b, HEADS, rows, HEADS, cols)
    return jnp.stack([s5[:, h, :, h, :] for h in range(HEADS)], axis=1)


def _prompt_tile(n, prec):
    for tm in ((256, 128) if prec else (512, 256, 128)):
        if n % tm == 0:
            return tm
    raise ValueError(n)


def kernel(x_prompt, x_sample, cache_k_a, cache_v_a, state_ret, state_lru_h, state_lru_conv, state_gla,
           page_table, meta, norm1, w_in, qn, kn, lam, an, rn, conv_w, conv_b, lru_wa, lru_ba, lru_wx,
           lru_bx, lru_lam, gla_w2, gla_b, gn, w_branch, w_merge, b_merge, w_out, norm2, rg_w, rg_b,
           re_w, re_b, ex_w1, ex_w3, ex_w2):
    bp, seq, _ = x_prompt.shape
    db = x_sample.shape[0]
    depth = w_in.shape[0]
    assert seq % BLK == 0 and x_sample.shape[1] == 1
    lp = BLK + seq
    past = page_table.shape[1] * PAGE
    raw = (norm1, w_in, qn, kn, lam, an, rn, conv_w, conv_b, lru_wa, lru_ba, lru_wx, lru_bx, lru_lam, gla_w2,
           gla_b, gn, w_branch, w_merge, b_merge, w_out, norm2, rg_w, rg_b, re_w, re_b, ex_w1, ex_w3, ex_w2)
    head = jnp.concatenate([jnp.zeros((PADL, D_MODEL), F32), meta.astype(F32)], axis=0)
    xp = jnp.concatenate([jnp.broadcast_to(head[None], (bp, BLK, D_MODEL)), x_prompt], axis=1)
    xp = xp.reshape(bp * lp, D_MODEL)
    xs = x_sample.reshape(db, D_MODEL)
    new_p, new_s = [], []
    for layer in range(depth):
        prec = 1 if layer == 0 else 0
        tm = _prompt_tile(bp * lp, prec)
        p = _prepare(raw, slice(layer, layer + 1), prec, 0, layer)
        tail_rows = [b * lp + lp - 1 - r for b in range(bp) for r in range(CONV_W - 1)]
        ka, va, sb, sc, sd, qp, kp, vp = _in_proj(xp, p, tm, prec, lp, tail_rows)
        oa = _attn_prompt(qp, kp, vp, p, bp, lp, prec)
        ob, s_ret = _ret_prompt(sb, p, bp, lp, prec)
        oc, h_last = _lru_prompt(sc, p, bp, lp, prec)
        od, s_gla = _gla_prompt(sd, p, bp, lp, prec)
        hres = _merge(xp, oa, ob, oc, od, p, tm, prec, tail_rows)
        xp = _moe(hres, p, _prompt_tile(bp * lp, 0), 0)
        new_p.append((ka.reshape(bp, lp, HEADS, 64)[:, PADL:], va.reshape(bp, lp, HEADS, 64)[:, PADL:],
                      _diag_blocks(s_ret, 64, 64), h_last,
                      sc.reshape(bp, lp, W_C)[:, lp - (CONV_W - 1):, 0:BW],
                      _diag_blocks(s_gla, 64, 32).transpose(0, 1, 3, 2)))
        ps = _prepare(raw, slice(None), 2, 2, layer)
        ka, va, sb, sc, sd, qa = _in_proj(xs, ps, db, 2)
        oa_fm = _dec_attn(qa, ka, va, cache_k_a, cache_v_a, page_table, ps)
        oa, ob, oc, od, ret_new, h_new, gla_new = _sample_step(
            oa_fm, sb, sc, sd, state_ret[layer], state_gla[layer], state_lru_conv[layer], state_lru_h[layer],
            ps, past)
        hres = _merge(xs, oa, ob, oc, od, ps, db, 2)
        xs = _moe(hres, ps, db, 2)
        conv_new = jnp.concatenate([state_lru_conv[layer][:, 1:], sc[:, None, 0:BW]], axis=1)
        new_s.append((ka.reshape(db, 1, HEADS, 64), va.reshape(db, 1, HEADS, 64), ret_new, h_new, conv_new,
                      gla_new))
    stk = lambda states, i: jnp.stack([s[i] for s in states])
    y_prompt = xp.reshape(bp, lp, D_MODEL)[:, BLK:]
    return ((y_prompt, xs.reshape(db, 1, D_MODEL))
            + tuple(stk(new_p, i) for i in range(6)) + tuple(stk(new_s, i) for i in range(6)))
```
